```python
import math
import jax
import jax.numpy as jnp
from jax import lax
import numpy as np

D_MODEL = 1024
BATCH = 8
SEQ = 8192
DEPTH = 2

GRID_W = 64
CTX_LEN = 256
EPS = 1e-6

MLSTM_HEADS = 8
MLSTM_DQK = 64
MLSTM_DV = 64
MLSTM_CONV = 3
MLSTM_CHUNK = 64
DIFF_HEADS = 4
DIFF_DQK = 64
DIFF_DV = 128
ROPE_BASE = 10000.0
Q_BLOCK = 128
CONV_WIDTH = 31
D_FF = 2816
N_EXPERTS = 8
TOP_K = 2
D_FF_EXPERT = 2816
MOE_BLOCK = 128

MQK = MLSTM_HEADS * MLSTM_DQK
MV = MLSTM_HEADS * MLSTM_DV
MG = 4 * MLSTM_HEADS
DQK = DIFF_HEADS * 2 * DIFF_DQK
DVW = DIFF_HEADS * DIFF_DV
OFF_MQ = 0
OFF_MK = OFF_MQ + MQK
OFF_MV = OFF_MK + MQK
OFF_MO = OFF_MV + MV
OFF_MG = OFF_MO + MV
OFF_DQ = OFF_MG + MG
OFF_DK = OFF_DQ + DQK
OFF_DV = OFF_DK + DQK
IN0_WIDTH = OFF_DV + DVW
MIX0_WIDTH = MV + DVW

kernel_name = 'hybrid_mlstm_diffattn_conformer_moe_dit'

F32 = jnp.float32


def rmsnorm(x, g):
    xf = x.astype(F32)
    y = xf * lax.rsqrt(jnp.mean(xf * xf, axis=-1, keepdims=True) + EPS)
    return (y * g.astype(F32)).astype(x.dtype)


def layernorm(x, g, b):
    xf = x.astype(F32)
    mu = jnp.mean(xf, axis=-1, keepdims=True)
    var = jnp.mean(jnp.square(xf - mu), axis=-1, keepdims=True)
    return ((xf - mu) * lax.rsqrt(var + EPS) * g.astype(F32) + b.astype(F32)).astype(x.dtype)


def adaln(cond, w, b):
    return jax.nn.silu(cond) @ w + b


def modulate(x, g, shift, scale):
    return rmsnorm(x, g) * (1 + scale) + shift


def depthwise_conv(x, w):
    k = w.shape[0]
    return lax.conv_general_dilated(
        x, w[:, None, :].astype(x.dtype), window_strides=(1,),
        padding=[((k - 1) // 2, k // 2)], dimension_numbers=('NWC', 'WIO', 'NWC'),
        feature_group_count=x.shape[-1])


def to_heads(a, n_heads):
    b, t, _ = a.shape
    return a.reshape(b, t, n_heads, -1).transpose(0, 2, 1, 3)


def axial_rope_tables(t_len):
    rows = t_len // GRID_W
    row = jnp.repeat(jnp.arange(rows, dtype=F32), GRID_W)
    col = jnp.tile(jnp.arange(GRID_W, dtype=F32), rows)
    axis_dim = DIFF_DQK // 2
    inv = ROPE_BASE ** (-jnp.arange(0, axis_dim, 2, dtype=F32) / axis_dim)
    ang_r = row[:, None] * inv
    ang_c = col[:, None] * inv
    return jnp.cos(ang_r), jnp.sin(ang_r), jnp.cos(ang_c), jnp.sin(ang_c)


def rotate(x, cos, sin):
    x1, x2 = jnp.split(x, 2, axis=-1)
    return jnp.concatenate([x1 * cos - x2 * sin, x2 * cos + x1 * sin], axis=-1)


def apply_axial_rope(x, tables):
    cr, sr, cc, sc = tables
    xr, xc = jnp.split(x.astype(F32), 2, axis=-1)
    return jnp.concatenate([rotate(xr, cr, sr), rotate(xc, cc, sc)], axis=-1).astype(x.dtype)


def mlstm_gates(g, gate_b):
    g = (g + gate_b).astype(F32)
    b, t, _ = g.shape
    g = g.reshape(b, t, 4, MLSTM_HEADS).transpose(2, 0, 3, 1)
    return g[0], jax.nn.log_sigmoid(g[1]), g[2], jax.nn.log_sigmoid(g[3])


def mlstm_state_update(C, n, m, k, v, ig, b):
    b_last = b[..., -1]
    log_s = b_last[..., None] - b + ig
    m_new = jnp.maximum(b_last + m, jnp.max(log_s, axis=-1))
    ws = jnp.exp(log_s - m_new[..., None])
    decay = jnp.exp(b_last + m - m_new)
    C_new = decay[..., None, None] * C + jnp.einsum('bhs,bhsd,bhse->bhde', ws, k, v)
    n_new = decay[..., None] * n + jnp.einsum('bhs,bhsd->bhd', ws, k)
    return C_new, n_new, m_new


def mlstm_chunk(carry, inp):
    C, n, m = carry
    q, k, v, ig, lf = inp
    L = q.shape[2]
    b = jnp.cumsum(lf, axis=-1)
    order = jnp.tril(jnp.ones((L, L), dtype=bool))
    log_d = jnp.where(order, b[..., :, None] - b[..., None, :] + ig[..., None, :], -jnp.inf)
    log_inter = b + m[..., None]
    m_t = jnp.maximum(log_inter, jnp.max(log_d, axis=-1))
    s = jnp.einsum('bhtd,bhsd->bhts', q, k) * jnp.exp(log_d - m_t[..., None])
    w_inter = jnp.exp(log_inter - m_t)
    num = w_inter[..., None] * jnp.einsum('bhtd,bhde->bhte', q, C) + jnp.einsum('bhts,bhse->bhte', s, v)
    den = w_inter * jnp.einsum('bhtd,bhd->bht', q, n) + jnp.sum(s, axis=-1)
    h = num / jnp.maximum(jnp.abs(den), jnp.exp(-m_t))[..., None]
    return mlstm_state_update(C, n, m, k, v, ig, b), h


def mlstm_scan(q, k, v, ig, lf, state):
    b, h, t, _ = q.shape
    nc = t // MLSTM_CHUNK

    def chunks(a):
        return jnp.moveaxis(a.reshape(a.shape[:2] + (nc, MLSTM_CHUNK) + a.shape[3:]), 2, 0)

    _, out = lax.scan(mlstm_chunk, state, (chunks(q), chunks(k), chunks(v), chunks(ig), chunks(lf)))
    return jnp.moveaxis(out, 0, 2).reshape(b, h, t, -1)


def mlstm_context_state(k, v, ig, lf):
    b, h, _, dk = k.shape
    C0 = jnp.zeros((b, h, dk, v.shape[-1]), F32)
    n0 = jnp.zeros((b, h, dk), F32)
    m0 = jnp.zeros((b, h), F32)
    return mlstm_state_update(C0, n0, m0, k, v, ig, jnp.cumsum(lf, axis=-1))


def mlstm_mixer(q, k, v, o, ig_f, lf_f, ig_b, lf_b, k_c, v_c, igc_f, lfc_f, igc_b, lfc_b, norm_g):
    H = MLSTM_HEADS
    B_, T_, _ = q.shape
    q = to_heads(q, H).astype(F32) * (MLSTM_DQK ** -0.5)
    k = to_heads(k, H).astype(F32)
    v = to_heads(v, H).astype(F32)
    k_c = to_heads(k_c, H).astype(F32)
    v_c = to_heads(v_c, H).astype(F32)
    st_f = mlstm_context_state(k_c, v_c, igc_f, lfc_f)
    st_b = mlstm_context_state(jnp.flip(k_c, 2), jnp.flip(v_c, 2), jnp.flip(igc_b, -1), jnp.flip(lfc_b, -1))
    h_f = mlstm_scan(q, k, v, ig_f, lf_f, st_f)
    h_b = jnp.flip(mlstm_scan(jnp.flip(q, 2), jnp.flip(k, 2), jnp.flip(v, 2),
                              jnp.flip(ig_b, -1), jnp.flip(lf_b, -1), st_b), 2)
    h = (h_f + h_b).transpose(0, 2, 1, 3)
    h = rmsnorm(h, norm_g.reshape(H, MLSTM_DV)).reshape(B_, T_, MV)
    return (h * jax.nn.sigmoid(o.astype(F32))).astype(o.dtype)


def diff_attention_blocks(q1, q2, k1_all, k2_all, v_all, lam):
    B_, H, T_, d = q1.shape
    nb = T_ // Q_BLOCK
    scale = DIFF_DQK ** -0.5

    def blocks(a):
        return a.reshape(B_, H, nb, Q_BLOCK, d).transpose(2, 0, 1, 3, 4)

    def one_block(qs):
        a1, a2 = qs
        s1 = jnp.einsum('bhqd,bhkd->bhqk', a1, k1_all).astype(F32) * scale
        s2 = jnp.einsum('bhqd,bhkd->bhqk', a2, k2_all).astype(F32) * scale
        p = jax.nn.softmax(s1, axis=-1) - lam * jax.nn.softmax(s2, axis=-1)
        return jnp.einsum('bhqk,bhkv->bhqv', p.astype(v_all.dtype), v_all)

    out = lax.map(one_block, (blocks(q1), blocks(q2)))
    return out.transpose(1, 2, 0, 3, 4).reshape(B_, H, T_, -1)


def diff_mixer(q, k, v, k_c, v_c, lq1, lk1, lq2, lk2, norm_g, layer_idx):
    H = DIFF_HEADS
    B_, T_, _ = q.shape

    def split_qk(a):
        a = a.reshape(a.shape[0], a.shape[1], H, 2, DIFF_DQK)
        return a[..., 0, :].transpose(0, 2, 1, 3), a[..., 1, :].transpose(0, 2, 1, 3)

    tables = axial_rope_tables(T_)
    q1, q2 = [apply_axial_rope(a, tables) for a in split_qk(q)]
    k1, k2 = [apply_axial_rope(a, tables) for a in split_qk(k)]
    k1c, k2c = split_qk(k_c)
    k1_all = jnp.concatenate([k1c, k1], axis=2)
    k2_all = jnp.concatenate([k2c, k2], axis=2)
    v_all = jnp.concatenate([to_heads(v_c, H), to_heads(v, H)], axis=2)
    lam_init = 0.8 - 0.6 * math.exp(-0.3 * layer_idx)
    lam = (jnp.exp(jnp.sum(lq1.astype(F32) * lk1.astype(F32)))
           - jnp.exp(jnp.sum(lq2.astype(F32) * lk2.astype(F32))) + lam_init)
    o = diff_attention_blocks(q1, q2, k1_all, k2_all, v_all, lam).transpose(0, 2, 1, 3)
    o = rmsnorm(o, norm_g.reshape(H, DIFF_DV)) * (1.0 - lam_init)
    return o.reshape(B_, T_, DVW)


def swiglu(u, w1, w3, w2):
    return (jax.nn.silu(u @ w1) * (u @ w3)) @ w2


def moe_swiglu(u, router_w, w1, w3, w2):
    B_, T_, D = u.shape
    tokens = u.reshape(-1, D)
    N = tokens.shape[0]
    logits = (tokens @ router_w).astype(F32)
    top_vals, top_idx = lax.top_k(logits, TOP_K)
    gates = jax.nn.softmax(top_vals, axis=-1)
    A = N * TOP_K
    flat_e = top_idx.reshape(-1)
    flat_tok = jnp.repeat(jnp.arange(N, dtype=jnp.int32), TOP_K)
    flat_g = gates.reshape(-1)
    order = jnp.argsort(flat_e)
    s_e, s_tok, s_g = flat_e[order], flat_tok[order], flat_g[order]
    counts = jnp.bincount(flat_e, length=N_EXPERTS)
    starts = jnp.cumsum(counts) - counts
    padded = ((counts + MOE_BLOCK - 1) // MOE_BLOCK) * MOE_BLOCK
    pends = jnp.cumsum(padded)
    pstarts = pends - padded
    dest = pstarts[s_e] + (jnp.arange(A) - starts[s_e])
    n_blk = (A + MOE_BLOCK - 1) // MOE_BLOCK + N_EXPERTS
    R = n_blk * MOE_BLOCK
    slot_tok = jnp.zeros((R,), jnp.int32).at[dest].set(s_tok)
    slot_g = jnp.zeros((R,), F32).at[dest].set(s_g)
    block_e = jnp.clip(jnp.searchsorted(pends, jnp.arange(n_blk) * MOE_BLOCK, side='right'), 0, N_EXPERTS - 1)

    def run_block(args):
        tok_idx, e = args
        xb = tokens[tok_idx]
        return (jax.nn.silu(xb @ w1[e]) * (xb @ w3[e])) @ w2[e]

    y = lax.map(run_block, (slot_tok.reshape(n_blk, MOE_BLOCK), block_e)).reshape(R, D)
    y = y * slot_g[:, None].astype(y.dtype)
    out = jnp.zeros((N, D), y.dtype).at[slot_tok].add(y)
    return out.reshape(B_, T_, D)


def even_layer(x, c, ctx, c_ctx, layer_idx, mod_w, mod_b, mix_pre_g, mix_post_g, w_in, mlstm_gate_b,
               mlstm_conv_w, mlstm_norm_g, lambda_q1, lambda_k1, lambda_q2, lambda_k2, diff_norm_g,
               w_out, ffn_pre_g, ffn_post_g, ffn_w1, ffn_w3, ffn_w2):
    sh_m, sc_m, gt_m, sh_f, sc_f, gt_f = jnp.split(adaln(c, mod_w, mod_b)[:, None, :], 6, axis=-1)
    sh_c, sc_c = jnp.split(adaln(c_ctx, mod_w, mod_b), 6, axis=-1)[:2]
    u = modulate(x, mix_pre_g, sh_m, sc_m)
    p = u @ w_in
    q_m = jax.nn.silu(depthwise_conv(p[..., OFF_MQ:OFF_MK], mlstm_conv_w[:, :MQK]))
    k_m = jax.nn.silu(depthwise_conv(p[..., OFF_MK:OFF_MV], mlstm_conv_w[:, MQK:]))
    v_m = p[..., OFF_MV:OFF_MO]
    o_m = p[..., OFF_MO:OFF_MG]
    ig_f, lf_f, ig_b, lf_b = mlstm_gates(p[..., OFF_MG:OFF_DQ], mlstm_gate_b)
    uc = modulate(ctx, mix_pre_g, sh_c, sc_c)
    kv_mc = uc @ w_in[:, OFF_MK:OFF_MO]
    k_mc = jax.nn.silu(depthwise_conv(kv_mc[..., :MQK], mlstm_conv_w[:, MQK:]))
    v_mc = kv_mc[..., MQK:]
    igc_f, lfc_f, igc_b, lfc_b = mlstm_gates(uc @ w_in[:, OFF_MG:OFF_DQ], mlstm_gate_b)
    kv_dc = uc @ w_in[:, OFF_DK:]
    h_m = mlstm_mixer(q_m, k_m, v_m, o_m, ig_f, lf_f, ig_b, lf_b,
                      k_mc, v_mc, igc_f, lfc_f, igc_b, lfc_b, mlstm_norm_g)
    h_d = diff_mixer(p[..., OFF_DQ:OFF_DK], p[..., OFF_DK:OFF_DV], p[..., OFF_DV:],
                     kv_dc[..., :DQK], kv_dc[..., DQK:],
                     lambda_q1, lambda_k1, lambda_q2, lambda_k2, diff_norm_g, layer_idx)
    mix = jnp.concatenate([h_m, h_d], axis=-1) @ w_out
    x = x + gt_m * rmsnorm(mix, mix_post_g)
    y = swiglu(modulate(x, ffn_pre_g, sh_f, sc_f), ffn_w1, ffn_w3, ffn_w2)
    return x + gt_f * rmsnorm(y, ffn_post_g)


def odd_layer(x, c, mod_w, mod_b, mix_pre_g, mix_post_g, pw1_w, pw1_b, dw_w, dw_b, ln_g, ln_b,
              pw2_w, pw2_b, ffn_pre_g, ffn_post_g, router_w, moe_w1, moe_w3, moe_w2):
    sh_m, sc_m, gt_m, sh_f, sc_f, gt_f = jnp.split(adaln(c, mod_w, mod_b)[:, None, :], 6, axis=-1)
    u = modulate(x, mix_pre_g, sh_m, sc_m)
    a, g = jnp.split(u @ pw1_w + pw1_b, 2, axis=-1)
    h = a * jax.nn.sigmoid(g)
    h = depthwise_conv(h, dw_w) + dw_b
    h = jax.nn.silu(layernorm(h, ln_g, ln_b))
    y = h @ pw2_w + pw2_b
    x = x + gt_m * rmsnorm(y, mix_post_g)
    y = moe_swiglu(modulate(x, ffn_pre_g, sh_f, sc_f), router_w, moe_w1, moe_w3, moe_w2)
    return x + gt_f * rmsnorm(y, ffn_post_g)


def setup_inputs(seed: int = 0) -> dict:
    key = jax.random.key(seed)
    keys = list(jax.random.split(key, 48))
    D = D_MODEL

    def nrm(shape, scale):
        return jax.random.normal(keys.pop(), shape, F32) * scale

    def gain(n):
        return 1.0 + nrm((n,), 0.05)

    inp = {}
    inp['x'] = nrm((BATCH, SEQ, D), 1.0)
    inp['c'] = nrm((BATCH, D), 1.0)
    inp['ctx'] = nrm((BATCH, CTX_LEN, D), 1.0)
    inp['c_ctx'] = nrm((D,), 1.0)
    inp['l0_mod_w'] = nrm((D, 6 * D), 0.5 * D ** -0.5)
    inp['l0_mod_b'] = nrm((6 * D,), 0.02)
    inp['l0_mix_pre_g'] = gain(D)
    inp['l0_mix_post_g'] = gain(D)
    inp['l0_w_in'] = nrm((D, IN0_WIDTH), D ** -0.5)
    gate_offsets = jnp.concatenate([
        jnp.full((MLSTM_HEADS,), -1.0, F32), jnp.linspace(3.0, 6.0, MLSTM_HEADS, dtype=F32),
        jnp.full((MLSTM_HEADS,), -1.0, F32), jnp.linspace(3.0, 6.0, MLSTM_HEADS, dtype=F32)])
    inp['l0_mlstm_gate_b'] = gate_offsets + nrm((MG,), 0.1)
    inp['l0_mlstm_conv_w'] = nrm((MLSTM_CONV, 2 * MQK), MLSTM_CONV ** -0.5)
    inp['l0_mlstm_norm_g'] = gain(MV)
    inp['l0_lambda_q1'] = nrm((DIFF_DQK,), 0.1)
    inp['l0_lambda_k1'] = nrm((DIFF_DQK,), 0.1)
    inp['l0_lambda_q2'] = nrm((DIFF_DQK,), 0.1)
    inp['l0_lambda_k2'] = nrm((DIFF_DQK,), 0.1)
    inp['l0_diff_norm_g'] = gain(DVW)
    inp['l0_w_out'] = nrm((MIX0_WIDTH, D), MIX0_WIDTH ** -0.5)
    inp['l0_ffn_pre_g'] = gain(D)
    inp['l0_ffn_post_g'] = gain(D)
    inp['l0_ffn_w1'] = nrm((D, D_FF), D ** -0.5)
    inp['l0_ffn_w3'] = nrm((D, D_FF), D ** -0.5)
    inp['l0_ffn_w2'] = nrm((D_FF, D), D_FF ** -0.5)
    inp['l1_mod_w'] = nrm((D, 6 * D), 0.5 * D ** -0.5)
    inp['l1_mod_b'] = nrm((6 * D,), 0.02)
    inp['l1_mix_pre_g'] = gain(D)
    inp['l1_mix_post_g'] = gain(D)
    inp['l1_conv_pw1_w'] = nrm((D, 2 * D), D ** -0.5)
    inp['l1_conv_pw1_b'] = nrm((2 * D,), 0.02)
    inp['l1_conv_dw_w'] = nrm((CONV_WIDTH, D), CONV_WIDTH ** -0.5)
    inp['l1_conv_dw_b'] = nrm((D,), 0.02)
    inp['l1_conv_ln_g'] = gain(D)
    inp['l1_conv_ln_b'] = nrm((D,), 0.02)
    inp['l1_conv_pw2_w'] = nrm((D, D), D ** -0.5)
    inp['l1_conv_pw2_b'] = nrm((D,), 0.02)
    inp['l1_ffn_pre_g'] = gain(D)
    inp['l1_ffn_post_g'] = gain(D)
    inp['l1_router_w'] = nrm((D, N_EXPERTS), D ** -0.5)
    inp['l1_moe_w1'] = nrm((N_EXPERTS, D, D_FF_EXPERT), D ** -0.5)
    inp['l1_moe_w3'] = nrm((N_EXPERTS, D, D_FF_EXPERT), D ** -0.5)
    inp['l1_moe_w2'] = nrm((N_EXPERTS, D_FF_EXPERT, D), D_FF_EXPERT ** -0.5)
    return inp


def reference(x, c, ctx, c_ctx,
              l0_mod_w, l0_mod_b, l0_mix_pre_g, l0_mix_post_g, l0_w_in, l0_mlstm_gate_b,
              l0_mlstm_conv_w, l0_mlstm_norm_g, l0_lambda_q1, l0_lambda_k1, l0_lambda_q2,
              l0_lambda_k2, l0_diff_norm_g, l0_w_out, l0_ffn_pre_g, l0_ffn_post_g,
              l0_ffn_w1, l0_ffn_w3, l0_ffn_w2,
              l1_mod_w, l1_mod_b, l1_mix_pre_g, l1_mix_post_g, l1_conv_pw1_w, l1_conv_pw1_b,
              l1_conv_dw_w, l1_conv_dw_b, l1_conv_ln_g, l1_conv_ln_b, l1_conv_pw2_w,
              l1_conv_pw2_b, l1_ffn_pre_g, l1_ffn_post_g, l1_router_w, l1_moe_w1, l1_moe_w3,
              l1_moe_w2):
    layer_params = (
        (l0_mod_w, l0_mod_b, l0_mix_pre_g, l0_mix_post_g, l0_w_in, l0_mlstm_gate_b,
         l0_mlstm_conv_w, l0_mlstm_norm_g, l0_lambda_q1, l0_lambda_k1, l0_lambda_q2,
         l0_lambda_k2, l0_diff_norm_g, l0_w_out, l0_ffn_pre_g, l0_ffn_post_g,
         l0_ffn_w1, l0_ffn_w3, l0_ffn_w2),
        (l1_mod_w, l1_mod_b, l1_mix_pre_g, l1_mix_post_g, l1_conv_pw1_w, l1_conv_pw1_b,
         l1_conv_dw_w, l1_conv_dw_b, l1_conv_ln_g, l1_conv_ln_b, l1_conv_pw2_w,
         l1_conv_pw2_b, l1_ffn_pre_g, l1_ffn_post_g, l1_router_w, l1_moe_w1, l1_moe_w3,
         l1_moe_w2),
    )
    for layer in range(DEPTH):
        if layer % 2 == 0:
            x = even_layer(x, c, ctx, c_ctx, layer, *layer_params[layer])
        else:
            x = odd_layer(x, c, *layer_params[layer])
    return x
```

```python
import functools
import math

import jax
import jax.numpy as jnp
from jax import lax
from jax.experimental import pallas as pl
from jax.experimental.pallas import tpu as pltpu

F32 = jnp.float32
BF16 = jnp.bfloat16
EPS = 1e-6
NEG = -1e30

D_MODEL = 1024
GRID_W = 64
MLSTM_HEADS = 8
MLSTM_D = 64
MLSTM_CHUNK = 256
DIFF_HEADS = 4
DIFF_DQK = 64
ROPE_BASE = 10000.0
CONV_WIDTH = 31
N_EXPERTS = 8
MOE_ROWS = 512
LANES = 128
VMEM_LIMIT = 52 * 1024 * 1024


def _cp(*sem):
    return pltpu.CompilerParams(dimension_semantics=sem, vmem_limit_bytes=VMEM_LIMIT)


def _rms(x, g):
    return x * lax.rsqrt(jnp.mean(x * x, axis=-1, keepdims=True) + EPS) * g


def _silu(x):
    return x * jax.nn.sigmoid(x)


def _dot(a, b):
    return jnp.dot(a, b, preferred_element_type=F32)


def _dot_nt(a, b):
    return lax.dot_general(a, b, (((1,), (1,)), ((), ())), preferred_element_type=F32)


def _dot_tn(a, b):
    return lax.dot_general(a, b, (((0,), (0,)), ((), ())), preferred_element_type=F32)


def _adaln_kernel(c_ref, w_ref, b_ref, o_ref):
    s = _silu(c_ref[...])
    o_ref[...] = _dot(s.astype(BF16), w_ref[...].astype(BF16)) + b_ref[...]


def _adaln(cpad, w, b):
    rows, d = cpad.shape
    n = w.shape[1]
    tn = 1536
    return pl.pallas_call(
        _adaln_kernel,
        grid=(n // tn,),
        in_specs=[pl.BlockSpec((rows, d), lambda j: (0, 0)),
                  pl.BlockSpec((d, tn), lambda j: (0, j)),
                  pl.BlockSpec((1, tn), lambda j: (0, j))],
        out_specs=pl.BlockSpec((rows, tn), lambda j: (0, j)),
        out_shape=jax.ShapeDtypeStruct((rows, n), F32),
        compiler_params=_cp("arbitrary"),
        name="adaln",
    )(cpad, w, b.reshape(1, n))


def _inproj_kernel(x_ref, xp_ref, xn_ref, mod_ref, g_ref, w_ref, cw_ref, gb_ref, cos_ref, sin_ref,
                   mq_ref, mk_ref, mv_ref, mo_ref, dq_ref, dk_ref, dv_ref, gt_ref, pext_ref,
                   *, tm, nt, rope):
    i = pl.program_id(1)
    g = g_ref[...]
    sh = mod_ref[0:1, :]
    sc = mod_ref[1:2, :]

    def mod(xv):
        return _rms(xv, g) * (1.0 + sc) + sh

    u = mod(x_ref[...])
    up = jnp.where(i > 0, mod(xp_ref[...]), 0.0)
    un = jnp.where(i < nt - 1, mod(xn_ref[...]), 0.0)
    ub = u.astype(BF16)
    uext = jnp.concatenate([up.astype(BF16), ub, un.astype(BF16)], axis=0)

    pext_ref[...] = _dot(uext, w_ref[:, 0:1024])
    cw = cw_ref[...]
    conv = (cw[0:1, :] * pext_ref[pl.ds(7, tm), :] + cw[1:2, :] * pext_ref[pl.ds(8, tm), :]
            + cw[2:3, :] * pext_ref[pl.ds(9, tm), :])
    act = _silu(conv)
    mq_ref[...] = (act[:, 0:512] * (MLSTM_D ** -0.5)).astype(BF16)
    mk_ref[...] = act[:, 512:1024].astype(BF16)

    p = _dot(ub, w_ref[:, 1024:2048])
    mv_ref[...] = p[:, 0:512].astype(BF16)
    mo_ref[...] = p[:, 512:1024].astype(BF16)

    p = _dot(ub, w_ref[:, 2048:3072])
    if rope:
        lane = lax.broadcasted_iota(jnp.int32, p.shape, 1)
        first_half = ((lane // 16) % 2) == 0
        nl = p.shape[1]
        partner = jnp.where(first_half, pltpu.roll(p, nl - 16, 1), pltpu.roll(p, 16, 1))
        cos = jnp.concatenate([cos_ref[...]] * 8, axis=1)
        sin = jnp.concatenate([sin_ref[...]] * 8, axis=1)
        p = p * cos + partner * sin
    dq_ref[...] = (p[:, 0:512] * (DIFF_DQK ** -0.5)).astype(BF16)
    dk_ref[...] = p[:, 512:1024].astype(BF16)

    p = _dot(ub, w_ref[:, 3072:3712])
    dv_ref[...] = p[:, 0:512].astype(BF16)
    gt_ref[...] = p[:, 512:640] + gb_ref[...]


def _inproj(x, mod, mod_row, g, w_perm, conv_w, gate_b, cos, sin, *, tm, rope):
    b, t, d = x.shape
    nt = t // tm
    hb = tm // 8
    nb8 = t // 8
    bf = lambda: jax.ShapeDtypeStruct((b, t, 512), BF16)
    if mod_row is None:
        mod_map = lambda bi, i: (bi, 0, 0)
    else:
        mod_map = lambda bi, i: (mod_row, 0, 0)
    kern = functools.partial(_inproj_kernel, tm=tm, nt=nt, rope=rope)
    o512 = pl.BlockSpec((None, tm, 512), lambda bi, i: (bi, i, 0))
    return pl.pallas_call(
        kern,
        grid=(b, nt),
        in_specs=[
            pl.BlockSpec((None, tm, d), lambda bi, i: (bi, i, 0)),
            pl.BlockSpec((None, 8, d), lambda bi, i: (bi, jnp.maximum(i * hb - 1, 0), 0)),
            pl.BlockSpec((None, 8, d), lambda bi, i: (bi, jnp.minimum((i + 1) * hb, nb8 - 1), 0)),
            pl.BlockSpec((None, 6, d), mod_map),
            pl.BlockSpec((1, d), lambda bi, i: (0, 0)),
            pl.BlockSpec((d, 3712), lambda bi, i: (0, 0)),
            pl.BlockSpec((3, d), lambda bi, i: (0, 0)),
            pl.BlockSpec((1, LANES), lambda bi, i: (0, 0)),
            pl.BlockSpec((tm, LANES), lambda bi, i: (i, 0)),
            pl.BlockSpec((tm, LANES), lambda bi, i: (i, 0)),
        ],
        out_specs=[o512] * 7 + [pl.BlockSpec((None, tm, LANES), lambda bi, i: (bi, i, 0))],
        out_shape=[bf() for _ in range(7)] + [jax.ShapeDtypeStruct((b, t, LANES), F32)],
        scratch_shapes=[pltpu.VMEM((tm + 16, 1024), F32)],
        compiler_params=_cp("arbitrary", "arbitrary"),
        name="inproj_rope" if rope else "inproj_ctx",
    )(x, x, x, mod, g, w_perm, conv_w, gate_b, cos, sin)


def _gateprep_kernel(g_ref, arow_ref, col_ref, *, L, nch):
    tt = L * nch
    gt = g_ref[...].T
    i_f, f_f, i_b, f_b = gt[0:8], gt[8:16], gt[16:24], gt[24:32]

    def logsig(v):
        return jnp.minimum(v, 0.0) - jnp.log1p(jnp.exp(-jnp.abs(v)))

    pos = lax.broadcasted_iota(jnp.int32, (8, tt), 1) % L

    def scan(v, op, ident, reverse):
        s = 1
        while s < L:
            if reverse:
                shifted = pltpu.roll(v, tt - s, 1)
                valid = pos < L - s
            else:
                shifted = pltpu.roll(v, s, 1)
                valid = pos >= s
            v = op(v, jnp.where(valid, shifted, ident))
            s *= 2
        return v

    outs = []
    for d, (ig, fg) in enumerate(((i_f, f_f), (i_b, f_b))):
        rev = d == 1
        bcum = scan(logsig(fg), jnp.add, 0.0, rev)
        a = ig - bcum
        cm = scan(a, jnp.maximum, NEG, rev)
        order = list(range(nch)) if not rev else [0] + list(range(nch - 1, 0, -1))
        mp = jnp.zeros((8, 1), F32)
        mp_c = [None] * nch
        ml_c = [None] * nch
        for j in order:
            e = j * L if rev else j * L + L - 1
            mlast = jnp.maximum(mp, cm[:, e:e + 1])
            mp_c[j] = jnp.broadcast_to(mp, (8, L))
            ml_c[j] = jnp.broadcast_to(mlast, (8, L))
            mp = bcum[:, e:e + 1] + mlast
        mprev = jnp.concatenate(mp_c, axis=1)
        mlast = jnp.concatenate(ml_c, axis=1)
        m = jnp.maximum(mprev, cm)
        outs.append((a, m, jnp.exp(mprev - m), jnp.exp(a - mlast), jnp.exp(-(bcum + m))))

    arow_ref[...] = jnp.concatenate([outs[0][0], outs[1][0]], axis=0)
    rows = []
    for pair in range(4):
        for q in range(1, 5):
            for d in range(2):
                rows.append(outs[d][q][2 * pair:2 * pair + 2])
    rows.append(jnp.zeros((64, tt), F32))
    col_ref[...] = jnp.concatenate(rows, axis=0).T


def _gateprep(gall, *, L):
    b, tt, _ = gall.shape
    nch = tt // L
    return pl.pallas_call(
        functools.partial(_gateprep_kernel, L=L, nch=nch),
        grid=(b,),
        in_specs=[pl.BlockSpec((None, tt, LANES), lambda bi: (bi, 0, 0))],
        out_specs=[pl.BlockSpec((None, 16, tt), lambda bi: (bi, 0, 0)),
                   pl.BlockSpec((None, tt, LANES), lambda bi: (bi, 0, 0))],
        out_shape=[jax.ShapeDtypeStruct((b, 16, tt), F32), jax.ShapeDtypeStruct((b, tt, LANES), F32)],
        compiler_params=_cp("arbitrary"),
        name="mlstm_gateprep",
    )(gall)


def _mlstm_kernel(q_ref, k_ref, v_ref, o_ref, kc_ref, vc_ref, arow_ref, col_ref, ng_ref, out_ref,
                  c_ref, hf_ref, hb_ref, *, L, nc):
    p = pl.program_id(1)
    half = nc // 2
    lane = lax.broadcasted_iota(jnp.int32, (L, LANES), 1)
    lo = lane < MLSTM_D
    head_mask = (lo, jnp.logical_not(lo))
    ri = lax.broadcasted_iota(jnp.int32, (L, L), 0)
    ci = lax.broadcasted_iota(jnp.int32, (L, L), 1)
    causal = (ci <= ri, ci >= ri)
    ones_t = jnp.ones((L, LANES), BF16)
    shift = lax.rem(LANES - 16 * p, LANES)

    def cols(off):
        return pltpu.roll(col_ref[pl.ds(off, L), :], shift, 1)

    def col(blk, q, d, hh):
        j = q * 4 + d * 2 + hh
        return blk[:, j:j + 1]

    def vext_of(vb, hh):
        return jnp.concatenate([jnp.where(head_mask[hh], vb, jnp.zeros_like(vb)), ones_t], axis=1)

    def state_update(d, hh, kb, vext, blk, dec):
        khm = jnp.where(head_mask[hh], kb, jnp.zeros_like(kb))
        wv = (col(blk, 2, d, hh) * vext.astype(F32)).astype(BF16)
        upd = _dot_tn(khm, wv)
        if dec is None:
            c_ref[d, hh] = upd
        else:
            c_ref[d, hh] = dec * c_ref[d, hh] + upd

    blk0 = cols(0)
    kcb = kc_ref[...]
    vcb = vc_ref[...]
    for d in range(2):
        for hh in range(2):
            state_update(d, hh, kcb, vext_of(vcb, hh), blk0, None)

    def compute(d, c):
        t0 = pl.multiple_of(c * L, L)
        off = pl.multiple_of(c * L + L, L)
        qb = q_ref[pl.ds(t0, L), :]
        kb = k_ref[pl.ds(t0, L), :]
        vb = v_ref[pl.ds(t0, L), :]
        blk = cols(off)
        last = L - 1 if d == 0 else 0
        hs = []
        for hh in range(2):
            arow = arow_ref[pl.ds(d * 8 + 2 * p + hh, 1), pl.ds(off, L)]
            khm = jnp.where(head_mask[hh], kb, jnp.zeros_like(kb))
            s = _dot_nt(qb, khm)
            arg = jnp.where(causal[d], arow - col(blk, 0, d, hh), NEG)
            pm = (s * jnp.exp(arg)).astype(BF16)
            vext = vext_of(vb, hh)
            ch = c_ref[d, hh]
            tot = _dot(pm, vext) + col(blk, 1, d, hh) * _dot(qb, ch.astype(BF16))
            den = jnp.maximum(jnp.abs(tot[:, LANES:]), col(blk, 3, d, hh))
            hs.append(tot[:, :LANES] / den)
            dec = col(blk, 1, d, hh)[last:last + 1, :]
            state_update(d, hh, kb, vext, blk, dec)
        return jnp.where(lo, hs[0], hs[1])

    def finalize(c, hsum):
        t0 = pl.multiple_of(c * L, L)
        sq = hsum * hsum
        s0 = jnp.sum(jnp.where(lo, sq, 0.0), axis=-1, keepdims=True)
        s1 = jnp.sum(jnp.where(lo, 0.0, sq), axis=-1, keepdims=True)
        ms = jnp.where(lo, s0, s1) * (1.0 / MLSTM_D)
        y = hsum * lax.rsqrt(ms + EPS) * ng_ref[...]
        gate = jax.nn.sigmoid(o_ref[pl.ds(t0, L), :].astype(F32))
        out_ref[pl.ds(t0, L), :] = (y * gate).astype(BF16)

    def phase_a(i, carry):
        hf_ref[pl.ds(pl.multiple_of(i * L, L), L), :] = compute(0, i)
        cb = nc - 1 - i
        hb_ref[pl.ds(pl.multiple_of((cb - half) * L, L), L), :] = compute(1, cb)
        return carry

    def phase_b(i, carry):
        hf = compute(0, i)
        finalize(i, hf + hb_ref[pl.ds(pl.multiple_of((i - half) * L, L), L), :])
        cb = nc - 1 - i
        hb = compute(1, cb)
        finalize(cb, hb + hf_ref[pl.ds(pl.multiple_of(cb * L, L), L), :])
        return carry

    lax.fori_loop(0, half, phase_a, 0)
    lax.fori_loop(half, nc, phase_b, 0)


def _mlstm(mq, mk, mv, mo, mkc, mvc, arow, cols, norm_g, *, L):
    b, t, _ = mq.shape
    ctx = mkc.shape[1]
    tt = arow.shape[2]
    nc = t // L
    tok = pl.BlockSpec((None, t, LANES), lambda bi, p: (bi, 0, p))
    ctxs = pl.BlockSpec((None, ctx, LANES), lambda bi, p: (bi, 0, p))
    return pl.pallas_call(
        functools.partial(_mlstm_kernel, L=L, nc=nc),
        grid=(b, 4),
        in_specs=[tok, tok, tok, tok, ctxs, ctxs,
                  pl.BlockSpec((None, 16, tt), lambda bi, p: (bi, 0, 0)),
                  pl.BlockSpec((None, tt, LANES), lambda bi, p: (bi, 0, 0)),
                  pl.BlockSpec((1, LANES), lambda bi, p: (0, p))],
        out_specs=tok,
        out_shape=jax.ShapeDtypeStruct((b, t, 512), BF16),
        scratch_shapes=[pltpu.VMEM((2, 2, LANES, 2 * LANES), F32),
                        pltpu.VMEM((t // 2, LANES), F32),
                        pltpu.VMEM((t // 2, LANES), F32)],
        compiler_params=_cp("arbitrary", "arbitrary"),
        name="mlstm_scan",
    )(mq, mk, mv, mo, mkc, mvc, arow, cols, norm_g.reshape(1, 512))


def _attn_kernel(q_ref, k_ref, v_ref, kc_ref, vc_ref, lam_ref, ng_ref, out_ref, m_ref, acc_ref,
                 *, tq, kb, nkb, lam_init):
    lane = lax.broadcasted_iota(jnp.int32, (tq, LANES), 1)
    lo = lane < DIFF_DQK
    q = q_ref[...]
    zq = jnp.zeros_like(q)
    qs = jnp.concatenate([jnp.where(lo, q, zq), jnp.where(lo, zq, q)], axis=0)
    m_ref[...] = jnp.full(m_ref.shape, NEG, F32)
    acc_ref[...] = jnp.zeros(acc_ref.shape, F32)

    def step(kblk, vblk):
        s = _dot_nt(qs, kblk)
        m_old = m_ref[...]
        m_new = jnp.maximum(m_old, jnp.max(s, axis=-1, keepdims=True))
        pm = jnp.exp(s - m_new).astype(BF16)
        vext = jnp.concatenate([vblk, jnp.ones_like(vblk)], axis=1)
        acc_ref[...] = jnp.exp(m_old - m_new) * acc_ref[...] + _dot(pm, vext)
        m_ref[...] = m_new

    step(kc_ref[...], vc_ref[...])

    def body(j, carry):
        k0 = pl.multiple_of(j * kb, kb)
        step(k_ref[pl.ds(k0, kb), :], v_ref[pl.ds(k0, kb), :])
        return carry

    lax.fori_loop(0, nkb, body, 0)

    lq = lam_ref[...]
    lam = (jnp.exp(jnp.sum(lq[0:1, :] * lq[1:2, :], axis=-1, keepdims=True))
           - jnp.exp(jnp.sum(lq[2:3, :] * lq[3:4, :], axis=-1, keepdims=True)) + lam_init)
    a1 = acc_ref[0:tq, :]
    a2 = acc_ref[tq:2 * tq, :]
    o = a1[:, :LANES] / a1[:, LANES:] - lam * (a2[:, :LANES] / a2[:, LANES:])
    out_ref[...] = (_rms(o, ng_ref[...]) * (1.0 - lam_init)).astype(BF16)


def _attn(dq, dk, dv, dkc, dvc, lam_in, norm_g, *, tq, kb, lam_init):
    b, t, _ = dq.shape
    ctx = dkc.shape[1]
    full = pl.BlockSpec((None, t, LANES), lambda bi, h, i: (bi, 0, h))
    ctxs = pl.BlockSpec((None, ctx, LANES), lambda bi, h, i: (bi, 0, h))
    qs = pl.BlockSpec((None, tq, LANES), lambda bi, h, i: (bi, i, h))
    return pl.pallas_call(
        functools.partial(_attn_kernel, tq=tq, kb=kb, nkb=t // kb, lam_init=lam_init),
        grid=(b, DIFF_HEADS, t // tq),
        in_specs=[qs, full, full, ctxs, ctxs,
                  pl.BlockSpec((8, LANES), lambda bi, h, i: (0, 0)),
                  pl.BlockSpec((1, LANES), lambda bi, h, i: (0, h))],
        out_specs=qs,
        out_shape=jax.ShapeDtypeStruct((b, t, 512), BF16),
        scratch_shapes=[pltpu.VMEM((2 * tq, 1), F32), pltpu.VMEM((2 * tq, 2 * LANES), F32)],
        compiler_params=_cp("arbitrary", "arbitrary", "arbitrary"),
        name="diff_attn",
    )(dq, dk, dv, dkc, dvc, lam_in, norm_g.reshape(1, 512))


def _outproj_kernel(hm_ref, hd_ref, wt_ref, wb_ref, x_ref, mod_ref, pg_ref, fg_ref, x1_ref, u_ref):
    mix = _dot(hm_ref[...], wt_ref[...]) + _dot(hd_ref[...], wb_ref[...])
    x1 = x_ref[...] + mod_ref[2:3, :] * _rms(mix, pg_ref[...])
    x1_ref[...] = x1
    u_ref[...] = (_rms(x1, fg_ref[...]) * (1.0 + mod_ref[4:5, :]) + mod_ref[3:4, :]).astype(BF16)


def _outproj(hm, hd, wt, wb, x, mod, post_g, ffn_pre_g, *, tm):
    b, t, d = x.shape
    row = lambda w: pl.BlockSpec((None, tm, w), lambda bi, i: (bi, i, 0))
    cst = lambda s: pl.BlockSpec(s, lambda bi, i: (0,) * len(s))
    return pl.pallas_call(
        _outproj_kernel,
        grid=(b, t // tm),
        in_specs=[row(512), row(512), cst((512, d)), cst((512, d)), row(d),
                  pl.BlockSpec((None, 6, d), lambda bi, i: (bi, 0, 0)), cst((1, d)), cst((1, d))],
        out_specs=[row(d), row(d)],
        out_shape=[jax.ShapeDtypeStruct((b, t, d), F32), jax.ShapeDtypeStruct((b, t, d), BF16)],
        compiler_params=_cp("arbitrary", "arbitrary"),
        name="outproj",
    )(hm, hd, wt, wb, x, mod, post_g.reshape(1, d), ffn_pre_g.reshape(1, d))


def _ffn_kernel(u_ref, w1_ref, w3_ref, w2_ref, x_ref, mod0_ref, mod1_ref, pg_ref, ng_ref,
                x2_ref, u3_ref, acc_ref, *, nf):
    j = pl.program_id(2)

    @pl.when(j == 0)
    def _():
        acc_ref[...] = jnp.zeros(acc_ref.shape, F32)

    u = u_ref[...]
    h = _silu(_dot(u, w1_ref[...])) * _dot(u, w3_ref[...])
    acc_ref[...] += _dot(h.astype(BF16), w2_ref[...])

    @pl.when(j == nf - 1)
    def _():
        x2 = x_ref[...] + mod0_ref[5:6, :] * _rms(acc_ref[...], pg_ref[...])
        x2_ref[...] = x2
        u3_ref[...] = (_rms(x2, ng_ref[...]) * (1.0 + mod1_ref[1:2, :]) + mod1_ref[0:1, :]).astype(BF16)


def _ffn(u, w1, w3, w2, x1, mod0, mod1, post_g, next_pre_g, *, tm, tf):
    b, t, d = x1.shape
    f = w1.shape[1]
    nf = f // tf
    row = pl.BlockSpec((None, tm, d), lambda bi, i, j: (bi, i, 0))
    modb = pl.BlockSpec((None, 6, d), lambda bi, i, j: (bi, 0, 0))
    vec = pl.BlockSpec((1, d), lambda bi, i, j: (0, 0))
    return pl.pallas_call(
        functools.partial(_ffn_kernel, nf=nf),
        grid=(b, t // tm, nf),
        in_specs=[row, pl.BlockSpec((d, tf), lambda bi, i, j: (0, j)),
                  pl.BlockSpec((d, tf), lambda bi, i, j: (0, j)),
                  pl.BlockSpec((tf, d), lambda bi, i, j: (j, 0)), row, modb, modb, vec, vec],
        out_specs=[row, row],
        out_shape=[jax.ShapeDtypeStruct((b, t, d), F32), jax.ShapeDtypeStruct((b, t, d), BF16)],
        scratch_shapes=[pltpu.VMEM((tm, d), F32)],
        compiler_params=_cp("arbitrary", "arbitrary", "arbitrary"),
        name="ffn_swiglu",
    )(u, w1, w3, w2, x1, mod0, mod1, post_g.reshape(1, d), next_pre_g.reshape(1, d))


HALO = 16


def _conv_kernel(u_ref, up_ref, un_ref, w1_ref, b1_ref, dw_ref, dwb_ref, lng_ref, lnb_ref, w2_ref, b2_ref,
                 x_ref, mod_ref, pg_ref, fg_ref, rw_ref, x3_ref, u4_ref, route_ref, cnt_ref,
                 hs_ref, run_ref, *, tm, nt):
    bi = pl.program_id(0)
    i = pl.program_id(1)

    @pl.when((bi == 0) & (i == 0))
    def _():
        run_ref[...] = jnp.zeros(run_ref.shape, F32)

    uext = jnp.concatenate([up_ref[...], u_ref[...], un_ref[...]], axis=0)
    ag = _dot(uext, w1_ref[...]) + b1_ref[...]
    h = ag[:, :D_MODEL] * jax.nn.sigmoid(ag[:, D_MODEL:])
    row = lax.broadcasted_iota(jnp.int32, (tm + 2 * HALO, 1), 0)
    keep = ((row >= HALO) | (i > 0)) & ((row < tm + HALO) | (i < nt - 1))
    hs_ref[...] = jnp.where(keep, h, 0.0)

    acc = jnp.zeros((tm, D_MODEL), F32) + dwb_ref[...]
    for j in range(CONV_WIDTH):
        acc = acc + dw_ref[j:j + 1, :] * hs_ref[pl.ds(j + 1, tm), :]
    mu = jnp.mean(acc, axis=-1, keepdims=True)
    cen = acc - mu
    var = jnp.mean(cen * cen, axis=-1, keepdims=True)
    hn = _silu(cen * lax.rsqrt(var + EPS) * lng_ref[...] + lnb_ref[...])
    y = _dot(hn.astype(BF16), w2_ref[...]) + b2_ref[...]
    x3 = x_ref[...] + mod_ref[2:3, :] * _rms(y, pg_ref[...])
    x3_ref[...] = x3
    u4 = _rms(x3, fg_ref[...]) * (1.0 + mod_ref[4:5, :]) + mod_ref[3:4, :]
    u4_ref[...] = u4
    u4b = u4.astype(BF16)

    lane = lax.broadcasted_iota(jnp.int32, (tm, LANES), 1).astype(F32)
    logits = jnp.where(lane < N_EXPERTS, _dot(u4b, rw_ref[...]), NEG)
    m1 = jnp.max(logits, axis=-1, keepdims=True)
    i1 = jnp.min(jnp.where(logits == m1, lane, float(LANES)), axis=-1, keepdims=True)
    l2 = jnp.where(lane == i1, NEG, logits)
    m2 = jnp.max(l2, axis=-1, keepdims=True)
    i2 = jnp.min(jnp.where(l2 == m2, lane, float(LANES)), axis=-1, keepdims=True)
    e21 = jnp.exp(m2 - m1)
    g1 = 1.0 / (1.0 + e21)
    g2 = e21 * g1
    sel = ((lane == i1) | (lane == i2)).astype(F32)
    ri = lax.broadcasted_iota(jnp.int32, (tm, tm), 0)
    ci = lax.broadcasted_iota(jnp.int32, (tm, tm), 1)
    tri = (ci <= ri).astype(BF16)
    csum = _dot(tri, sel.astype(BF16))
    rank = csum - sel + run_ref[0:1, :]
    r1 = jnp.sum(jnp.where(lane == i1, rank, 0.0), axis=-1, keepdims=True)
    r2 = jnp.sum(jnp.where(lane == i2, rank, 0.0), axis=-1, keepdims=True)
    run_ref[...] = run_ref[...] + csum[tm - 1:tm, :]
    cnt_ref[...] = run_ref[...]
    route = jnp.zeros((tm, LANES), F32)
    for n, v in enumerate((i1, i2, g1, g2, r1, r2)):
        route = jnp.where(lane == float(n), v, route)
    route_ref[...] = route


def _convmod(u3, w1, b1, dw, dwb, lng, lnb, w2, b2, x2, mod1, post_g, ffn_pre_g, rw, *, tm):
    b, t, d = x2.shape
    nt = t // tm
    hb = tm // HALO
    nbh = t // HALO
    row = lambda w: pl.BlockSpec((None, tm, w), lambda bi, i: (bi, i, 0))
    cst = lambda s: pl.BlockSpec(s, lambda bi, i: (0,) * len(s))
    return pl.pallas_call(
        functools.partial(_conv_kernel, tm=tm, nt=nt),
        grid=(b, nt),
        in_specs=[row(d),
                  pl.BlockSpec((None, HALO, d), lambda bi, i: (bi, jnp.maximum(i * hb - 1, 0), 0)),
                  pl.BlockSpec((None, HALO, d), lambda bi, i: (bi, jnp.minimum((i + 1) * hb, nbh - 1), 0)),
                  cst((d, 2 * d)), cst((1, 2 * d)), cst((32, d)), cst((1, d)), cst((1, d)), cst((1, d)),
                  cst((d, d)), cst((1, d)), row(d),
                  pl.BlockSpec((None, 6, d), lambda bi, i: (bi, 0, 0)), cst((1, d)), cst((1, d)),
                  cst((d, LANES))],
        out_specs=[row(d), row(d), row(LANES), cst((8, LANES))],
        out_shape=[jax.ShapeDtypeStruct((b, t, d), F32), jax.ShapeDtypeStruct((b, t, d), F32),
                   jax.ShapeDtypeStruct((b, t, LANES), F32), jax.ShapeDtypeStruct((8, LANES), F32)],
        scratch_shapes=[pltpu.VMEM((tm + 2 * HALO, d), F32), pltpu.VMEM((8, LANES), F32)],
        compiler_params=_cp("arbitrary", "arbitrary"),
        name="conv_module_router",
    )(u3, u3, u3, w1, b1, dw, dwb, lng, lnb, w2, b2, x2, mod1, post_g, ffn_pre_g, rw)


def _gather_kernel(idx_ref, src_ref, out_ref, sem, *, rb):
    def copy(r):
        return pltpu.make_async_copy(src_ref.at[pl.ds(idx_ref[0, r], 1), :], out_ref.at[pl.ds(r, 1), :], sem)

    def start(r, carry):
        copy(r).start()
        return carry

    def wait(r, carry):
        copy(r).wait()
        return carry

    lax.fori_loop(0, rb, start, 0)
    lax.fori_loop(0, rb, wait, 0)


def _gather_rows(src, idx, *, rb):
    n_rows = idx.shape[0]
    d = src.shape[1]
    nb = n_rows // rb
    return pl.pallas_call(
        functools.partial(_gather_kernel, rb=rb),
        grid=(nb,),
        in_specs=[pl.BlockSpec((None, 1, rb), lambda i: (i, 0, 0), memory_space=pltpu.SMEM),
                  pl.BlockSpec(memory_space=pl.ANY)],
        out_specs=pl.BlockSpec((rb, d), lambda i: (i, 0)),
        out_shape=jax.ShapeDtypeStruct((n_rows, d), src.dtype),
        scratch_shapes=[pltpu.SemaphoreType.DMA],
        compiler_params=_cp("arbitrary"),
        name="moe_gather",
    )(idx.reshape(nb, 1, rb), src)


def _moe_kernel(be_ref, x_ref, w1_ref, w3_ref, w2_ref, y_ref, acc_ref, *, nf):
    j = pl.program_id(1)

    @pl.when(j == 0)
    def _():
        acc_ref[...] = jnp.zeros(acc_ref.shape, F32)

    xb = x_ref[...].astype(BF16)
    h = _silu(_dot(xb, w1_ref[...])) * _dot(xb, w3_ref[...])
    acc_ref[...] += _dot(h.astype(BF16), w2_ref[...])

    @pl.when(j == nf - 1)
    def _():
        y_ref[...] = acc_ref[...]


def _moe_ffn(xs, block_e, w1, w3, w2, *, rows, tf):
    n_rows, d = xs.shape
    f = w1.shape[2]
    nf = f // tf
    grid_spec = pltpu.PrefetchScalarGridSpec(
        num_scalar_prefetch=1,
        grid=(n_rows // rows, nf),
        in_specs=[pl.BlockSpec((rows, d), lambda i, j, be: (i, 0)),
                  pl.BlockSpec((None, d, tf), lambda i, j, be: (be[i], 0, j)),
                  pl.BlockSpec((None, d, tf), lambda i, j, be: (be[i], 0, j)),
                  pl.BlockSpec((None, tf, d), lambda i, j, be: (be[i], j, 0))],
        out_specs=pl.BlockSpec((rows, d), lambda i, j, be: (i, 0)),
        scratch_shapes=[pltpu.VMEM((rows, d), F32)],
    )
    return pl.pallas_call(
        functools.partial(_moe_kernel, nf=nf),
        grid_spec=grid_spec,
        out_shape=jax.ShapeDtypeStruct((n_rows, d), F32),
        compiler_params=_cp("arbitrary", "arbitrary"),
        name="moe_ffn",
    )(block_e, xs, w1, w3, w2)


def _combine_kernel(s1_ref, s2_ref, ys_ref, route_ref, x_ref, mod_ref, pg_ref, out_ref, y1_ref, y2_ref, sem,
                    *, tm):
    def copies(r):
        return (pltpu.make_async_copy(ys_ref.at[pl.ds(s1_ref[0, r], 1), :], y1_ref.at[pl.ds(r, 1), :], sem.at[0]),
                pltpu.make_async_copy(ys_ref.at[pl.ds(s2_ref[0, r], 1), :], y2_ref.at[pl.ds(r, 1), :], sem.at[1]))

    def start(r, carry):
        a, b = copies(r)
        a.start()
        b.start()
        return carry

    def wait(r, carry):
        a, b = copies(r)
        a.wait()
        b.wait()
        return carry

    lax.fori_loop(0, tm, start, 0)
    lax.fori_loop(0, tm, wait, 0)
    route = route_ref[...]
    y = route[:, 2:3] * y1_ref[...] + route[:, 3:4] * y2_ref[...]
    out_ref[...] = x_ref[...] + mod_ref[5:6, :] * _rms(y, pg_ref[...])


def _combine(slot1, slot2, ys, route, x3, mod1, post_g, *, tm):
    b, t, d = x3.shape
    nt = t // tm
    idx = lambda: pl.BlockSpec((None, 1, tm), lambda bi, i: (bi * nt + i, 0, 0), memory_space=pltpu.SMEM)
    row = lambda w: pl.BlockSpec((None, tm, w), lambda bi, i: (bi, i, 0))
    return pl.pallas_call(
        functools.partial(_combine_kernel, tm=tm),
        grid=(b, nt),
        in_specs=[idx(), idx(), pl.BlockSpec(memory_space=pl.ANY), row(LANES), row(d),
                  pl.BlockSpec((None, 6, d), lambda bi, i: (bi, 0, 0)),
                  pl.BlockSpec((1, d), lambda bi, i: (0, 0))],
        out_specs=row(d),
        out_shape=jax.ShapeDtypeStruct((b, t, d), F32),
        scratch_shapes=[pltpu.VMEM((tm, d), F32), pltpu.VMEM((tm, d), F32), pltpu.SemaphoreType.DMA((2,))],
        compiler_params=_cp("arbitrary", "arbitrary"),
        name="moe_combine",
    )(slot1.reshape(b * nt, 1, tm), slot2.reshape(b * nt, 1, tm), ys, route, x3, mod1, post_g.reshape(1, d))


def _rope_tables(t_len):
    rows = t_len // GRID_W
    row = jnp.repeat(jnp.arange(rows, dtype=F32), GRID_W)
    col = jnp.tile(jnp.arange(GRID_W, dtype=F32), rows)
    axis_dim = DIFF_DQK // 2
    inv = ROPE_BASE ** (-jnp.arange(0, axis_dim, 2, dtype=F32) / axis_dim)
    ang_r = row[:, None] * inv
    ang_c = col[:, None] * inv
    cr, sr, cc, sc = jnp.cos(ang_r), jnp.sin(ang_r), jnp.cos(ang_c), jnp.sin(ang_c)
    cos = jnp.concatenate([cr, cr, cc, cc] * 2, axis=1)
    sin = jnp.concatenate([-sr, sr, -sc, sc] * 2, axis=1)
    return cos, sin


def kernel(x, c, ctx, c_ctx, l0_mod_w, l0_mod_b, l0_mix_pre_g, l0_mix_post_g, l0_w_in, l0_mlstm_gate_b, l0_mlstm_conv_w, l0_mlstm_norm_g, l0_lambda_q1, l0_lambda_k1, l0_lambda_q2, l0_lambda_k2, l0_diff_norm_g, l0_w_out, l0_ffn_pre_g, l0_ffn_post_g, l0_ffn_w1, l0_ffn_w3, l0_ffn_w2, l1_mod_w, l1_mod_b, l1_mix_pre_g, l1_mix_post_g, l1_conv_pw1_w, l1_conv_pw1_b, l1_conv_dw_w, l1_conv_dw_b, l1_conv_ln_g, l1_conv_ln_b, l1_conv_pw2_w, l1_conv_pw2_b, l1_ffn_pre_g, l1_ffn_post_g, l1_router_w, l1_moe_w1, l1_moe_w3, l1_moe_w2):
    b, t, d = x.shape
    n_ctx = ctx.shape[1]
    L = MLSTM_CHUNK
    assert d == D_MODEL and n_ctx == L and t % (2 * L) == 0 and b <= 8

    cpad = jnp.zeros((16, d), F32).at[:b].set(c).at[8].set(c_ctx)
    mod0 = _adaln(cpad, l0_mod_w, l0_mod_b).reshape(16, 6, d)
    mod1 = _adaln(cpad, l1_mod_w, l1_mod_b).reshape(16, 6, d)

    w_perm = jnp.concatenate(
        [l0_w_in[:, :2048], l0_w_in[:, 2080:], l0_w_in[:, 2048:2080], jnp.zeros((d, 96), F32)], axis=1).astype(BF16)
    gate_b = jnp.concatenate([l0_mlstm_gate_b, jnp.zeros((96,), F32)]).reshape(1, LANES)
    cos, sin = _rope_tables(t)
    g0 = l0_mix_pre_g.reshape(1, d)
    mq, mk, mv, mo, dq, dk, dv, gates = _inproj(x, mod0, None, g0, w_perm, l0_mlstm_conv_w, gate_b, cos, sin,
                                                 tm=512, rope=True)
    _, mkc, mvc, _, _, dkc, dvc, gates_c = _inproj(ctx, mod0, 8, g0, w_perm, l0_mlstm_conv_w, gate_b,
                                                   cos[:n_ctx], sin[:n_ctx], tm=n_ctx, rope=False)

    arow, cols = _gateprep(jnp.concatenate([gates_c, gates], axis=1), L=L)
    hm = _mlstm(mq, mk, mv, mo, mkc, mvc, arow, cols, l0_mlstm_norm_g, L=L)

    lam_init = 0.8 - 0.6 * math.exp(-0.3 * 0)
    lam_in = jnp.zeros((8, LANES), F32).at[0, :DIFF_DQK].set(l0_lambda_q1).at[1, :DIFF_DQK].set(l0_lambda_k1)
    lam_in = lam_in.at[2, :DIFF_DQK].set(l0_lambda_q2).at[3, :DIFF_DQK].set(l0_lambda_k2)
    hd = _attn(dq, dk, dv, dkc, dvc, lam_in, l0_diff_norm_g, tq=256, kb=512, lam_init=lam_init)

    w_out = l0_w_out.astype(BF16)
    x1, u2 = _outproj(hm, hd, w_out[:512], w_out[512:], x, mod0, l0_mix_post_g, l0_ffn_pre_g, tm=512)
    x2, u3 = _ffn(u2, l0_ffn_w1.astype(BF16), l0_ffn_w3.astype(BF16), l0_ffn_w2.astype(BF16), x1, mod0, mod1,
                  l0_ffn_post_g, l1_mix_pre_g, tm=512, tf=1408)

    dw = jnp.concatenate([l1_conv_dw_w, jnp.zeros((1, d), F32)], axis=0)
    rw = jnp.concatenate([l1_router_w, jnp.zeros((d, LANES - N_EXPERTS), F32)], axis=1).astype(BF16)
    v1 = lambda a: a.reshape(1, -1)
    x3, u4, route, counts = _convmod(
        u3, l1_conv_pw1_w.astype(BF16), v1(l1_conv_pw1_b), dw, v1(l1_conv_dw_b), v1(l1_conv_ln_g),
        v1(l1_conv_ln_b), l1_conv_pw2_w.astype(BF16), v1(l1_conv_pw2_b), x2, mod1, v1(l1_mix_post_g),
        v1(l1_ffn_pre_g), rw, tm=512)

    n = b * t
    rows = MOE_ROWS
    rt = route.reshape(n, LANES)
    e1, e2 = rt[:, 0].astype(jnp.int32), rt[:, 1].astype(jnp.int32)
    r1, r2 = rt[:, 4].astype(jnp.int32), rt[:, 5].astype(jnp.int32)
    cnt = counts[0, :N_EXPERTS].astype(jnp.int32)
    padded = ((cnt + rows - 1) // rows) * rows
    pends = jnp.cumsum(padded)
    pstarts = pends - padded
    slot1 = pstarts[e1] + r1
    slot2 = pstarts[e2] + r2
    n_blk = (2 * n) // rows + N_EXPERTS
    tok = jnp.arange(n, dtype=jnp.int32)
    slot_tok = jnp.zeros((n_blk * rows,), jnp.int32).at[slot1].set(tok).at[slot2].set(tok)
    block_e = jnp.clip(jnp.searchsorted(pends, jnp.arange(n_blk, dtype=jnp.int32) * rows, side='right'),
                       0, N_EXPERTS - 1).astype(jnp.int32)

    xs = _gather_rows(u4.reshape(n, d), slot_tok, rb=256)
    ys = _moe_ffn(xs, block_e, l1_moe_w1.astype(BF16), l1_moe_w3.astype(BF16), l1_moe_w2.astype(BF16),
                  rows=rows, tf=1408)
    return _combine(slot1, slot2, ys, route, x3, mod1, l1_ffn_post_g, tm=256)
```

```python
import functools
import math

import jax
import jax.numpy as jnp
from jax import lax
from jax.experimental import pallas as pl
from jax.experimental.pallas import tpu as pltpu

F32 = jnp.float32
BF16 = jnp.bfloat16
EPS = 1e-6
NEG = -1e30
LOG2E = 1.4426950408889634

D_MODEL = 1024
GRID_W = 64
MLSTM_HEADS = 8
MLSTM_D = 64
MLSTM_CHUNK = 256
DIFF_HEADS = 4
DIFF_DQK = 64
ROPE_BASE = 10000.0
CONV_WIDTH = 31
N_EXPERTS = 8
MOE_ROWS = 512
LANES = 128
VMEM_LIMIT = 52 * 1024 * 1024


def _cp(*sem):
    return pltpu.CompilerParams(dimension_semantics=sem, vmem_limit_bytes=VMEM_LIMIT)


def _rms(x, g):
    return x * lax.rsqrt(jnp.mean(x * x, axis=-1, keepdims=True) + EPS) * g


def _silu(x):
    return x * jax.nn.sigmoid(x)


def _dot(a, b):
    return jnp.dot(a, b, preferred_element_type=F32)


def _dot_nt(a, b):
    return lax.dot_general(a, b, (((1,), (1,)), ((), ())), preferred_element_type=F32)


def _dot_tn(a, b):
    return lax.dot_general(a, b, (((0,), (0,)), ((), ())), preferred_element_type=F32)


def _adaln_kernel(c_ref, w_ref, b_ref, o_ref):
    s = _silu(c_ref[...])
    o_ref[...] = _dot(s.astype(BF16), w_ref[...].astype(BF16)) + b_ref[...]


def _adaln(cpad, w, b):
    rows, d = cpad.shape
    n = w.shape[1]
    tn = 1536
    return pl.pallas_call(
        _adaln_kernel,
        grid=(n // tn,),
        in_specs=[pl.BlockSpec((rows, d), lambda j: (0, 0)),
                  pl.BlockSpec((d, tn), lambda j: (0, j)),
                  pl.BlockSpec((1, tn), lambda j: (0, j))],
        out_specs=pl.BlockSpec((rows, tn), lambda j: (0, j)),
        out_shape=jax.ShapeDtypeStruct((rows, n), F32),
        compiler_params=_cp("arbitrary"),
        name="adaln",
    )(cpad, w, b.reshape(1, n))


def _inproj_kernel(x_ref, xp_ref, xn_ref, mod_ref, g_ref, w_ref, cw_ref, gb_ref, cos_ref, sin_ref,
                   mq_ref, mk_ref, mv_ref, mo_ref, dq_ref, dk_ref, dv_ref, gt_ref, pext_ref,
                   *, tm, nt, rope):
    i = pl.program_id(1)
    g = g_ref[...]
    sh = mod_ref[0:1, :]
    sc = mod_ref[1:2, :]

    def mod(xv):
        return _rms(xv, g) * (1.0 + sc) + sh

    u = mod(x_ref[...])
    up = jnp.where(i > 0, mod(xp_ref[...]), 0.0)
    un = jnp.where(i < nt - 1, mod(xn_ref[...]), 0.0)
    ub = u.astype(BF16)
    uext = jnp.concatenate([up.astype(BF16), ub, un.astype(BF16)], axis=0)

    pext_ref[...] = _dot(uext, w_ref[:, 0:1024])
    cw = cw_ref[...]
    conv = (cw[0:1, :] * pext_ref[pl.ds(7, tm), :] + cw[1:2, :] * pext_ref[pl.ds(8, tm), :]
            + cw[2:3, :] * pext_ref[pl.ds(9, tm), :])
    act = _silu(conv)
    mq_ref[...] = (act[:, 0:512] * (MLSTM_D ** -0.5)).astype(BF16)
    mk_ref[...] = act[:, 512:1024].astype(BF16)

    p = _dot(ub, w_ref[:, 1024:2048])
    mv_ref[...] = p[:, 0:512].astype(BF16)
    mo_ref[...] = p[:, 512:1024].astype(BF16)

    p = _dot(ub, w_ref[:, 2048:3072])
    if rope:
        lane = lax.broadcasted_iota(jnp.int32, p.shape, 1)
        first_half = ((lane // 16) % 2) == 0
        nl = p.shape[1]
        partner = jnp.where(first_half, pltpu.roll(p, nl - 16, 1), pltpu.roll(p, 16, 1))
        cos = jnp.concatenate([cos_ref[...]] * 8, axis=1)
        sin = jnp.concatenate([sin_ref[...]] * 8, axis=1)
        p = p * cos + partner * sin
    dq_ref[...] = (p[:, 0:512] * (LOG2E * DIFF_DQK ** -0.5)).astype(BF16)
    dk_ref[...] = p[:, 512:1024].astype(BF16)

    p = _dot(ub, w_ref[:, 3072:3712])
    dv_ref[...] = p[:, 0:512].astype(BF16)
    gt_ref[...] = p[:, 512:640] + gb_ref[...]


def _inproj(x, mod, mod_row, g, w_perm, conv_w, gate_b, cos, sin, *, tm, rope):
    b, t, d = x.shape
    nt = t // tm
    hb = tm // 8
    nb8 = t // 8
    bf = lambda: jax.ShapeDtypeStruct((b, t, 512), BF16)
    if mod_row is None:
        mod_map = lambda bi, i: (bi, 0, 0)
    else:
        mod_map = lambda bi, i: (mod_row, 0, 0)
    kern = functools.partial(_inproj_kernel, tm=tm, nt=nt, rope=rope)
    o512 = pl.BlockSpec((None, tm, 512), lambda bi, i: (bi, i, 0))
    return pl.pallas_call(
        kern,
        grid=(b, nt),
        in_specs=[
            pl.BlockSpec((None, tm, d), lambda bi, i: (bi, i, 0)),
            pl.BlockSpec((None, 8, d), lambda bi, i: (bi, jnp.maximum(i * hb - 1, 0), 0)),
            pl.BlockSpec((None, 8, d), lambda bi, i: (bi, jnp.minimum((i + 1) * hb, nb8 - 1), 0)),
            pl.BlockSpec((None, 6, d), mod_map),
            pl.BlockSpec((1, d), lambda bi, i: (0, 0)),
            pl.BlockSpec((d, 3712), lambda bi, i: (0, 0)),
            pl.BlockSpec((3, d), lambda bi, i: (0, 0)),
            pl.BlockSpec((1, LANES), lambda bi, i: (0, 0)),
            pl.BlockSpec((tm, LANES), lambda bi, i: (i, 0)),
            pl.BlockSpec((tm, LANES), lambda bi, i: (i, 0)),
        ],
        out_specs=[o512] * 7 + [pl.BlockSpec((None, tm, LANES), lambda bi, i: (bi, i, 0))],
        out_shape=[bf() for _ in range(7)] + [jax.ShapeDtypeStruct((b, t, LANES), F32)],
        scratch_shapes=[pltpu.VMEM((tm + 16, 1024), F32)],
        compiler_params=_cp("arbitrary", "arbitrary"),
        name="inproj_rope" if rope else "inproj_ctx",
    )(x, x, x, mod, g, w_perm, conv_w, gate_b, cos, sin)


def _gateprep_kernel(g_ref, arow_ref, col_ref, *, L, nch):
    tt = L * nch
    gt = g_ref[...].T
    i_f, f_f, i_b, f_b = gt[0:8], gt[8:16], gt[16:24], gt[24:32]

    def logsig(v):
        return jnp.minimum(v, 0.0) - jnp.log1p(jnp.exp(-jnp.abs(v)))

    pos = lax.broadcasted_iota(jnp.int32, (8, tt), 1) % L

    def scan(v, op, ident, reverse):
        s = 1
        while s < L:
            if reverse:
                shifted = pltpu.roll(v, tt - s, 1)
                valid = pos < L - s
            else:
                shifted = pltpu.roll(v, s, 1)
                valid = pos >= s
            v = op(v, jnp.where(valid, shifted, ident))
            s *= 2
        return v

    outs = []
    for d, (ig, fg) in enumerate(((i_f, f_f), (i_b, f_b))):
        rev = d == 1
        bcum = scan(logsig(fg), jnp.add, 0.0, rev)
        a = ig - bcum
        cm = scan(a, jnp.maximum, NEG, rev)
        order = list(range(nch)) if not rev else [0] + list(range(nch - 1, 0, -1))
        mp = jnp.zeros((8, 1), F32)
        mp_c = [None] * nch
        ml_c = [None] * nch
        for j in order:
            e = j * L if rev else j * L + L - 1
            mlast = jnp.maximum(mp, cm[:, e:e + 1])
            mp_c[j] = jnp.broadcast_to(mp, (8, L))
            ml_c[j] = jnp.broadcast_to(mlast, (8, L))
            mp = bcum[:, e:e + 1] + mlast
        mprev = jnp.concatenate(mp_c, axis=1)
        mlast = jnp.concatenate(ml_c, axis=1)
        m = jnp.maximum(mprev, cm)
        outs.append((a, m, jnp.exp(mprev - m), jnp.exp(a - mlast), jnp.exp(-(bcum + m))))

    arow_ref[...] = jnp.concatenate([outs[0][0], outs[1][0]], axis=0)
    rows = []
    for pair in range(4):
        for q in range(1, 5):
            for d in range(2):
                rows.append(outs[d][q][2 * pair:2 * pair + 2])
    rows.append(jnp.zeros((64, tt), F32))
    col_ref[...] = jnp.concatenate(rows, axis=0).T


def _gateprep(gall, *, L):
    b, tt, _ = gall.shape
    nch = tt // L
    return pl.pallas_call(
        functools.partial(_gateprep_kernel, L=L, nch=nch),
        grid=(b,),
        in_specs=[pl.BlockSpec((None, tt, LANES), lambda bi: (bi, 0, 0))],
        out_specs=[pl.BlockSpec((None, 16, tt), lambda bi: (bi, 0, 0)),
                   pl.BlockSpec((None, tt, LANES), lambda bi: (bi, 0, 0))],
        out_shape=[jax.ShapeDtypeStruct((b, 16, tt), F32), jax.ShapeDtypeStruct((b, tt, LANES), F32)],
        compiler_params=_cp("arbitrary"),
        name="mlstm_gateprep",
    )(gall)


def _mlstm_kernel(q_ref, k_ref, v_ref, o_ref, kc_ref, vc_ref, arow_ref, col_ref, ng_ref, out_ref,
                  c_ref, hf_ref, hb_ref, *, L, nc):
    p = pl.program_id(1)
    half = nc // 2
    lane = lax.broadcasted_iota(jnp.int32, (L, LANES), 1)
    lo = lane < MLSTM_D
    head_mask = (lo, jnp.logical_not(lo))
    ri = lax.broadcasted_iota(jnp.int32, (L, L), 0)
    ci = lax.broadcasted_iota(jnp.int32, (L, L), 1)
    causal = (ci <= ri, ci >= ri)
    ones_t = jnp.ones((L, LANES), BF16)
    shift = lax.rem(LANES - 16 * p, LANES)

    def cols(off):
        return pltpu.roll(col_ref[pl.ds(off, L), :], shift, 1)

    def col(blk, q, d, hh):
        j = q * 4 + d * 2 + hh
        return blk[:, j:j + 1]

    def vext_of(vb, hh):
        return jnp.concatenate([jnp.where(head_mask[hh], vb, jnp.zeros_like(vb)), ones_t], axis=1)

    def state_update(d, hh, kb, vext, blk, dec):
        khm = jnp.where(head_mask[hh], kb, jnp.zeros_like(kb))
        wv = (col(blk, 2, d, hh) * vext.astype(F32)).astype(BF16)
        upd = _dot_tn(khm, wv)
        if dec is None:
            c_ref[d, hh] = upd
        else:
            c_ref[d, hh] = dec * c_ref[d, hh] + upd

    blk0 = cols(0)
    kcb = kc_ref[...]
    vcb = vc_ref[...]
    for d in range(2):
        for hh in range(2):
            state_update(d, hh, kcb, vext_of(vcb, hh), blk0, None)

    def compute(d, c):
        t0 = pl.multiple_of(c * L, L)
        off = pl.multiple_of(c * L + L, L)
        qb = q_ref[pl.ds(t0, L), :]
        kb = k_ref[pl.ds(t0, L), :]
        vb = v_ref[pl.ds(t0, L), :]
        blk = cols(off)
        last = L - 1 if d == 0 else 0
        hs = []
        for hh in range(2):
            arow = arow_ref[pl.ds(d * 8 + 2 * p + hh, 1), pl.ds(off, L)]
            khm = jnp.where(head_mask[hh], kb, jnp.zeros_like(kb))
            s = _dot_nt(qb, khm)
            arg = jnp.where(causal[d], arow - col(blk, 0, d, hh), NEG)
            pm = (s * jnp.exp(arg)).astype(BF16)
            vext = vext_of(vb, hh)
            ch = c_ref[d, hh]
            tot = _dot(pm, vext) + col(blk, 1, d, hh) * _dot(qb, ch.astype(BF16))
            den = jnp.maximum(jnp.abs(tot[:, LANES:]), col(blk, 3, d, hh))
            hs.append(tot[:, :LANES] / den)
            dec = col(blk, 1, d, hh)[last:last + 1, :]
            state_update(d, hh, kb, vext, blk, dec)
        return jnp.where(lo, hs[0], hs[1])

    def finalize(c, hsum):
        t0 = pl.multiple_of(c * L, L)
        sq = hsum * hsum
        s0 = jnp.sum(jnp.where(lo, sq, 0.0), axis=-1, keepdims=True)
        s1 = jnp.sum(jnp.where(lo, 0.0, sq), axis=-1, keepdims=True)
        ms = jnp.where(lo, s0, s1) * (1.0 / MLSTM_D)
        y = hsum * lax.rsqrt(ms + EPS) * ng_ref[...]
        gate = jax.nn.sigmoid(o_ref[pl.ds(t0, L), :].astype(F32))
        out_ref[pl.ds(t0, L), :] = (y * gate).astype(BF16)

    def phase_a(i, carry):
        hf_ref[pl.ds(pl.multiple_of(i * L, L), L), :] = compute(0, i)
        cb = nc - 1 - i
        hb_ref[pl.ds(pl.multiple_of((cb - half) * L, L), L), :] = compute(1, cb)
        return carry

    def phase_b(i, carry):
        hf = compute(0, i)
        finalize(i, hf + hb_ref[pl.ds(pl.multiple_of((i - half) * L, L), L), :])
        cb = nc - 1 - i
        hb = compute(1, cb)
        finalize(cb, hb + hf_ref[pl.ds(pl.multiple_of(cb * L, L), L), :])
        return carry

    lax.fori_loop(0, half, phase_a, 0)
    lax.fori_loop(half, nc, phase_b, 0)


def _mlstm(mq, mk, mv, mo, mkc, mvc, arow, cols, norm_g, *, L):
    b, t, _ = mq.shape
    ctx = mkc.shape[1]
    tt = arow.shape[2]
    nc = t // L
    tok = pl.BlockSpec((None, t, LANES), lambda bi, p: (bi, 0, p))
    ctxs = pl.BlockSpec((None, ctx, LANES), lambda bi, p: (bi, 0, p))
    return pl.pallas_call(
        functools.partial(_mlstm_kernel, L=L, nc=nc),
        grid=(b, 4),
        in_specs=[tok, tok, tok, tok, ctxs, ctxs,
                  pl.BlockSpec((None, 16, tt), lambda bi, p: (bi, 0, 0)),
                  pl.BlockSpec((None, tt, LANES), lambda bi, p: (bi, 0, 0)),
                  pl.BlockSpec((1, LANES), lambda bi, p: (0, p))],
        out_specs=tok,
        out_shape=jax.ShapeDtypeStruct((b, t, 512), BF16),
        scratch_shapes=[pltpu.VMEM((2, 2, LANES, 2 * LANES), F32),
                        pltpu.VMEM((t // 2, LANES), F32),
                        pltpu.VMEM((t // 2, LANES), F32)],
        compiler_params=_cp("arbitrary", "arbitrary"),
        name="mlstm_scan",
    )(mq, mk, mv, mo, mkc, mvc, arow, cols, norm_g.reshape(1, 512))


def _attn_kernel(q_ref, k_ref, v_ref, lam_ref, ng_ref, out_ref, kmax_ref, m_ref, acc_ref,
                 *, tq, kb, nkb, sub, lam_init):
    i = pl.program_id(2)
    lane = lax.broadcasted_iota(jnp.int32, (tq, LANES), 1)
    lo = lane < DIFF_DQK
    rr = lax.broadcasted_iota(jnp.int32, (LANES, LANES), 0)
    cc = lax.broadcasted_iota(jnp.int32, (LANES, LANES), 1)
    same_comp = ((rr < DIFF_DQK) == (cc < DIFF_DQK)).astype(BF16)

    def comp_sqnorm(a):
        af = a.astype(F32)
        return _dot((af * af).astype(BF16), same_comp)

    @pl.when(i == 0)
    def _():
        mx = jnp.zeros((1, LANES), F32)

        def kbody(j, mx):
            k0 = pl.multiple_of(j * kb, kb)
            return jnp.maximum(mx, jnp.max(comp_sqnorm(k_ref[pl.ds(k0, kb), :]), axis=0, keepdims=True))

        kmax_ref[...] = lax.fori_loop(0, nkb, kbody, mx)

    q = q_ref[...]
    zq = jnp.zeros_like(q)
    bnd = jnp.sqrt(comp_sqnorm(q) * kmax_ref[...]) * 1.02
    b1 = jnp.where(lane == 0, -bnd, 0.0)
    b2 = jnp.where(lane == 0, -pltpu.roll(bnd, DIFF_DQK, 1), 0.0)
    qs = jnp.concatenate(
        [jnp.concatenate([jnp.where(lo, q, zq), jnp.where(lo, zq, q)], axis=0),
         jnp.concatenate([b1, b2], axis=0).astype(BF16)], axis=1)
    acc_ref[...] = jnp.zeros(acc_ref.shape, F32)

    def ext(blk):
        return jnp.concatenate([blk, jnp.ones_like(blk)], axis=1)

    def run(step):
        def body(j, carry):
            k0 = pl.multiple_of(j * kb, kb)
            step(k_ref[pl.ds(k0, kb), :], v_ref[pl.ds(k0, kb), :])
            return carry

        lax.fori_loop(0, nkb, body, 0)

    def fast_step(kblk, vblk):
        tot = None
        for c in range(0, kblk.shape[0], sub):
            pm = jnp.exp2(_dot_nt(qs, ext(kblk[c:c + sub]))).astype(BF16)
            part = _dot(pm, ext(vblk[c:c + sub]))
            tot = part if tot is None else tot + part
        acc_ref[...] += tot

    def slow_step(kblk, vblk):
        s = _dot_nt(qs, ext(kblk))
        m_old = m_ref[...]
        m_new = jnp.maximum(m_old, jnp.max(s, axis=-1, keepdims=True))
        pm = jnp.exp2(s - m_new).astype(BF16)
        acc_ref[...] = jnp.exp2(m_old - m_new) * acc_ref[...] + _dot(pm, ext(vblk))
        m_ref[...] = m_new

    fast = jnp.max(bnd) <= 56.0

    @pl.when(fast)
    def _():
        run(fast_step)

    @pl.when(jnp.logical_not(fast))
    def _():
        m_ref[...] = jnp.full(m_ref.shape, NEG, F32)
        run(slow_step)

    lq = lam_ref[...]
    lam = (jnp.exp(jnp.sum(lq[0:1, :] * lq[1:2, :], axis=-1, keepdims=True))
           - jnp.exp(jnp.sum(lq[2:3, :] * lq[3:4, :], axis=-1, keepdims=True)) + lam_init)
    a1 = acc_ref[0:tq, :]
    a2 = acc_ref[tq:2 * tq, :]
    o = a1[:, :LANES] / a1[:, LANES:] - lam * (a2[:, :LANES] / a2[:, LANES:])
    out_ref[...] = (_rms(o, ng_ref[...]) * (1.0 - lam_init)).astype(BF16)


def _attn(dq, dk, dv, lam_in, norm_g, *, tq, sub, max_sub_per_step, lam_init):
    b, t, _ = dq.shape
    tk = dk.shape[1]
    nsub = tk // sub
    per = max(g for g in range(1, max_sub_per_step + 1) if nsub % g == 0)
    kb = per * sub
    full = pl.BlockSpec((None, tk, LANES), lambda bi, h, i: (bi, 0, h))
    qs = pl.BlockSpec((None, tq, LANES), lambda bi, h, i: (bi, i, h))
    return pl.pallas_call(
        functools.partial(_attn_kernel, tq=tq, kb=kb, nkb=tk // kb, sub=sub, lam_init=lam_init),
        grid=(b, DIFF_HEADS, t // tq),
        in_specs=[qs, full, full,
                  pl.BlockSpec((8, LANES), lambda bi, h, i: (0, 0)),
                  pl.BlockSpec((1, LANES), lambda bi, h, i: (0, h))],
        out_specs=qs,
        out_shape=jax.ShapeDtypeStruct((b, t, 512), BF16),
        scratch_shapes=[pltpu.VMEM((1, LANES), F32), pltpu.VMEM((2 * tq, 1), F32),
                        pltpu.VMEM((2 * tq, 2 * LANES), F32)],
        compiler_params=_cp("arbitrary", "arbitrary", "arbitrary"),
        name="diff_attn",
    )(dq, dk, dv, lam_in, norm_g.reshape(1, 512))


def _outproj_kernel(hm_ref, hd_ref, wt_ref, wb_ref, x_ref, mod_ref, pg_ref, fg_ref, x1_ref, u_ref):
    mix = _dot(hm_ref[...], wt_ref[...]) + _dot(hd_ref[...], wb_ref[...])
    x1 = x_ref[...] + mod_ref[2:3, :] * _rms(mix, pg_ref[...])
    x1_ref[...] = x1
    u_ref[...] = (_rms(x1, fg_ref[...]) * (1.0 + mod_ref[4:5, :]) + mod_ref[3:4, :]).astype(BF16)


def _outproj(hm, hd, wt, wb, x, mod, post_g, ffn_pre_g, *, tm):
    b, t, d = x.shape
    row = lambda w: pl.BlockSpec((None, tm, w), lambda bi, i: (bi, i, 0))
    cst = lambda s: pl.BlockSpec(s, lambda bi, i: (0,) * len(s))
    return pl.pallas_call(
        _outproj_kernel,
        grid=(b, t // tm),
        in_specs=[row(512), row(512), cst((512, d)), cst((512, d)), row(d),
                  pl.BlockSpec((None, 6, d), lambda bi, i: (bi, 0, 0)), cst((1, d)), cst((1, d))],
        out_specs=[row(d), row(d)],
        out_shape=[jax.ShapeDtypeStruct((b, t, d), F32), jax.ShapeDtypeStruct((b, t, d), BF16)],
        compiler_params=_cp("arbitrary", "arbitrary"),
        name="outproj",
    )(hm, hd, wt, wb, x, mod, post_g.reshape(1, d), ffn_pre_g.reshape(1, d))


def _ffn_kernel(u_ref, w1_ref, w3_ref, w2_ref, x_ref, mod0_ref, mod1_ref, pg_ref, ng_ref,
                x2_ref, u3_ref, acc_ref, *, nf):
    j = pl.program_id(2)

    @pl.when(j == 0)
    def _():
        acc_ref[...] = jnp.zeros(acc_ref.shape, F32)

    u = u_ref[...]
    h = _silu(_dot(u, w1_ref[...])) * _dot(u, w3_ref[...])
    acc_ref[...] += _dot(h.astype(BF16), w2_ref[...])

    @pl.when(j == nf - 1)
    def _():
        x2 = x_ref[...] + mod0_ref[5:6, :] * _rms(acc_ref[...], pg_ref[...])
        x2_ref[...] = x2
        u3_ref[...] = (_rms(x2, ng_ref[...]) * (1.0 + mod1_ref[1:2, :]) + mod1_ref[0:1, :]).astype(BF16)


def _ffn(u, w1, w3, w2, x1, mod0, mod1, post_g, next_pre_g, *, tm, tf):
    b, t, d = x1.shape
    f = w1.shape[1]
    nf = f // tf
    row = pl.BlockSpec((None, tm, d), lambda bi, i, j: (bi, i, 0))
    modb = pl.BlockSpec((None, 6, d), lambda bi, i, j: (bi, 0, 0))
    vec = pl.BlockSpec((1, d), lambda bi, i, j: (0, 0))
    return pl.pallas_call(
        functools.partial(_ffn_kernel, nf=nf),
        grid=(b, t // tm, nf),
        in_specs=[row, pl.BlockSpec((d, tf), lambda bi, i, j: (0, j)),
                  pl.BlockSpec((d, tf), lambda bi, i, j: (0, j)),
                  pl.BlockSpec((tf, d), lambda bi, i, j: (j, 0)), row, modb, modb, vec, vec],
        out_specs=[row, row],
        out_shape=[jax.ShapeDtypeStruct((b, t, d), F32), jax.ShapeDtypeStruct((b, t, d), BF16)],
        scratch_shapes=[pltpu.VMEM((tm, d), F32)],
        compiler_params=_cp("arbitrary", "arbitrary", "arbitrary"),
        name="ffn_swiglu",
    )(u, w1, w3, w2, x1, mod0, mod1, post_g.reshape(1, d), next_pre_g.reshape(1, d))


HALO = 16
CONV_ROW_BLOCK = 32
CONV_ROWS_EXTRA = 24


def _conv_kernel(u_ref, up_ref, un_ref, w1_ref, b1_ref, dw_ref, dwb_ref, lng_ref, lnb_ref, w2_ref, b2_ref,
                 x_ref, mod_ref, pg_ref, fg_ref, rw_ref, x3_ref, u4_ref, route_ref, cnt_ref,
                 hs_ref, sh_ref, cv_ref, wb_ref, run_ref, *, tm, nt):
    bi = pl.program_id(0)
    i = pl.program_id(1)

    @pl.when((bi == 0) & (i == 0))
    def _():
        run_ref[...] = jnp.zeros(run_ref.shape, F32)

    uext = jnp.concatenate([up_ref[...], u_ref[...], un_ref[...]], axis=0)
    ag = _dot(uext, w1_ref[...]) + b1_ref[...]
    h = ag[:, :D_MODEL] * jax.nn.sigmoid(ag[:, D_MODEL:])
    row = lax.broadcasted_iota(jnp.int32, (tm + 2 * HALO, 1), 0)
    keep = ((row >= HALO) | (i > 0)) & ((row < tm + HALO) | (i < nt - 1))
    hs_ref[...] = jnp.where(keep, h, 0.0)

    for r in range(1, 8):
        sh_ref[r - 1] = hs_ref[pl.ds(r, tm + CONV_ROWS_EXTRA), :]

    for j in range(CONV_WIDTH):
        wb_ref[j] = jnp.broadcast_to(dw_ref[j:j + 1, :], (8, D_MODEL))
    nsub = CONV_ROW_BLOCK // 8

    def conv_rows(rb, carry):
        r0 = pl.multiple_of(rb * CONV_ROW_BLOCK, CONV_ROW_BLOCK)
        acc = jnp.broadcast_to(dwb_ref[...].reshape(1, 1, D_MODEL), (nsub, 8, D_MODEL))
        for j in range(CONV_WIDTH):
            r, a = (j + 1) % 8, (j + 1) // 8
            if r == 0:
                win = hs_ref[pl.ds(r0 + 8 * a, CONV_ROW_BLOCK), :]
            else:
                win = sh_ref[r - 1, pl.ds(r0 + 8 * a, CONV_ROW_BLOCK), :]
            acc = acc + wb_ref[j][None] * win.reshape(nsub, 8, D_MODEL)
        cv_ref[pl.ds(r0, CONV_ROW_BLOCK), :] = acc.reshape(CONV_ROW_BLOCK, D_MODEL)
        return carry

    lax.fori_loop(0, tm // CONV_ROW_BLOCK, conv_rows, 0)
    acc = cv_ref[...]
    mu = jnp.mean(acc, axis=-1, keepdims=True)
    cen = acc - mu
    var = jnp.mean(cen * cen, axis=-1, keepdims=True)
    hn = _silu(cen * lax.rsqrt(var + EPS) * lng_ref[...] + lnb_ref[...])
    y = _dot(hn.astype(BF16), w2_ref[...]) + b2_ref[...]
    x3 = x_ref[...] + mod_ref[2:3, :] * _rms(y, pg_ref[...])
    x3_ref[...] = x3
    u4 = _rms(x3, fg_ref[...]) * (1.0 + mod_ref[4:5, :]) + mod_ref[3:4, :]
    u4_ref[...] = u4
    u4b = u4.astype(BF16)

    lane = lax.broadcasted_iota(jnp.int32, (tm, LANES), 1).astype(F32)
    logits = jnp.where(lane < N_EXPERTS, _dot(u4b, rw_ref[...]), NEG)
    m1 = jnp.max(logits, axis=-1, keepdims=True)
    i1 = jnp.min(jnp.where(logits == m1, lane, float(LANES)), axis=-1, keepdims=True)
    l2 = jnp.where(lane == i1, NEG, logits)
    m2 = jnp.max(l2, axis=-1, keepdims=True)
    i2 = jnp.min(jnp.where(l2 == m2, lane, float(LANES)), axis=-1, keepdims=True)
    e21 = jnp.exp(m2 - m1)
    g1 = 1.0 / (1.0 + e21)
    g2 = e21 * g1
    sel = ((lane == i1) | (lane == i2)).astype(F32)
    ri = lax.broadcasted_iota(jnp.int32, (tm, tm), 0)
    ci = lax.broadcasted_iota(jnp.int32, (tm, tm), 1)
    tri = (ci <= ri).astype(BF16)
    csum = _dot(tri, sel.astype(BF16))
    rank = csum - sel + run_ref[0:1, :]
    r1 = jnp.sum(jnp.where(lane == i1, rank, 0.0), axis=-1, keepdims=True)
    r2 = jnp.sum(jnp.where(lane == i2, rank, 0.0), axis=-1, keepdims=True)
    run_ref[...] = run_ref[...] + csum[tm - 1:tm, :]
    cnt_ref[...] = run_ref[...]
    route = jnp.zeros((tm, LANES), F32)
    for n, v in enumerate((i1, i2, g1, g2, r1, r2)):
        route = jnp.where(lane == float(n), v, route)
    route_ref[...] = route


def _convmod(u3, w1, b1, dw, dwb, lng, lnb, w2, b2, x2, mod1, post_g, ffn_pre_g, rw, *, tm):
    b, t, d = x2.shape
    nt = t // tm
    hb = tm // HALO
    nbh = t // HALO
    row = lambda w: pl.BlockSpec((None, tm, w), lambda bi, i: (bi, i, 0))
    cst = lambda s: pl.BlockSpec(s, lambda bi, i: (0,) * len(s))
    return pl.pallas_call(
        functools.partial(_conv_kernel, tm=tm, nt=nt),
        grid=(b, nt),
        in_specs=[row(d),
                  pl.BlockSpec((None, HALO, d), lambda bi, i: (bi, jnp.maximum(i * hb - 1, 0), 0)),
                  pl.BlockSpec((None, HALO, d), lambda bi, i: (bi, jnp.minimum((i + 1) * hb, nbh - 1), 0)),
                  cst((d, 2 * d)), cst((1, 2 * d)), cst((32, d)), cst((1, d)), cst((1, d)), cst((1, d)),
                  cst((d, d)), cst((1, d)), row(d),
                  pl.BlockSpec((None, 6, d), lambda bi, i: (bi, 0, 0)), cst((1, d)), cst((1, d)),
                  cst((d, LANES))],
        out_specs=[row(d), row(d), row(LANES), cst((8, LANES))],
        out_shape=[jax.ShapeDtypeStruct((b, t, d), F32), jax.ShapeDtypeStruct((b, t, d), F32),
                   jax.ShapeDtypeStruct((b, t, LANES), F32), jax.ShapeDtypeStruct((8, LANES), F32)],
        scratch_shapes=[pltpu.VMEM((tm + 2 * HALO, d), F32), pltpu.VMEM((7, tm + CONV_ROWS_EXTRA, d), F32),
                        pltpu.VMEM((tm, d), F32), pltpu.VMEM((CONV_WIDTH, 8, d), F32),
                        pltpu.VMEM((8, LANES), F32)],
        compiler_params=_cp("arbitrary", "arbitrary"),
        name="conv_module_router",
    )(u3, u3, u3, w1, b1, dw, dwb, lng, lnb, w2, b2, x2, mod1, post_g, ffn_pre_g, rw)


def _gather_kernel(idx_ref, src_ref, out_ref, sem, *, rb):
    def copy(r):
        return pltpu.make_async_copy(src_ref.at[pl.ds(idx_ref[0, r], 1), :], out_ref.at[pl.ds(r, 1), :], sem)

    def start(r, carry):
        copy(r).start()
        return carry

    def wait(r, carry):
        copy(r).wait()
        return carry

    lax.fori_loop(0, rb, start, 0)
    lax.fori_loop(0, rb, wait, 0)


def _gather_rows(src, idx, *, rb):
    n_rows = idx.shape[0]
    d = src.shape[1]
    nb = n_rows // rb
    return pl.pallas_call(
        functools.partial(_gather_kernel, rb=rb),
        grid=(nb,),
        in_specs=[pl.BlockSpec((None, 1, rb), lambda i: (i, 0, 0), memory_space=pltpu.SMEM),
                  pl.BlockSpec(memory_space=pl.ANY)],
        out_specs=pl.BlockSpec((rb, d), lambda i: (i, 0)),
        out_shape=jax.ShapeDtypeStruct((n_rows, d), src.dtype),
        scratch_shapes=[pltpu.SemaphoreType.DMA],
        compiler_params=_cp("arbitrary"),
        name="moe_gather",
    )(idx.reshape(nb, 1, rb), src)


def _moe_kernel(be_ref, x_ref, w1_ref, w3_ref, w2_ref, y_ref, acc_ref, *, nf):
    j = pl.program_id(1)

    @pl.when(j == 0)
    def _():
        acc_ref[...] = jnp.zeros(acc_ref.shape, F32)

    xb = x_ref[...].astype(BF16)
    h = _silu(_dot(xb, w1_ref[...])) * _dot(xb, w3_ref[...])
    acc_ref[...] += _dot(h.astype(BF16), w2_ref[...])

    @pl.when(j == nf - 1)
    def _():
        y_ref[...] = acc_ref[...]


def _moe_ffn(xs, block_e, w1, w3, w2, *, rows, tf):
    n_rows, d = xs.shape
    f = w1.shape[2]
    nf = f // tf
    grid_spec = pltpu.PrefetchScalarGridSpec(
        num_scalar_prefetch=1,
        grid=(n_rows // rows, nf),
        in_specs=[pl.BlockSpec((rows, d), lambda i, j, be: (i, 0)),
                  pl.BlockSpec((None, d, tf), lambda i, j, be: (be[i], 0, j)),
                  pl.BlockSpec((None, d, tf), lambda i, j, be: (be[i], 0, j)),
                  pl.BlockSpec((None, tf, d), lambda i, j, be: (be[i], j, 0))],
        out_specs=pl.BlockSpec((rows, d), lambda i, j, be: (i, 0)),
        scratch_shapes=[pltpu.VMEM((rows, d), F32)],
    )
    return pl.pallas_call(
        functools.partial(_moe_kernel, nf=nf),
        grid_spec=grid_spec,
        out_shape=jax.ShapeDtypeStruct((n_rows, d), F32),
        compiler_params=_cp("arbitrary", "arbitrary"),
        name="moe_ffn",
    )(block_e, xs, w1, w3, w2)


def _combine_kernel(s1_ref, s2_ref, ys_ref, route_ref, x_ref, mod_ref, pg_ref, out_ref, y1_ref, y2_ref, sem,
                    *, tm):
    def copies(r):
        return (pltpu.make_async_copy(ys_ref.at[pl.ds(s1_ref[0, r], 1), :], y1_ref.at[pl.ds(r, 1), :], sem.at[0]),
                pltpu.make_async_copy(ys_ref.at[pl.ds(s2_ref[0, r], 1), :], y2_ref.at[pl.ds(r, 1), :], sem.at[1]))

    def start(r, carry):
        a, b = copies(r)
        a.start()
        b.start()
        return carry

    def wait(r, carry):
        a, b = copies(r)
        a.wait()
        b.wait()
        return carry

    lax.fori_loop(0, tm, start, 0)
    lax.fori_loop(0, tm, wait, 0)
    route = route_ref[...]
    y = route[:, 2:3] * y1_ref[...] + route[:, 3:4] * y2_ref[...]
    out_ref[...] = x_ref[...] + mod_ref[5:6, :] * _rms(y, pg_ref[...])


def _combine(slot1, slot2, ys, route, x3, mod1, post_g, *, tm):
    b, t, d = x3.shape
    nt = t // tm
    idx = lambda: pl.BlockSpec((None, 1, tm), lambda bi, i: (bi * nt + i, 0, 0), memory_space=pltpu.SMEM)
    row = lambda w: pl.BlockSpec((None, tm, w), lambda bi, i: (bi, i, 0))
    return pl.pallas_call(
        functools.partial(_combine_kernel, tm=tm),
        grid=(b, nt),
        in_specs=[idx(), idx(), pl.BlockSpec(memory_space=pl.ANY), row(LANES), row(d),
                  pl.BlockSpec((None, 6, d), lambda bi, i: (bi, 0, 0)),
                  pl.BlockSpec((1, d), lambda bi, i: (0, 0))],
        out_specs=row(d),
        out_shape=jax.ShapeDtypeStruct((b, t, d), F32),
        scratch_shapes=[pltpu.VMEM((tm, d), F32), pltpu.VMEM((tm, d), F32), pltpu.SemaphoreType.DMA((2,))],
        compiler_params=_cp("arbitrary", "arbitrary"),
        name="moe_combine",
    )(slot1.reshape(b * nt, 1, tm), slot2.reshape(b * nt, 1, tm), ys, route, x3, mod1, post_g.reshape(1, d))


def _rope_tables(t_len):
    rows = t_len // GRID_W
    row = jnp.repeat(jnp.arange(rows, dtype=F32), GRID_W)
    col = jnp.tile(jnp.arange(GRID_W, dtype=F32), rows)
    axis_dim = DIFF_DQK // 2
    inv = ROPE_BASE ** (-jnp.arange(0, axis_dim, 2, dtype=F32) / axis_dim)
    ang_r = row[:, None] * inv
    ang_c = col[:, None] * inv
    cr, sr, cc, sc = jnp.cos(ang_r), jnp.sin(ang_r), jnp.cos(ang_c), jnp.sin(ang_c)
    cos = jnp.concatenate([cr, cr, cc, cc] * 2, axis=1)
    sin = jnp.concatenate([-sr, sr, -sc, sc] * 2, axis=1)
    return cos, sin


def kernel(x, c, ctx, c_ctx, l0_mod_w, l0_mod_b, l0_mix_pre_g, l0_mix_post_g, l0_w_in, l0_mlstm_gate_b, l0_mlstm_conv_w, l0_mlstm_norm_g, l0_lambda_q1, l0_lambda_k1, l0_lambda_q2, l0_lambda_k2, l0_diff_norm_g, l0_w_out, l0_ffn_pre_g, l0_ffn_post_g, l0_ffn_w1, l0_ffn_w3, l0_ffn_w2, l1_mod_w, l1_mod_b, l1_mix_pre_g, l1_mix_post_g, l1_conv_pw1_w, l1_conv_pw1_b, l1_conv_dw_w, l1_conv_dw_b, l1_conv_ln_g, l1_conv_ln_b, l1_conv_pw2_w, l1_conv_pw2_b, l1_ffn_pre_g, l1_ffn_post_g, l1_router_w, l1_moe_w1, l1_moe_w3, l1_moe_w2):
    b, t, d = x.shape
    n_ctx = ctx.shape[1]
    L = MLSTM_CHUNK
    assert d == D_MODEL and n_ctx == L and t % (2 * L) == 0 and b <= 8

    cpad = jnp.zeros((16, d), F32).at[:b].set(c).at[8].set(c_ctx)
    mod0 = _adaln(cpad, l0_mod_w, l0_mod_b).reshape(16, 6, d)
    mod1 = _adaln(cpad, l1_mod_w, l1_mod_b).reshape(16, 6, d)

    w_perm = jnp.concatenate(
        [l0_w_in[:, :2048], l0_w_in[:, 2080:], l0_w_in[:, 2048:2080], jnp.zeros((d, 96), F32)], axis=1).astype(BF16)
    gate_b = jnp.concatenate([l0_mlstm_gate_b, jnp.zeros((96,), F32)]).reshape(1, LANES)
    cos, sin = _rope_tables(t)
    g0 = l0_mix_pre_g.reshape(1, d)
    mq, mk, mv, mo, dq, dk, dv, gates = _inproj(x, mod0, None, g0, w_perm, l0_mlstm_conv_w, gate_b, cos, sin,
                                                 tm=512, rope=True)
    _, mkc, mvc, _, _, dkc, dvc, gates_c = _inproj(ctx, mod0, 8, g0, w_perm, l0_mlstm_conv_w, gate_b,
                                                   cos[:n_ctx], sin[:n_ctx], tm=n_ctx, rope=False)

    arow, cols = _gateprep(jnp.concatenate([gates_c, gates], axis=1), L=L)
    hm = _mlstm(mq, mk, mv, mo, mkc, mvc, arow, cols, l0_mlstm_norm_g, L=L)

    lam_init = 0.8 - 0.6 * math.exp(-0.3 * 0)
    lam_in = jnp.zeros((8, LANES), F32).at[0, :DIFF_DQK].set(l0_lambda_q1).at[1, :DIFF_DQK].set(l0_lambda_k1)
    lam_in = lam_in.at[2, :DIFF_DQK].set(l0_lambda_q2).at[3, :DIFF_DQK].set(l0_lambda_k2)
    hd = _attn(dq, jnp.concatenate([dkc, dk], axis=1), jnp.concatenate([dvc, dv], axis=1), lam_in,
               l0_diff_norm_g, tq=512, sub=256, max_sub_per_step=11, lam_init=lam_init)

    w_out = l0_w_out.astype(BF16)
    x1, u2 = _outproj(hm, hd, w_out[:512], w_out[512:], x, mod0, l0_mix_post_g, l0_ffn_pre_g, tm=512)
    x2, u3 = _ffn(u2, l0_ffn_w1.astype(BF16), l0_ffn_w3.astype(BF16), l0_ffn_w2.astype(BF16), x1, mod0, mod1,
                  l0_ffn_post_g, l1_mix_pre_g, tm=512, tf=1408)

    dw = jnp.concatenate([l1_conv_dw_w, jnp.zeros((1, d), F32)], axis=0)
    rw = jnp.concatenate([l1_router_w, jnp.zeros((d, LANES - N_EXPERTS), F32)], axis=1).astype(BF16)
    v1 = lambda a: a.reshape(1, -1)
    x3, u4, route, counts = _convmod(
        u3, l1_conv_pw1_w.astype(BF16), v1(l1_conv_pw1_b), dw, v1(l1_conv_dw_b), v1(l1_conv_ln_g),
        v1(l1_conv_ln_b), l1_conv_pw2_w.astype(BF16), v1(l1_conv_pw2_b), x2, mod1, v1(l1_mix_post_g),
        v1(l1_ffn_pre_g), rw, tm=256)

    n = b * t
    rows = MOE_ROWS
    rt = route.reshape(n, LANES)
    e1, e2 = rt[:, 0].astype(jnp.int32), rt[:, 1].astype(jnp.int32)
    r1, r2 = rt[:, 4].astype(jnp.int32), rt[:, 5].astype(jnp.int32)
    cnt = counts[0, :N_EXPERTS].astype(jnp.int32)
    padded = ((cnt + rows - 1) // rows) * rows
    pends = jnp.cumsum(padded)
    pstarts = pends - padded
    slot1 = pstarts[e1] + r1
    slot2 = pstarts[e2] + r2
    n_blk = (2 * n) // rows + N_EXPERTS
    tok = jnp.arange(n, dtype=jnp.int32)
    slot_tok = jnp.zeros((n_blk * rows,), jnp.int32).at[slot1].set(tok).at[slot2].set(tok)
    block_e = jnp.clip(jnp.searchsorted(pends, jnp.arange(n_blk, dtype=jnp.int32) * rows, side='right'),
                       0, N_EXPERTS - 1).astype(jnp.int32)

    xs = _gather_rows(u4.reshape(n, d), slot_tok, rb=256)
    ys = _moe_ffn(xs, block_e, l1_moe_w1.astype(BF16), l1_moe_w3.astype(BF16), l1_moe_w2.astype(BF16),
                  rows=rows, tf=1408)
    return _combine(slot1, slot2, ys, route, x3, mod1, l1_ffn_post_g, tm=256)
```

```python
import functools
import math

import jax
import jax.numpy as jnp
from jax import lax
from jax.experimental import pallas as pl
from jax.experimental.pallas import tpu as pltpu

F32 = jnp.float32
BF16 = jnp.bfloat16
EPS = 1e-6
NEG = -1e30
LOG2E = 1.4426950408889634

D_MODEL = 1024
GRID_W = 64
MLSTM_HEADS = 8
MLSTM_D = 64
MLSTM_CHUNK = 256
DIFF_HEADS = 4
DIFF_DQK = 64
ROPE_BASE = 10000.0
CONV_WIDTH = 31
N_EXPERTS = 8
MOE_ROWS = 512
MOE_TILE = 512
MOE_WIN = 512
MOE_ALIGN = 16
LANES = 128
VMEM_LIMIT = 52 * 1024 * 1024


def _cp(*sem):
    return pltpu.CompilerParams(dimension_semantics=sem, vmem_limit_bytes=VMEM_LIMIT)


def _rms(x, g):
    return x * lax.rsqrt(jnp.mean(x * x, axis=-1, keepdims=True) + EPS) * g


def _silu(x):
    return x * jax.nn.sigmoid(x)


def _dot(a, b):
    return jnp.dot(a, b, preferred_element_type=F32)


def _dot_nt(a, b):
    return lax.dot_general(a, b, (((1,), (1,)), ((), ())), preferred_element_type=F32)


def _dot_tn(a, b):
    return lax.dot_general(a, b, (((0,), (0,)), ((), ())), preferred_element_type=F32)


def _adaln_kernel(c_ref, w_ref, b_ref, o_ref):
    s = _silu(c_ref[...])
    o_ref[...] = _dot(s.astype(BF16), w_ref[...].astype(BF16)) + b_ref[...]


def _adaln(cpad, w, b):
    rows, d = cpad.shape
    n = w.shape[1]
    tn = 1536
    return pl.pallas_call(
        _adaln_kernel,
        grid=(n // tn,),
        in_specs=[pl.BlockSpec((rows, d), lambda j: (0, 0)),
                  pl.BlockSpec((d, tn), lambda j: (0, j)),
                  pl.BlockSpec((1, tn), lambda j: (0, j))],
        out_specs=pl.BlockSpec((rows, tn), lambda j: (0, j)),
        out_shape=jax.ShapeDtypeStruct((rows, n), F32),
        compiler_params=_cp("arbitrary"),
        name="adaln",
    )(cpad, w, b.reshape(1, n))


def _inproj_kernel(x_ref, xp_ref, xn_ref, mod_ref, g_ref, w_ref, cw_ref, gb_ref, cos_ref, sin_ref,
                   mq_ref, mk_ref, mv_ref, mo_ref, dq_ref, dk_ref, dv_ref, gt_ref, pext_ref,
                   *, tm, nt, rope):
    i = pl.program_id(1)
    g = g_ref[...]
    sh = mod_ref[0:1, :]
    sc = mod_ref[1:2, :]

    def mod(xv):
        return _rms(xv, g) * (1.0 + sc) + sh

    u = mod(x_ref[...])
    up = jnp.where(i > 0, mod(xp_ref[...]), 0.0)
    un = jnp.where(i < nt - 1, mod(xn_ref[...]), 0.0)
    ub = u.astype(BF16)
    uext = jnp.concatenate([up.astype(BF16), ub, un.astype(BF16)], axis=0)

    pext_ref[...] = _dot(uext, w_ref[:, 0:1024])
    cw = cw_ref[...]
    conv = (cw[0:1, :] * pext_ref[pl.ds(7, tm), :] + cw[1:2, :] * pext_ref[pl.ds(8, tm), :]
            + cw[2:3, :] * pext_ref[pl.ds(9, tm), :])
    act = _silu(conv)
    mq_ref[...] = (act[:, 0:512] * (MLSTM_D ** -0.5)).astype(BF16)
    mk_ref[...] = act[:, 512:1024].astype(BF16)

    p = _dot(ub, w_ref[:, 1024:2048])
    mv_ref[...] = p[:, 0:512].astype(BF16)
    mo_ref[...] = p[:, 512:1024].astype(BF16)

    p = _dot(ub, w_ref[:, 2048:3072])
    if rope:
        lane = lax.broadcasted_iota(jnp.int32, p.shape, 1)
        first_half = ((lane // 16) % 2) == 0
        nl = p.shape[1]
        partner = jnp.where(first_half, pltpu.roll(p, nl - 16, 1), pltpu.roll(p, 16, 1))
        cos = jnp.concatenate([cos_ref[...]] * 8, axis=1)
        sin = jnp.concatenate([sin_ref[...]] * 8, axis=1)
        p = p * cos + partner * sin
    dq_ref[...] = (p[:, 0:512] * (LOG2E * DIFF_DQK ** -0.5)).astype(BF16)
    dk_ref[...] = p[:, 512:1024].astype(BF16)

    p = _dot(ub, w_ref[:, 3072:3712])
    dv_ref[...] = p[:, 0:512].astype(BF16)
    gt_ref[...] = p[:, 512:640] + gb_ref[...]


def _inproj(x, mod, mod_row, g, w_perm, conv_w, gate_b, cos, sin, *, tm, rope):
    b, t, d = x.shape
    nt = t // tm
    hb = tm // 8
    nb8 = t // 8
    bf = lambda: jax.ShapeDtypeStruct((b, t, 512), BF16)
    if mod_row is None:
        mod_map = lambda bi, i: (bi, 0, 0)
    else:
        mod_map = lambda bi, i: (mod_row, 0, 0)
    kern = functools.partial(_inproj_kernel, tm=tm, nt=nt, rope=rope)
    o512 = pl.BlockSpec((None, tm, 512), lambda bi, i: (bi, i, 0))
    return pl.pallas_call(
        kern,
        grid=(b, nt),
        in_specs=[
            pl.BlockSpec((None, tm, d), lambda bi, i: (bi, i, 0)),
            pl.BlockSpec((None, 8, d), lambda bi, i: (bi, jnp.maximum(i * hb - 1, 0), 0)),
            pl.BlockSpec((None, 8, d), lambda bi, i: (bi, jnp.minimum((i + 1) * hb, nb8 - 1), 0)),
            pl.BlockSpec((None, 6, d), mod_map),
            pl.BlockSpec((1, d), lambda bi, i: (0, 0)),
            pl.BlockSpec((d, 3712), lambda bi, i: (0, 0)),
            pl.BlockSpec((3, d), lambda bi, i: (0, 0)),
            pl.BlockSpec((1, LANES), lambda bi, i: (0, 0)),
            pl.BlockSpec((tm, LANES), lambda bi, i: (i, 0)),
            pl.BlockSpec((tm, LANES), lambda bi, i: (i, 0)),
        ],
        out_specs=[o512] * 7 + [pl.BlockSpec((None, tm, LANES), lambda bi, i: (bi, i, 0))],
        out_shape=[bf() for _ in range(7)] + [jax.ShapeDtypeStruct((b, t, LANES), F32)],
        scratch_shapes=[pltpu.VMEM((tm + 16, 1024), F32)],
        compiler_params=_cp("arbitrary", "arbitrary"),
        name="inproj_rope" if rope else "inproj_ctx",
    )(x, x, x, mod, g, w_perm, conv_w, gate_b, cos, sin)


def _gateprep_kernel(g_ref, arow_ref, col_ref, *, L, nch):
    tt = L * nch
    gt = g_ref[...].T
    i_f, f_f, i_b, f_b = gt[0:8], gt[8:16], gt[16:24], gt[24:32]

    def logsig(v):
        return jnp.minimum(v, 0.0) - jnp.log1p(jnp.exp(-jnp.abs(v)))

    pos = lax.broadcasted_iota(jnp.int32, (8, tt), 1) % L

    def scan(v, op, ident, reverse):
        s = 1
        while s < L:
            if reverse:
                shifted = pltpu.roll(v, tt - s, 1)
                valid = pos < L - s
            else:
                shifted = pltpu.roll(v, s, 1)
                valid = pos >= s
            v = op(v, jnp.where(valid, shifted, ident))
            s *= 2
        return v

    outs = []
    for d, (ig, fg) in enumerate(((i_f, f_f), (i_b, f_b))):
        rev = d == 1
        bcum = scan(logsig(fg), jnp.add, 0.0, rev)
        a = ig - bcum
        cm = scan(a, jnp.maximum, NEG, rev)
        order = list(range(nch)) if not rev else [0] + list(range(nch - 1, 0, -1))
        mp = jnp.zeros((8, 1), F32)
        mp_c = [None] * nch
        ml_c = [None] * nch
        for j in order:
            e = j * L if rev else j * L + L - 1
            mlast = jnp.maximum(mp, cm[:, e:e + 1])
            mp_c[j] = jnp.broadcast_to(mp, (8, L))
            ml_c[j] = jnp.broadcast_to(mlast, (8, L))
            mp = bcum[:, e:e + 1] + mlast
        mprev = jnp.concatenate(mp_c, axis=1)
        mlast = jnp.concatenate(ml_c, axis=1)
        m = jnp.maximum(mprev, cm)
        outs.append((a, m, jnp.exp(mprev - m), jnp.exp(a - mlast), jnp.exp(-(bcum + m))))

    arow_ref[...] = jnp.concatenate([outs[0][0], outs[1][0]], axis=0)
    rows = []
    for pair in range(4):
        for q in range(1, 5):
            for d in range(2):
                rows.append(outs[d][q][2 * pair:2 * pair + 2])
    rows.append(jnp.zeros((64, tt), F32))
    col_ref[...] = jnp.concatenate(rows, axis=0).T


def _gateprep(gall, *, L):
    b, tt, _ = gall.shape
    nch = tt // L
    return pl.pallas_call(
        functools.partial(_gateprep_kernel, L=L, nch=nch),
        grid=(b,),
        in_specs=[pl.BlockSpec((None, tt, LANES), lambda bi: (bi, 0, 0))],
        out_specs=[pl.BlockSpec((None, 16, tt), lambda bi: (bi, 0, 0)),
                   pl.BlockSpec((None, tt, LANES), lambda bi: (bi, 0, 0))],
        out_shape=[jax.ShapeDtypeStruct((b, 16, tt), F32), jax.ShapeDtypeStruct((b, tt, LANES), F32)],
        compiler_params=_cp("arbitrary"),
        name="mlstm_gateprep",
    )(gall)


def _mlstm_kernel(q_ref, k_ref, v_ref, o_ref, kc_ref, vc_ref, arow_ref, col_ref, ng_ref, out_ref,
                  c_ref, hf_ref, hb_ref, *, L, nc):
    p = pl.program_id(1)
    half = nc // 2
    lane = lax.broadcasted_iota(jnp.int32, (L, LANES), 1)
    lo = lane < MLSTM_D
    head_mask = (lo, jnp.logical_not(lo))
    ri = lax.broadcasted_iota(jnp.int32, (L, L), 0)
    ci = lax.broadcasted_iota(jnp.int32, (L, L), 1)
    causal = (ci <= ri, ci >= ri)
    ones_t = jnp.ones((L, LANES), BF16)
    shift = lax.rem(LANES - 16 * p, LANES)

    def cols(off):
        return pltpu.roll(col_ref[pl.ds(off, L), :], shift, 1)

    def col(blk, q, d, hh):
        j = q * 4 + d * 2 + hh
        return blk[:, j:j + 1]

    def vext_of(vb, hh):
        return jnp.concatenate([jnp.where(head_mask[hh], vb, jnp.zeros_like(vb)), ones_t], axis=1)

    def state_update(d, hh, kb, vext, blk, dec):
        khm = jnp.where(head_mask[hh], kb, jnp.zeros_like(kb))
        wv = (col(blk, 2, d, hh) * vext.astype(F32)).astype(BF16)
        upd = _dot_tn(khm, wv)
        if dec is None:
            c_ref[d, hh] = upd
        else:
            c_ref[d, hh] = dec * c_ref[d, hh] + upd

    blk0 = cols(0)
    kcb = kc_ref[...]
    vcb = vc_ref[...]
    for d in range(2):
        for hh in range(2):
            state_update(d, hh, kcb, vext_of(vcb, hh), blk0, None)

    def compute(d, c):
        t0 = pl.multiple_of(c * L, L)
        off = pl.multiple_of(c * L + L, L)
        qb = q_ref[pl.ds(t0, L), :]
        kb = k_ref[pl.ds(t0, L), :]
        vb = v_ref[pl.ds(t0, L), :]
        blk = cols(off)
        last = L - 1 if d == 0 else 0
        hs = []
        for hh in range(2):
            arow = arow_ref[pl.ds(d * 8 + 2 * p + hh, 1), pl.ds(off, L)]
            khm = jnp.where(head_mask[hh], kb, jnp.zeros_like(kb))
            s = _dot_nt(qb, khm)
            arg = jnp.where(causal[d], arow - col(blk, 0, d, hh), NEG)
            pm = (s * jnp.exp(arg)).astype(BF16)
            vext = vext_of(vb, hh)
            ch = c_ref[d, hh]
            tot = _dot(pm, vext) + col(blk, 1, d, hh) * _dot(qb, ch.astype(BF16))
            den = jnp.maximum(jnp.abs(tot[:, LANES:]), col(blk, 3, d, hh))
            hs.append(tot[:, :LANES] / den)
            dec = col(blk, 1, d, hh)[last:last + 1, :]
            state_update(d, hh, kb, vext, blk, dec)
        return jnp.where(lo, hs[0], hs[1])

    def finalize(c, hsum):
        t0 = pl.multiple_of(c * L, L)
        sq = hsum * hsum
        s0 = jnp.sum(jnp.where(lo, sq, 0.0), axis=-1, keepdims=True)
        s1 = jnp.sum(jnp.where(lo, 0.0, sq), axis=-1, keepdims=True)
        ms = jnp.where(lo, s0, s1) * (1.0 / MLSTM_D)
        y = hsum * lax.rsqrt(ms + EPS) * ng_ref[...]
        gate = jax.nn.sigmoid(o_ref[pl.ds(t0, L), :].astype(F32))
        out_ref[pl.ds(t0, L), :] = (y * gate).astype(BF16)

    def phase_a(i, carry):
        hf_ref[pl.ds(pl.multiple_of(i * L, L), L), :] = compute(0, i)
        cb = nc - 1 - i
        hb_ref[pl.ds(pl.multiple_of((cb - half) * L, L), L), :] = compute(1, cb)
        return carry

    def phase_b(i, carry):
        hf = compute(0, i)
        finalize(i, hf + hb_ref[pl.ds(pl.multiple_of((i - half) * L, L), L), :])
        cb = nc - 1 - i
        hb = compute(1, cb)
        finalize(cb, hb + hf_ref[pl.ds(pl.multiple_of(cb * L, L), L), :])
        return carry

    lax.fori_loop(0, half, phase_a, 0)
    lax.fori_loop(half, nc, phase_b, 0)


def _mlstm(mq, mk, mv, mo, mkc, mvc, arow, cols, norm_g, *, L):
    b, t, _ = mq.shape
    ctx = mkc.shape[1]
    tt = arow.shape[2]
    nc = t // L
    tok = pl.BlockSpec((None, t, LANES), lambda bi, p: (bi, 0, p))
    ctxs = pl.BlockSpec((None, ctx, LANES), lambda bi, p: (bi, 0, p))
    return pl.pallas_call(
        functools.partial(_mlstm_kernel, L=L, nc=nc),
        grid=(b, 4),
        in_specs=[tok, tok, tok, tok, ctxs, ctxs,
                  pl.BlockSpec((None, 16, tt), lambda bi, p: (bi, 0, 0)),
                  pl.BlockSpec((None, tt, LANES), lambda bi, p: (bi, 0, 0)),
                  pl.BlockSpec((1, LANES), lambda bi, p: (0, p))],
        out_specs=tok,
        out_shape=jax.ShapeDtypeStruct((b, t, 512), BF16),
        scratch_shapes=[pltpu.VMEM((2, 2, LANES, 2 * LANES), F32),
                        pltpu.VMEM((t // 2, LANES), F32),
                        pltpu.VMEM((t // 2, LANES), F32)],
        compiler_params=_cp("arbitrary", "arbitrary"),
        name="mlstm_scan",
    )(mq, mk, mv, mo, mkc, mvc, arow, cols, norm_g.reshape(1, 512))


def _attn_kernel(q_ref, k_ref, v_ref, lam_ref, ng_ref, out_ref, kmax_ref, m_ref, acc_ref,
                 *, tq, kb, nkb, sub, lam_init):
    i = pl.program_id(2)
    lane = lax.broadcasted_iota(jnp.int32, (tq, LANES), 1)
    lo = lane < DIFF_DQK
    rr = lax.broadcasted_iota(jnp.int32, (LANES, LANES), 0)
    cc = lax.broadcasted_iota(jnp.int32, (LANES, LANES), 1)
    same_comp = ((rr < DIFF_DQK) == (cc < DIFF_DQK)).astype(BF16)

    def comp_sqnorm(a):
        af = a.astype(F32)
        return _dot((af * af).astype(BF16), same_comp)

    @pl.when(i == 0)
    def _():
        mx = jnp.zeros((1, LANES), F32)

        def kbody(j, mx):
            k0 = pl.multiple_of(j * kb, kb)
            return jnp.maximum(mx, jnp.max(comp_sqnorm(k_ref[pl.ds(k0, kb), :]), axis=0, keepdims=True))

        kmax_ref[...] = lax.fori_loop(0, nkb, kbody, mx)

    q = q_ref[...]
    zq = jnp.zeros_like(q)
    bnd = jnp.sqrt(comp_sqnorm(q) * kmax_ref[...]) * 1.02
    b1 = jnp.where(lane == 0, -bnd, 0.0)
    b2 = jnp.where(lane == 0, -pltpu.roll(bnd, DIFF_DQK, 1), 0.0)
    qs = jnp.concatenate(
        [jnp.concatenate([jnp.where(lo, q, zq), jnp.where(lo, zq, q)], axis=0),
         jnp.concatenate([b1, b2], axis=0).astype(BF16)], axis=1)
    acc_ref[...] = jnp.zeros(acc_ref.shape, F32)

    def ext(blk):
        return jnp.concatenate([blk, jnp.ones_like(blk)], axis=1)

    def run(step):
        def body(j, carry):
            k0 = pl.multiple_of(j * kb, kb)
            step(k_ref[pl.ds(k0, kb), :], v_ref[pl.ds(k0, kb), :])
            return carry

        lax.fori_loop(0, nkb, body, 0)

    def fast_step(kblk, vblk):
        tot = None
        for c in range(0, kblk.shape[0], sub):
            pm = jnp.exp2(_dot_nt(qs, ext(kblk[c:c + sub]))).astype(BF16)
            part = _dot(pm, ext(vblk[c:c + sub]))
            tot = part if tot is None else tot + part
        acc_ref[...] += tot

    def slow_step(kblk, vblk):
        s = _dot_nt(qs, ext(kblk))
        m_old = m_ref[...]
        m_new = jnp.maximum(m_old, jnp.max(s, axis=-1, keepdims=True))
        pm = jnp.exp2(s - m_new).astype(BF16)
        acc_ref[...] = jnp.exp2(m_old - m_new) * acc_ref[...] + _dot(pm, ext(vblk))
        m_ref[...] = m_new

    fast = jnp.max(bnd) <= 56.0

    @pl.when(fast)
    def _():
        run(fast_step)

    @pl.when(jnp.logical_not(fast))
    def _():
        m_ref[...] = jnp.full(m_ref.shape, NEG, F32)
        run(slow_step)

    lq = lam_ref[...]
    lam = (jnp.exp(jnp.sum(lq[0:1, :] * lq[1:2, :], axis=-1, keepdims=True))
           - jnp.exp(jnp.sum(lq[2:3, :] * lq[3:4, :], axis=-1, keepdims=True)) + lam_init)
    a1 = acc_ref[0:tq, :]
    a2 = acc_ref[tq:2 * tq, :]
    o = a1[:, :LANES] / a1[:, LANES:] - lam * (a2[:, :LANES] / a2[:, LANES:])
    out_ref[...] = (_rms(o, ng_ref[...]) * (1.0 - lam_init)).astype(BF16)


def _attn(dq, dk, dv, lam_in, norm_g, *, tq, sub, max_sub_per_step, lam_init):
    b, t, _ = dq.shape
    tk = dk.shape[1]
    nsub = tk // sub
    per = max(g for g in range(1, max_sub_per_step + 1) if nsub % g == 0)
    kb = per * sub
    full = pl.BlockSpec((None, tk, LANES), lambda bi, h, i: (bi, 0, h))
    qs = pl.BlockSpec((None, tq, LANES), lambda bi, h, i: (bi, i, h))
    return pl.pallas_call(
        functools.partial(_attn_kernel, tq=tq, kb=kb, nkb=tk // kb, sub=sub, lam_init=lam_init),
        grid=(b, DIFF_HEADS, t // tq),
        in_specs=[qs, full, full,
                  pl.BlockSpec((8, LANES), lambda bi, h, i: (0, 0)),
                  pl.BlockSpec((1, LANES), lambda bi, h, i: (0, h))],
        out_specs=qs,
        out_shape=jax.ShapeDtypeStruct((b, t, 512), BF16),
        scratch_shapes=[pltpu.VMEM((1, LANES), F32), pltpu.VMEM((2 * tq, 1), F32),
                        pltpu.VMEM((2 * tq, 2 * LANES), F32)],
        compiler_params=_cp("arbitrary", "arbitrary", "arbitrary"),
        name="diff_attn",
    )(dq, dk, dv, lam_in, norm_g.reshape(1, 512))


def _outproj_kernel(hm_ref, hd_ref, wt_ref, wb_ref, x_ref, mod_ref, pg_ref, fg_ref, x1_ref, u_ref):
    mix = _dot(hm_ref[...], wt_ref[...]) + _dot(hd_ref[...], wb_ref[...])
    x1 = x_ref[...] + mod_ref[2:3, :] * _rms(mix, pg_ref[...])
    x1_ref[...] = x1
    u_ref[...] = (_rms(x1, fg_ref[...]) * (1.0 + mod_ref[4:5, :]) + mod_ref[3:4, :]).astype(BF16)


def _outproj(hm, hd, wt, wb, x, mod, post_g, ffn_pre_g, *, tm):
    b, t, d = x.shape
    row = lambda w: pl.BlockSpec((None, tm, w), lambda bi, i: (bi, i, 0))
    cst = lambda s: pl.BlockSpec(s, lambda bi, i: (0,) * len(s))
    return pl.pallas_call(
        _outproj_kernel,
        grid=(b, t // tm),
        in_specs=[row(512), row(512), cst((512, d)), cst((512, d)), row(d),
                  pl.BlockSpec((None, 6, d), lambda bi, i: (bi, 0, 0)), cst((1, d)), cst((1, d))],
        out_specs=[row(d), row(d)],
        out_shape=[jax.ShapeDtypeStruct((b, t, d), F32), jax.ShapeDtypeStruct((b, t, d), BF16)],
        compiler_params=_cp("arbitrary", "arbitrary"),
        name="outproj",
    )(hm, hd, wt, wb, x, mod, post_g.reshape(1, d), ffn_pre_g.reshape(1, d))


def _swiglu(u, w1_ref, w3_ref, w2_ref):
    return _dot((_silu(_dot(u, w1_ref[...])) * _dot(u, w3_ref[...])).astype(BF16), w2_ref[...])


def _ffn_kernel(u_ref, w1_ref, w3_ref, w2_ref, x_ref, mod0_ref, mod1_ref, pg_ref, ng_ref, x2_ref, u3_ref):
    y = _swiglu(u_ref[...], w1_ref, w3_ref, w2_ref)
    x2 = x_ref[...] + mod0_ref[5:6, :] * _rms(y, pg_ref[...])
    x2_ref[...] = x2
    u3_ref[...] = (_rms(x2, ng_ref[...]) * (1.0 + mod1_ref[1:2, :]) + mod1_ref[0:1, :]).astype(BF16)


def _ffn(u, w1, w3, w2, x1, mod0, mod1, post_g, next_pre_g, *, tm):
    b, t, d = x1.shape
    f = w1.shape[1]
    row = pl.BlockSpec((None, tm, d), lambda bi, i: (bi, i, 0))
    modb = pl.BlockSpec((None, 6, d), lambda bi, i: (bi, 0, 0))
    vec = pl.BlockSpec((1, d), lambda bi, i: (0, 0))
    wspec = lambda s: pl.BlockSpec(s, lambda bi, i: (0, 0), pipeline_mode=pl.Buffered(1))
    return pl.pallas_call(
        _ffn_kernel,
        grid=(b, t // tm),
        in_specs=[row, wspec((d, f)), wspec((d, f)), wspec((f, d)), row, modb, modb, vec, vec],
        out_specs=[row, row],
        out_shape=[jax.ShapeDtypeStruct((b, t, d), F32), jax.ShapeDtypeStruct((b, t, d), BF16)],
        compiler_params=_cp("arbitrary", "arbitrary"),
        name="ffn_swiglu",
    )(u, w1, w3, w2, x1, mod0, mod1, post_g.reshape(1, d), next_pre_g.reshape(1, d))


HALO = 16
CONV_ROW_BLOCK = 32
CONV_ROWS_EXTRA = 24


def _conv_kernel(u_ref, up_ref, un_ref, w1_ref, b1_ref, dw_ref, dwb_ref, lng_ref, lnb_ref, w2_ref, b2_ref,
                 x_ref, mod_ref, pg_ref, fg_ref, rw_ref, x3_ref, u4_ref, route_ref, routet_ref, cnt_ref,
                 hs_ref, sh_ref, cv_ref, wb_ref, *, tm, nt):
    i = pl.program_id(1)

    uext = jnp.concatenate([up_ref[...], u_ref[...], un_ref[...]], axis=0)
    ag = _dot(uext, w1_ref[...]) + b1_ref[...]
    h = ag[:, :D_MODEL] * jax.nn.sigmoid(ag[:, D_MODEL:])
    row = lax.broadcasted_iota(jnp.int32, (tm + 2 * HALO, 1), 0)
    keep = ((row >= HALO) | (i > 0)) & ((row < tm + HALO) | (i < nt - 1))
    hs_ref[...] = jnp.where(keep, h, 0.0)

    for r in range(1, 8):
        sh_ref[r - 1] = hs_ref[pl.ds(r, tm + CONV_ROWS_EXTRA), :]

    for j in range(CONV_WIDTH):
        wb_ref[j] = jnp.broadcast_to(dw_ref[j:j + 1, :], (8, D_MODEL))
    nsub = CONV_ROW_BLOCK // 8

    def conv_rows(rb, carry):
        r0 = pl.multiple_of(rb * CONV_ROW_BLOCK, CONV_ROW_BLOCK)
        acc = jnp.broadcast_to(dwb_ref[...].reshape(1, 1, D_MODEL), (nsub, 8, D_MODEL))
        for j in range(CONV_WIDTH):
            r, a = (j + 1) % 8, (j + 1) // 8
            if r == 0:
                win = hs_ref[pl.ds(r0 + 8 * a, CONV_ROW_BLOCK), :]
            else:
                win = sh_ref[r - 1, pl.ds(r0 + 8 * a, CONV_ROW_BLOCK), :]
            acc = acc + wb_ref[j][None] * win.reshape(nsub, 8, D_MODEL)
        cv_ref[pl.ds(r0, CONV_ROW_BLOCK), :] = acc.reshape(CONV_ROW_BLOCK, D_MODEL)
        return carry

    lax.fori_loop(0, tm // CONV_ROW_BLOCK, conv_rows, 0)
    acc = cv_ref[...]
    mu = jnp.mean(acc, axis=-1, keepdims=True)
    cen = acc - mu
    var = jnp.mean(cen * cen, axis=-1, keepdims=True)
    hn = _silu(cen * lax.rsqrt(var + EPS) * lng_ref[...] + lnb_ref[...])
    y = _dot(hn.astype(BF16), w2_ref[...]) + b2_ref[...]
    x3 = x_ref[...] + mod_ref[2:3, :] * _rms(y, pg_ref[...])
    x3_ref[...] = x3
    u4 = _rms(x3, fg_ref[...]) * (1.0 + mod_ref[4:5, :]) + mod_ref[3:4, :]
    u4b = u4.astype(BF16)
    u4_ref[...] = u4b

    lane = lax.broadcasted_iota(jnp.int32, (tm, LANES), 1).astype(F32)
    logits = jnp.where(lane < N_EXPERTS, _dot(u4b, rw_ref[...]), NEG)
    m1 = jnp.max(logits, axis=-1, keepdims=True)
    i1 = jnp.min(jnp.where(logits == m1, lane, float(LANES)), axis=-1, keepdims=True)
    l2 = jnp.where(lane == i1, NEG, logits)
    m2 = jnp.max(l2, axis=-1, keepdims=True)
    i2 = jnp.min(jnp.where(l2 == m2, lane, float(LANES)), axis=-1, keepdims=True)
    e21 = jnp.exp(m2 - m1)
    g1 = 1.0 / (1.0 + e21)
    g2 = e21 * g1
    sel = ((lane == i1) | (lane == i2)).astype(F32)
    ri = lax.broadcasted_iota(jnp.int32, (tm, tm), 0)
    ci = lax.broadcasted_iota(jnp.int32, (tm, tm), 1)
    tri = (ci <= ri).astype(BF16)
    csum = _dot(tri, sel.astype(BF16))
    rank = csum - sel
    r1 = jnp.sum(jnp.where(lane == i1, rank, 0.0), axis=-1, keepdims=True)
    r2 = jnp.sum(jnp.where(lane == i2, rank, 0.0), axis=-1, keepdims=True)
    cnt_ref[...] = jnp.broadcast_to(csum[tm - 1:tm, :], (8, LANES))
    route = jnp.zeros((tm, LANES), F32)
    for n, v in enumerate((i1, i2, g1, g2, r1, r2)):
        route = jnp.where(lane == float(n), v, route)
    route_ref[...] = route
    routet_ref[...] = route.T[0:8, :]


def _convmod(u3, w1, b1, dw, dwb, lng, lnb, w2, b2, x2, mod1, post_g, ffn_pre_g, rw, *, tm):
    b, t, d = x2.shape
    nt = t // tm
    hb = tm // HALO
    nbh = t // HALO
    row = lambda w: pl.BlockSpec((None, tm, w), lambda bi, i: (bi, i, 0))
    cst = lambda s: pl.BlockSpec(s, lambda bi, i: (0,) * len(s))
    return pl.pallas_call(
        functools.partial(_conv_kernel, tm=tm, nt=nt),
        grid=(b, nt),
        in_specs=[row(d),
                  pl.BlockSpec((None, HALO, d), lambda bi, i: (bi, jnp.maximum(i * hb - 1, 0), 0)),
                  pl.BlockSpec((None, HALO, d), lambda bi, i: (bi, jnp.minimum((i + 1) * hb, nbh - 1), 0)),
                  cst((d, 2 * d)), cst((1, 2 * d)), cst((32, d)), cst((1, d)), cst((1, d)), cst((1, d)),
                  cst((d, d)), cst((1, d)), row(d),
                  pl.BlockSpec((None, 6, d), lambda bi, i: (bi, 0, 0)), cst((1, d)), cst((1, d)),
                  cst((d, LANES))],
        out_specs=[row(d), row(d), row(LANES),
                   pl.BlockSpec((None, 8, tm), lambda bi, i: (bi * nt + i, 0, 0)),
                   pl.BlockSpec((None, 8, LANES), lambda bi, i: (bi * nt + i, 0, 0))],
        out_shape=[jax.ShapeDtypeStruct((b, t, d), F32), jax.ShapeDtypeStruct((b, t, d), BF16),
                   jax.ShapeDtypeStruct((b, t, LANES), F32), jax.ShapeDtypeStruct((b * nt, 8, tm), F32),
                   jax.ShapeDtypeStruct((b * nt, 8, LANES), F32)],
        scratch_shapes=[pltpu.VMEM((tm + 2 * HALO, d), F32), pltpu.VMEM((7, tm + CONV_ROWS_EXTRA, d), F32),
                        pltpu.VMEM((tm, d), F32), pltpu.VMEM((CONV_WIDTH, 8, d), F32)],
        compiler_params=_cp("arbitrary", "arbitrary"),
        name="conv_module_router",
    )(u3, u3, u3, w1, b1, dw, dwb, lng, lnb, w2, b2, x2, mod1, post_g, ffn_pre_g, rw)


def _window_copies(ref_hbm, buf_ref, sem, s0_ref, t, slot, to_hbm):
    out = []
    for e in range(N_EXPERTS):
        hbm = ref_hbm.at[pl.ds(pl.multiple_of(s0_ref[t * N_EXPERTS + e], MOE_ALIGN), MOE_WIN), :]
        vmem = buf_ref.at[slot, e]
        src, dst = (vmem, hbm) if to_hbm else (hbm, vmem)
        out.append(pltpu.make_async_copy(src, dst, sem.at[slot, e]))
    return out


def _dispatch_kernel(s0_ref, hoff_ref, cnt_ref, zoff_ref, nu_ref, u_ref, rt_ref, xs_ref, win_ref, zero_ref,
                     sem, zsem, *, nt, n_blk):
    t = pl.program_id(0)
    slot = lax.rem(t, 2)
    half = MOE_TILE // 2

    @pl.when(t == 0)
    def _():
        zero_ref[...] = jnp.zeros(zero_ref.shape, BF16)
        zc = [pltpu.make_async_copy(
            zero_ref, xs_ref.at[pl.ds(pl.multiple_of(zoff_ref[e], MOE_ALIGN), MOE_WIN), :], zsem.at[e])
            for e in range(N_EXPERTS)]
        for c in zc:
            c.start()
        for c in zc:
            c.wait()

        def clear_block(bk, carry):
            c = pltpu.make_async_copy(
                zero_ref, xs_ref.at[pl.ds(pl.multiple_of(bk * MOE_ROWS, MOE_ROWS), MOE_ROWS), :], zsem.at[0])
            c.start()
            c.wait()
            return carry

        lax.fori_loop(nu_ref[0], n_blk, clear_block, 0)

    rt = jnp.concatenate([rt_ref[0], rt_ref[1]], axis=1)
    e1, e2, r1, r2 = rt[0:1], rt[1:2], rt[4:5], rt[5:6]
    second = lax.broadcasted_iota(jnp.int32, (1, MOE_TILE), 1) >= half
    srow = lax.broadcasted_iota(jnp.int32, (half, MOE_TILE), 0).astype(F32)
    u = u_ref[...]
    for e in range(N_EXPERTS):
        off = jnp.where(second, hoff_ref[t * N_EXPERTS + e].astype(F32), 0.0)
        lr = jnp.where(e1 == float(e), r1 + off, jnp.where(e2 == float(e), r2 + off, -1.0))
        win_ref[slot, e, 0:half, :] = _dot((srow == lr).astype(BF16), u).astype(BF16)
        big = cnt_ref[t * N_EXPERTS + e] > half

        @pl.when(big)
        def _():
            win_ref[slot, e, half:MOE_WIN, :] = _dot((srow + float(half) == lr).astype(BF16), u).astype(BF16)

        @pl.when(jnp.logical_not(big))
        def _():
            win_ref[slot, e, half:MOE_WIN, :] = jnp.zeros((half, D_MODEL), BF16)

    @pl.when(t > 0)
    def _():
        for c in _window_copies(xs_ref, win_ref, sem, s0_ref, t - 1, 1 - slot, True):
            c.wait()

    for c in _window_copies(xs_ref, win_ref, sem, s0_ref, t, slot, True):
        c.start()

    @pl.when(t == nt - 1)
    def _():
        for c in _window_copies(xs_ref, win_ref, sem, s0_ref, t, slot, True):
            c.wait()


def _dispatch(u4, routet, s0, hoff, cnt, zoff, n_used, *, n_rows):
    assert MOE_WIN == MOE_ROWS
    n, d = u4.shape
    nt = n // MOE_TILE
    per = MOE_TILE // routet.shape[2]
    grid_spec = pltpu.PrefetchScalarGridSpec(
        num_scalar_prefetch=5,
        grid=(nt,),
        in_specs=[pl.BlockSpec((MOE_TILE, d), lambda t, *_: (t, 0)),
                  pl.BlockSpec((per, 8, routet.shape[2]), lambda t, *_: (t, 0, 0))],
        out_specs=pl.BlockSpec(memory_space=pl.ANY),
        scratch_shapes=[pltpu.VMEM((2, N_EXPERTS, MOE_WIN, d), BF16), pltpu.VMEM((MOE_WIN, d), BF16),
                        pltpu.SemaphoreType.DMA((2, N_EXPERTS)), pltpu.SemaphoreType.DMA((N_EXPERTS,))],
    )
    return pl.pallas_call(
        functools.partial(_dispatch_kernel, nt=nt, n_blk=n_rows // MOE_ROWS),
        grid_spec=grid_spec,
        out_shape=jax.ShapeDtypeStruct((n_rows, d), BF16),
        compiler_params=_cp("arbitrary"),
        name="moe_dispatch",
    )(s0, hoff, cnt, zoff, n_used, u4, routet)


def _moe_kernel(be_ref, nu_ref, x_ref, w1_ref, w3_ref, w2_ref, y_ref):
    used = pl.program_id(0) < nu_ref[0]

    @pl.when(used)
    def _():
        y_ref[...] = _swiglu(x_ref[...], w1_ref, w3_ref, w2_ref).astype(BF16)

    @pl.when(jnp.logical_not(used))
    def _():
        y_ref[...] = jnp.zeros(y_ref.shape, BF16)


def _moe_ffn(xs, block_e, n_used, w1, w3, w2, *, rows):
    n_rows, d = xs.shape
    f = w1.shape[2]
    blk = lambda i, nu: jnp.minimum(i, nu[0] - 1)
    wspec = lambda s: pl.BlockSpec((None,) + s, lambda i, be, nu: (be[blk(i, nu)], 0, 0),
                                   pipeline_mode=pl.Buffered(1))
    grid_spec = pltpu.PrefetchScalarGridSpec(
        num_scalar_prefetch=2,
        grid=(n_rows // rows,),
        in_specs=[pl.BlockSpec((rows, d), lambda i, be, nu: (blk(i, nu), 0)),
                  wspec((d, f)), wspec((d, f)), wspec((f, d))],
        out_specs=pl.BlockSpec((rows, d), lambda i, be, nu: (i, 0)),
    )
    return pl.pallas_call(
        _moe_kernel,
        grid_spec=grid_spec,
        out_shape=jax.ShapeDtypeStruct((n_rows, d), BF16),
        compiler_params=_cp("arbitrary"),
        name="moe_ffn",
    )(block_e, n_used, xs, w1, w3, w2)


def _combine_kernel(s0_ref, hoff_ref, cnt_ref, ys_ref, route_ref, x_ref, mod_ref, pg_ref, out_ref,
                    buf_ref, acc_ref, sem, *, nti, nt):
    t = pl.program_id(0) * nti + pl.program_id(1)
    slot = lax.rem(t, 2)
    half = MOE_TILE // 2

    @pl.when(t == 0)
    def _():
        for c in _window_copies(ys_ref, buf_ref, sem, s0_ref, 0, 0, False):
            c.start()

    @pl.when(t + 1 < nt)
    def _():
        for c in _window_copies(ys_ref, buf_ref, sem, s0_ref, t + 1, 1 - slot, False):
            c.start()

    for c in _window_copies(ys_ref, buf_ref, sem, s0_ref, t, slot, False):
        c.wait()

    route = route_ref[...]
    e1, e2, g1, g2, r1, r2 = (route[:, n:n + 1] for n in range(6))
    second = lax.broadcasted_iota(jnp.int32, (MOE_TILE, 1), 0) >= half
    scol = lax.broadcasted_iota(jnp.int32, (MOE_TILE, half), 1).astype(F32)
    for e in range(N_EXPERTS):
        off = jnp.where(second, hoff_ref[t * N_EXPERTS + e].astype(F32), 0.0)
        is1 = e1 == float(e)
        is2 = e2 == float(e)
        lr = jnp.where(is1, r1 + off, jnp.where(is2, r2 + off, -1.0))
        ge = jnp.where(is1, g1, jnp.where(is2, g2, 0.0))
        z = ge * _dot((scol == lr).astype(BF16), buf_ref[slot, e, 0:half, :])
        if e == 0:
            acc_ref[...] = z
        else:
            acc_ref[...] += z

        @pl.when(cnt_ref[t * N_EXPERTS + e] > half)
        def _():
            acc_ref[...] += ge * _dot((scol + float(half) == lr).astype(BF16), buf_ref[slot, e, half:MOE_WIN, :])

    out_ref[...] = x_ref[...] + mod_ref[5:6, :] * _rms(acc_ref[...], pg_ref[...])


def _combine(s0, hoff, cnt, ys, route, x3, mod1, post_g):
    b, t, d = x3.shape
    nti = t // MOE_TILE
    row = lambda w: pl.BlockSpec((None, MOE_TILE, w), lambda bi, i, *_: (bi, i, 0))
    grid_spec = pltpu.PrefetchScalarGridSpec(
        num_scalar_prefetch=3,
        grid=(b, nti),
        in_specs=[pl.BlockSpec(memory_space=pl.ANY), row(LANES), row(d),
                  pl.BlockSpec((None, 6, d), lambda bi, i, *_: (bi, 0, 0)),
                  pl.BlockSpec((1, d), lambda bi, i, *_: (0, 0))],
        out_specs=row(d),
        scratch_shapes=[pltpu.VMEM((2, N_EXPERTS, MOE_WIN, d), BF16), pltpu.VMEM((MOE_TILE, d), F32),
                        pltpu.SemaphoreType.DMA((2, N_EXPERTS))],
    )
    return pl.pallas_call(
        functools.partial(_combine_kernel, nti=nti, nt=b * nti),
        grid_spec=grid_spec,
        out_shape=jax.ShapeDtypeStruct((b, t, d), F32),
        compiler_params=_cp("arbitrary", "arbitrary"),
        name="moe_combine",
    )(s0, hoff, cnt, ys, route, x3, mod1, post_g.reshape(1, d))


def _rope_tables(t_len):
    rows = t_len // GRID_W
    row = jnp.repeat(jnp.arange(rows, dtype=F32), GRID_W)
    col = jnp.tile(jnp.arange(GRID_W, dtype=F32), rows)
    axis_dim = DIFF_DQK // 2
    inv = ROPE_BASE ** (-jnp.arange(0, axis_dim, 2, dtype=F32) / axis_dim)
    ang_r = row[:, None] * inv
    ang_c = col[:, None] * inv
    cr, sr, cc, sc = jnp.cos(ang_r), jnp.sin(ang_r), jnp.cos(ang_c), jnp.sin(ang_c)
    cos = jnp.concatenate([cr, cr, cc, cc] * 2, axis=1)
    sin = jnp.concatenate([-sr, sr, -sc, sc] * 2, axis=1)
    return cos, sin


def kernel(x, c, ctx, c_ctx, l0_mod_w, l0_mod_b, l0_mix_pre_g, l0_mix_post_g, l0_w_in, l0_mlstm_gate_b, l0_mlstm_conv_w, l0_mlstm_norm_g, l0_lambda_q1, l0_lambda_k1, l0_lambda_q2, l0_lambda_k2, l0_diff_norm_g, l0_w_out, l0_ffn_pre_g, l0_ffn_post_g, l0_ffn_w1, l0_ffn_w3, l0_ffn_w2, l1_mod_w, l1_mod_b, l1_mix_pre_g, l1_mix_post_g, l1_conv_pw1_w, l1_conv_pw1_b, l1_conv_dw_w, l1_conv_dw_b, l1_conv_ln_g, l1_conv_ln_b, l1_conv_pw2_w, l1_conv_pw2_b, l1_ffn_pre_g, l1_ffn_post_g, l1_router_w, l1_moe_w1, l1_moe_w3, l1_moe_w2):
    b, t, d = x.shape
    n_ctx = ctx.shape[1]
    L = MLSTM_CHUNK
    assert d == D_MODEL and n_ctx == L and t % (2 * L) == 0 and b <= 8

    cpad = jnp.zeros((16, d), F32).at[:b].set(c).at[8].set(c_ctx)
    mod0 = _adaln(cpad, l0_mod_w, l0_mod_b).reshape(16, 6, d)
    mod1 = _adaln(cpad, l1_mod_w, l1_mod_b).reshape(16, 6, d)

    w_perm = jnp.concatenate(
        [l0_w_in[:, :2048], l0_w_in[:, 2080:], l0_w_in[:, 2048:2080], jnp.zeros((d, 96), F32)], axis=1).astype(BF16)
    gate_b = jnp.concatenate([l0_mlstm_gate_b, jnp.zeros((96,), F32)]).reshape(1, LANES)
    cos, sin = _rope_tables(t)
    g0 = l0_mix_pre_g.reshape(1, d)
    mq, mk, mv, mo, dq, dk, dv, gates = _inproj(x, mod0, None, g0, w_perm, l0_mlstm_conv_w, gate_b, cos, sin,
                                                 tm=512, rope=True)
    _, mkc, mvc, _, _, dkc, dvc, gates_c = _inproj(ctx, mod0, 8, g0, w_perm, l0_mlstm_conv_w, gate_b,
                                                   cos[:n_ctx], sin[:n_ctx], tm=n_ctx, rope=False)

    arow, cols = _gateprep(jnp.concatenate([gates_c, gates], axis=1), L=L)
    hm = _mlstm(mq, mk, mv, mo, mkc, mvc, arow, cols, l0_mlstm_norm_g, L=L)

    lam_init = 0.8 - 0.6 * math.exp(-0.3 * 0)
    lam_in = jnp.zeros((8, LANES), F32).at[0, :DIFF_DQK].set(l0_lambda_q1).at[1, :DIFF_DQK].set(l0_lambda_k1)
    lam_in = lam_in.at[2, :DIFF_DQK].set(l0_lambda_q2).at[3, :DIFF_DQK].set(l0_lambda_k2)
    hd = _attn(dq, jnp.concatenate([dkc, dk], axis=1), jnp.concatenate([dvc, dv], axis=1), lam_in,
               l0_diff_norm_g, tq=512, sub=256, max_sub_per_step=11, lam_init=lam_init)

    w_out = l0_w_out.astype(BF16)
    x1, u2 = _outproj(hm, hd, w_out[:512], w_out[512:], x, mod0, l0_mix_post_g, l0_ffn_pre_g, tm=512)
    x2, u3 = _ffn(u2, l0_ffn_w1.astype(BF16), l0_ffn_w3.astype(BF16), l0_ffn_w2.astype(BF16), x1, mod0, mod1,
                  l0_ffn_post_g, l1_mix_pre_g, tm=512)

    dw = jnp.concatenate([l1_conv_dw_w, jnp.zeros((1, d), F32)], axis=0)
    rw = jnp.concatenate([l1_router_w, jnp.zeros((d, LANES - N_EXPERTS), F32)], axis=1).astype(BF16)
    v1 = lambda a: a.reshape(1, -1)
    conv_tm = MOE_TILE // 2
    x3, u4, route, routet, cnt_tile = _convmod(
        u3, l1_conv_pw1_w.astype(BF16), v1(l1_conv_pw1_b), dw, v1(l1_conv_dw_b), v1(l1_conv_ln_g),
        v1(l1_conv_ln_b), l1_conv_pw2_w.astype(BF16), v1(l1_conv_pw2_b), x2, mod1, v1(l1_mix_post_g),
        v1(l1_ffn_pre_g), rw, tm=conv_tm)

    n = b * t
    rows = MOE_ROWS
    nt = n // MOE_TILE
    cnt_half = cnt_tile[:, 0, :N_EXPERTS].astype(jnp.int32).reshape(nt, 2, N_EXPERTS)
    cnt = cnt_half[:, 0] + cnt_half[:, 1]
    aligned = ((cnt + MOE_ALIGN - 1) // MOE_ALIGN) * MOE_ALIGN
    base = jnp.cumsum(aligned, axis=0) - aligned
    cap = ((base[-1] + MOE_WIN + rows - 1) // rows) * rows
    pend = jnp.cumsum(cap)
    s0 = (pend - cap)[None, :] + base
    n_blk = (2 * n + N_EXPERTS * (MOE_ALIGN * nt + MOE_WIN + rows)) // rows + 1
    block_e = jnp.clip(jnp.searchsorted(pend, jnp.arange(n_blk, dtype=jnp.int32) * rows, side='right'),
                       0, N_EXPERTS - 1).astype(jnp.int32)
    n_used = (pend[-1:] // rows).astype(jnp.int32)
    flat = lambda a: a.reshape(-1).astype(jnp.int32)

    xs = _dispatch(u4.reshape(n, d), routet, flat(s0), flat(cnt_half[:, 0]), flat(cnt), flat(pend - MOE_WIN),
                   n_used, n_rows=n_blk * rows)
    ys = _moe_ffn(xs, block_e, n_used, l1_moe_w1.astype(BF16), l1_moe_w3.astype(BF16),
                  l1_moe_w2.astype(BF16), rows=rows)
    return _combine(flat(s0), flat(cnt_half[:, 0]), flat(cnt), ys, route, x3, mod1, l1_ffn_post_g)
```

```python
import functools
import math

import jax
import jax.numpy as jnp
from jax import lax
from jax.experimental import pallas as pl
from jax.experimental.pallas import tpu as pltpu

F32 = jnp.float32
BF16 = jnp.bfloat16
EPS = 1e-6
NEG = -1e30
LOG2E = 1.4426950408889634

D_MODEL = 1024
GRID_W = 64
MLSTM_HEADS = 8
MLSTM_D = 64
MLSTM_CHUNK = 256
DIFF_HEADS = 4
DIFF_DQK = 64
ROPE_BASE = 10000.0
CONV_WIDTH = 31
N_EXPERTS = 8
MOE_ROWS = 512
MOE_TILE = 512
MOE_WIN = 512
MOE_ALIGN = 16
LANES = 128
VMEM_LIMIT = 52 * 1024 * 1024


def _cp(*sem):
    return pltpu.CompilerParams(dimension_semantics=sem, vmem_limit_bytes=VMEM_LIMIT)


def _rms(x, g):
    return x * lax.rsqrt(jnp.mean(x * x, axis=-1, keepdims=True) + EPS) * g


def _silu(x):
    return x * jax.nn.sigmoid(x)


def _dot(a, b):
    return jnp.dot(a, b, preferred_element_type=F32)


def _dot_nt(a, b):
    return lax.dot_general(a, b, (((1,), (1,)), ((), ())), preferred_element_type=F32)


def _dot_tn(a, b):
    return lax.dot_general(a, b, (((0,), (0,)), ((), ())), preferred_element_type=F32)


def _adaln_kernel(c_ref, w_ref, b_ref, o_ref):
    s = _silu(c_ref[...])
    o_ref[...] = _dot(s.astype(BF16), w_ref[...].astype(BF16)) + b_ref[...]


def _adaln(cpad, w, b):
    rows, d = cpad.shape
    n = w.shape[1]
    tn = 1536
    return pl.pallas_call(
        _adaln_kernel,
        grid=(n // tn,),
        in_specs=[pl.BlockSpec((rows, d), lambda j: (0, 0)),
                  pl.BlockSpec((d, tn), lambda j: (0, j)),
                  pl.BlockSpec((1, tn), lambda j: (0, j))],
        out_specs=pl.BlockSpec((rows, tn), lambda j: (0, j)),
        out_shape=jax.ShapeDtypeStruct((rows, n), F32),
        compiler_params=_cp("arbitrary"),
        name="adaln",
    )(cpad, w, b.reshape(1, n))


def _inproj_kernel(x_ref, xp_ref, xn_ref, mod_ref, g_ref, w_ref, cw_ref, gb_ref, cos_ref, sin_ref,
                   mq_ref, mk_ref, mv_ref, mo_ref, dq_ref, dk_ref, dv_ref, gt_ref, pext_ref,
                   *, tm, nt, rope):
    i = pl.program_id(1)
    g = g_ref[...]
    sh = mod_ref[0:1, :]
    sc = mod_ref[1:2, :]

    def mod(xv):
        return _rms(xv, g) * (1.0 + sc) + sh

    u = mod(x_ref[...])
    up = jnp.where(i > 0, mod(xp_ref[...]), 0.0)
    un = jnp.where(i < nt - 1, mod(xn_ref[...]), 0.0)
    ub = u.astype(BF16)
    uext = jnp.concatenate([up.astype(BF16), ub, un.astype(BF16)], axis=0)

    pext_ref[...] = _dot(uext, w_ref[:, 0:1024])
    cw = cw_ref[...]
    conv = (cw[0:1, :] * pext_ref[pl.ds(7, tm), :] + cw[1:2, :] * pext_ref[pl.ds(8, tm), :]
            + cw[2:3, :] * pext_ref[pl.ds(9, tm), :])
    act = _silu(conv)
    mq_ref[...] = (act[:, 0:512] * (MLSTM_D ** -0.5)).astype(BF16)
    mk_ref[...] = act[:, 512:1024].astype(BF16)

    p = _dot(ub, w_ref[:, 1024:2048])
    mv_ref[...] = p[:, 0:512].astype(BF16)
    mo_ref[...] = p[:, 512:1024].astype(BF16)

    p = _dot(ub, w_ref[:, 2048:3072])
    if rope:
        lane = lax.broadcasted_iota(jnp.int32, p.shape, 1)
        first_half = ((lane // 16) % 2) == 0
        nl = p.shape[1]
        partner = jnp.where(first_half, pltpu.roll(p, nl - 16, 1), pltpu.roll(p, 16, 1))
        cos = jnp.concatenate([cos_ref[...]] * 8, axis=1)
        sin = jnp.concatenate([sin_ref[...]] * 8, axis=1)
        p = p * cos + partner * sin
    dq_ref[...] = (p[:, 0:512] * (LOG2E * DIFF_DQK ** -0.5)).astype(BF16)
    dk_ref[...] = p[:, 512:1024].astype(BF16)

    p = _dot(ub, w_ref[:, 3072:3712])
    dv_ref[...] = p[:, 0:512].astype(BF16)
    gt_ref[...] = p[:, 512:640] + gb_ref[...]


def _inproj(x, mod, mod_row, g, w_perm, conv_w, gate_b, cos, sin, *, tm, rope):
    b, t, d = x.shape
    nt = t // tm
    hb = tm // 8
    nb8 = t // 8
    bf = lambda: jax.ShapeDtypeStruct((b, t, 512), BF16)
    if mod_row is None:
        mod_map = lambda bi, i: (bi, 0, 0)
    else:
        mod_map = lambda bi, i: (mod_row, 0, 0)
    kern = functools.partial(_inproj_kernel, tm=tm, nt=nt, rope=rope)
    o512 = pl.BlockSpec((None, tm, 512), lambda bi, i: (bi, i, 0))
    return pl.pallas_call(
        kern,
        grid=(b, nt),
        in_specs=[
            pl.BlockSpec((None, tm, d), lambda bi, i: (bi, i, 0)),
            pl.BlockSpec((None, 8, d), lambda bi, i: (bi, jnp.maximum(i * hb - 1, 0), 0)),
            pl.BlockSpec((None, 8, d), lambda bi, i: (bi, jnp.minimum((i + 1) * hb, nb8 - 1), 0)),
            pl.BlockSpec((None, 6, d), mod_map),
            pl.BlockSpec((1, d), lambda bi, i: (0, 0)),
            pl.BlockSpec((d, 3712), lambda bi, i: (0, 0)),
            pl.BlockSpec((3, d), lambda bi, i: (0, 0)),
            pl.BlockSpec((1, LANES), lambda bi, i: (0, 0)),
            pl.BlockSpec((tm, LANES), lambda bi, i: (i, 0)),
            pl.BlockSpec((tm, LANES), lambda bi, i: (i, 0)),
        ],
        out_specs=[o512] * 7 + [pl.BlockSpec((None, tm, LANES), lambda bi, i: (bi, i, 0))],
        out_shape=[bf() for _ in range(7)] + [jax.ShapeDtypeStruct((b, t, LANES), F32)],
        scratch_shapes=[pltpu.VMEM((tm + 16, 1024), F32)],
        compiler_params=_cp("arbitrary", "arbitrary"),
        name="inproj_rope" if rope else "inproj_ctx",
    )(x, x, x, mod, g, w_perm, conv_w, gate_b, cos, sin)


def _gateprep_kernel(g_ref, arow_ref, col_ref, *, L, nch):
    tt = L * nch
    gt = g_ref[...].T
    i_f, f_f, i_b, f_b = gt[0:8], gt[8:16], gt[16:24], gt[24:32]

    def logsig(v):
        return jnp.minimum(v, 0.0) - jnp.log1p(jnp.exp(-jnp.abs(v)))

    pos = lax.broadcasted_iota(jnp.int32, (8, tt), 1) % L

    def scan(v, op, ident, reverse):
        s = 1
        while s < L:
            if reverse:
                shifted = pltpu.roll(v, tt - s, 1)
                valid = pos < L - s
            else:
                shifted = pltpu.roll(v, s, 1)
                valid = pos >= s
            v = op(v, jnp.where(valid, shifted, ident))
            s *= 2
        return v

    outs = []
    for d, (ig, fg) in enumerate(((i_f, f_f), (i_b, f_b))):
        rev = d == 1
        bcum = scan(logsig(fg), jnp.add, 0.0, rev)
        a = ig - bcum
        cm = scan(a, jnp.maximum, NEG, rev)
        order = list(range(nch)) if not rev else [0] + list(range(nch - 1, 0, -1))
        mp = jnp.zeros((8, 1), F32)
        mp_c = [None] * nch
        ml_c = [None] * nch
        for j in order:
            e = j * L if rev else j * L + L - 1
            mlast = jnp.maximum(mp, cm[:, e:e + 1])
            mp_c[j] = jnp.broadcast_to(mp, (8, L))
            ml_c[j] = jnp.broadcast_to(mlast, (8, L))
            mp = bcum[:, e:e + 1] + mlast
        mprev = jnp.concatenate(mp_c, axis=1)
        mlast = jnp.concatenate(ml_c, axis=1)
        m = jnp.maximum(mprev, cm)
        outs.append((a, m, jnp.exp(mprev - m), jnp.exp(a - mlast), jnp.exp(-(bcum + m))))

    arow_ref[...] = jnp.concatenate([outs[0][0], outs[1][0]], axis=0)
    rows = []
    for pair in range(4):
        for q in range(1, 5):
            for d in range(2):
                rows.append(outs[d][q][2 * pair:2 * pair + 2])
    rows.append(jnp.zeros((64, tt), F32))
    col_ref[...] = jnp.concatenate(rows, axis=0).T


def _gateprep(gall, *, L):
    b, tt, _ = gall.shape
    nch = tt // L
    return pl.pallas_call(
        functools.partial(_gateprep_kernel, L=L, nch=nch),
        grid=(b,),
        in_specs=[pl.BlockSpec((None, tt, LANES), lambda bi: (bi, 0, 0))],
        out_specs=[pl.BlockSpec((None, 16, tt), lambda bi: (bi, 0, 0)),
                   pl.BlockSpec((None, tt, LANES), lambda bi: (bi, 0, 0))],
        out_shape=[jax.ShapeDtypeStruct((b, 16, tt), F32), jax.ShapeDtypeStruct((b, tt, LANES), F32)],
        compiler_params=_cp("arbitrary"),
        name="mlstm_gateprep",
    )(gall)


def _mlstm_kernel(q_ref, k_ref, v_ref, o_ref, kc_ref, vc_ref, arow_ref, col_ref, ng_ref, out_ref,
                  c_ref, hf_ref, hb_ref, *, L, nc):
    p = pl.program_id(1)
    half = nc // 2
    lane = lax.broadcasted_iota(jnp.int32, (L, LANES), 1)
    lo = lane < MLSTM_D
    head_mask = (lo, jnp.logical_not(lo))
    ri = lax.broadcasted_iota(jnp.int32, (L, L), 0)
    ci = lax.broadcasted_iota(jnp.int32, (L, L), 1)
    causal = (ci <= ri, ci >= ri)
    ones_t = jnp.ones((L, LANES), BF16)
    shift = lax.rem(LANES - 16 * p, LANES)

    def cols(off):
        return pltpu.roll(col_ref[pl.ds(off, L), :], shift, 1)

    def col(blk, q, d, hh):
        j = q * 4 + d * 2 + hh
        return blk[:, j:j + 1]

    def vext_of(vb, hh):
        return jnp.concatenate([jnp.where(head_mask[hh], vb, jnp.zeros_like(vb)), ones_t], axis=1)

    def state_update(d, hh, kb, vext, blk, dec):
        khm = jnp.where(head_mask[hh], kb, jnp.zeros_like(kb))
        wv = (col(blk, 2, d, hh) * vext.astype(F32)).astype(BF16)
        upd = _dot_tn(khm, wv)
        if dec is None:
            c_ref[d, hh] = upd
        else:
            c_ref[d, hh] = dec * c_ref[d, hh] + upd

    blk0 = cols(0)
    kcb = kc_ref[...]
    vcb = vc_ref[...]
    for d in range(2):
        for hh in range(2):
            state_update(d, hh, kcb, vext_of(vcb, hh), blk0, None)

    def compute(d, c):
        t0 = pl.multiple_of(c * L, L)
        off = pl.multiple_of(c * L + L, L)
        qb = q_ref[pl.ds(t0, L), :]
        kb = k_ref[pl.ds(t0, L), :]
        vb = v_ref[pl.ds(t0, L), :]
        blk = cols(off)
        last = L - 1 if d == 0 else 0
        hs = []
        for hh in range(2):
            arow = arow_ref[pl.ds(d * 8 + 2 * p + hh, 1), pl.ds(off, L)]
            khm = jnp.where(head_mask[hh], kb, jnp.zeros_like(kb))
            s = _dot_nt(qb, khm)
            arg = jnp.where(causal[d], arow - col(blk, 0, d, hh), NEG)
            pm = (s * jnp.exp(arg)).astype(BF16)
            vext = vext_of(vb, hh)
            ch = c_ref[d, hh]
            tot = _dot(pm, vext) + col(blk, 1, d, hh) * _dot(qb, ch.astype(BF16))
            den = jnp.maximum(jnp.abs(tot[:, LANES:]), col(blk, 3, d, hh))
            hs.append(tot[:, :LANES] / den)
            dec = col(blk, 1, d, hh)[last:last + 1, :]
            state_update(d, hh, kb, vext, blk, dec)
        return jnp.where(lo, hs[0], hs[1])

    def finalize(c, hsum):
        t0 = pl.multiple_of(c * L, L)
        sq = hsum * hsum
        s0 = jnp.sum(jnp.where(lo, sq, 0.0), axis=-1, keepdims=True)
        s1 = jnp.sum(jnp.where(lo, 0.0, sq), axis=-1, keepdims=True)
        ms = jnp.where(lo, s0, s1) * (1.0 / MLSTM_D)
        y = hsum * lax.rsqrt(ms + EPS) * ng_ref[...]
        gate = jax.nn.sigmoid(o_ref[pl.ds(t0, L), :].astype(F32))
        out_ref[pl.ds(t0, L), :] = (y * gate).astype(BF16)

    def phase_a(i, carry):
        hf_ref[pl.ds(pl.multiple_of(i * L, L), L), :] = compute(0, i)
        cb = nc - 1 - i
        hb_ref[pl.ds(pl.multiple_of((cb - half) * L, L), L), :] = compute(1, cb)
        return carry

    def phase_b(i, carry):
        hf = compute(0, i)
        finalize(i, hf + hb_ref[pl.ds(pl.multiple_of((i - half) * L, L), L), :])
        cb = nc - 1 - i
        hb = compute(1, cb)
        finalize(cb, hb + hf_ref[pl.ds(pl.multiple_of(cb * L, L), L), :])
        return carry

    lax.fori_loop(0, half, phase_a, 0)
    lax.fori_loop(half, nc, phase_b, 0)


def _mlstm(mq, mk, mv, mo, mkc, mvc, arow, cols, norm_g, *, L):
    b, t, _ = mq.shape
    ctx = mkc.shape[1]
    tt = arow.shape[2]
    nc = t // L
    tok = pl.BlockSpec((None, t, LANES), lambda bi, p: (bi, 0, p))
    ctxs = pl.BlockSpec((None, ctx, LANES), lambda bi, p: (bi, 0, p))
    return pl.pallas_call(
        functools.partial(_mlstm_kernel, L=L, nc=nc),
        grid=(b, 4),
        in_specs=[tok, tok, tok, tok, ctxs, ctxs,
                  pl.BlockSpec((None, 16, tt), lambda bi, p: (bi, 0, 0)),
                  pl.BlockSpec((None, tt, LANES), lambda bi, p: (bi, 0, 0)),
                  pl.BlockSpec((1, LANES), lambda bi, p: (0, p))],
        out_specs=tok,
        out_shape=jax.ShapeDtypeStruct((b, t, 512), BF16),
        scratch_shapes=[pltpu.VMEM((2, 2, LANES, 2 * LANES), F32),
                        pltpu.VMEM((t // 2, LANES), F32),
                        pltpu.VMEM((t // 2, LANES), F32)],
        compiler_params=_cp("arbitrary", "arbitrary"),
        name="mlstm_scan",
    )(mq, mk, mv, mo, mkc, mvc, arow, cols, norm_g.reshape(1, 512))


def _attn_kernel(q_ref, k_ref, v_ref, lam_ref, ng_ref, out_ref, kmax_ref, m_ref, acc_ref,
                 *, tq, kb, nkb, sub, lam_init):
    i = pl.program_id(2)
    lane = lax.broadcasted_iota(jnp.int32, (tq, LANES), 1)
    lo = lane < DIFF_DQK
    rr = lax.broadcasted_iota(jnp.int32, (LANES, LANES), 0)
    cc = lax.broadcasted_iota(jnp.int32, (LANES, LANES), 1)
    same_comp = ((rr < DIFF_DQK) == (cc < DIFF_DQK)).astype(BF16)

    def comp_sqnorm(a):
        af = a.astype(F32)
        return _dot((af * af).astype(BF16), same_comp)

    @pl.when(i == 0)
    def _():
        mx = jnp.zeros((1, LANES), F32)

        def kbody(j, mx):
            k0 = pl.multiple_of(j * kb, kb)
            return jnp.maximum(mx, jnp.max(comp_sqnorm(k_ref[pl.ds(k0, kb), :]), axis=0, keepdims=True))

        kmax_ref[...] = lax.fori_loop(0, nkb, kbody, mx)

    q = q_ref[...]
    zq = jnp.zeros_like(q)
    bnd = jnp.sqrt(comp_sqnorm(q) * kmax_ref[...]) * 1.02
    b1 = jnp.where(lane == 0, -bnd, 0.0)
    b2 = jnp.where(lane == 0, -pltpu.roll(bnd, DIFF_DQK, 1), 0.0)
    qs = jnp.concatenate(
        [jnp.concatenate([jnp.where(lo, q, zq), jnp.where(lo, zq, q)], axis=0),
         jnp.concatenate([b1, b2], axis=0).astype(BF16)], axis=1)
    acc_ref[...] = jnp.zeros(acc_ref.shape, F32)

    def ext(blk):
        return jnp.concatenate([blk, jnp.ones_like(blk)], axis=1)

    def run(step):
        def body(j, carry):
            k0 = pl.multiple_of(j * kb, kb)
            step(k_ref[pl.ds(k0, kb), :], v_ref[pl.ds(k0, kb), :])
            return carry

        lax.fori_loop(0, nkb, body, 0)

    def fast_step(kblk, vblk):
        tot = None
        for c in range(0, kblk.shape[0], sub):
            pm = jnp.exp2(_dot_nt(qs, ext(kblk[c:c + sub]))).astype(BF16)
            part = _dot(pm, ext(vblk[c:c + sub]))
            tot = part if tot is None else tot + part
        acc_ref[...] += tot

    def slow_step(kblk, vblk):
        s = _dot_nt(qs, ext(kblk))
        m_old = m_ref[...]
        m_new = jnp.maximum(m_old, jnp.max(s, axis=-1, keepdims=True))
        pm = jnp.exp2(s - m_new).astype(BF16)
        acc_ref[...] = jnp.exp2(m_old - m_new) * acc_ref[...] + _dot(pm, ext(vblk))
        m_ref[...] = m_new

    fast = jnp.max(bnd) <= 56.0

    @pl.when(fast)
    def _():
        run(fast_step)

    @pl.when(jnp.logical_not(fast))
    def _():
        m_ref[...] = jnp.full(m_ref.shape, NEG, F32)
        run(slow_step)

    lq = lam_ref[...]
    lam = (jnp.exp(jnp.sum(lq[0:1, :] * lq[1:2, :], axis=-1, keepdims=True))
           - jnp.exp(jnp.sum(lq[2:3, :] * lq[3:4, :], axis=-1, keepdims=True)) + lam_init)
    a1 = acc_ref[0:tq, :]
    a2 = acc_ref[tq:2 * tq, :]
    o = a1[:, :LANES] / a1[:, LANES:] - lam * (a2[:, :LANES] / a2[:, LANES:])
    out_ref[...] = (_rms(o, ng_ref[...]) * (1.0 - lam_init)).astype(BF16)


def _attn(dq, dk, dv, lam_in, norm_g, *, tq, sub, max_sub_per_step, lam_init):
    b, t, _ = dq.shape
    tk = dk.shape[1]
    nsub = tk // sub
    per = max(g for g in range(1, max_sub_per_step + 1) if nsub % g == 0)
    kb = per * sub
    full = pl.BlockSpec((None, tk, LANES), lambda bi, h, i: (bi, 0, h))
    qs = pl.BlockSpec((None, tq, LANES), lambda bi, h, i: (bi, i, h))
    return pl.pallas_call(
        functools.partial(_attn_kernel, tq=tq, kb=kb, nkb=tk // kb, sub=sub, lam_init=lam_init),
        grid=(b, DIFF_HEADS, t // tq),
        in_specs=[qs, full, full,
                  pl.BlockSpec((8, LANES), lambda bi, h, i: (0, 0)),
                  pl.BlockSpec((1, LANES), lambda bi, h, i: (0, h))],
        out_specs=qs,
        out_shape=jax.ShapeDtypeStruct((b, t, 512), BF16),
        scratch_shapes=[pltpu.VMEM((1, LANES), F32), pltpu.VMEM((2 * tq, 1), F32),
                        pltpu.VMEM((2 * tq, 2 * LANES), F32)],
        compiler_params=_cp("arbitrary", "arbitrary", "arbitrary"),
        name="diff_attn",
    )(dq, dk, dv, lam_in, norm_g.reshape(1, 512))


def _outproj_kernel(hm_ref, hd_ref, wt_ref, wb_ref, x_ref, mod_ref, pg_ref, fg_ref, x1_ref, u_ref):
    mix = _dot(hm_ref[...], wt_ref[...]) + _dot(hd_ref[...], wb_ref[...])
    x1 = x_ref[...] + mod_ref[2:3, :] * _rms(mix, pg_ref[...])
    x1_ref[...] = x1
    u_ref[...] = (_rms(x1, fg_ref[...]) * (1.0 + mod_ref[4:5, :]) + mod_ref[3:4, :]).astype(BF16)


def _outproj(hm, hd, wt, wb, x, mod, post_g, ffn_pre_g, *, tm):
    b, t, d = x.shape
    row = lambda w: pl.BlockSpec((None, tm, w), lambda bi, i: (bi, i, 0))
    cst = lambda s: pl.BlockSpec(s, lambda bi, i: (0,) * len(s))
    return pl.pallas_call(
        _outproj_kernel,
        grid=(b, t // tm),
        in_specs=[row(512), row(512), cst((512, d)), cst((512, d)), row(d),
                  pl.BlockSpec((None, 6, d), lambda bi, i: (bi, 0, 0)), cst((1, d)), cst((1, d))],
        out_specs=[row(d), row(d)],
        out_shape=[jax.ShapeDtypeStruct((b, t, d), F32), jax.ShapeDtypeStruct((b, t, d), BF16)],
        compiler_params=_cp("arbitrary", "arbitrary"),
        name="outproj",
    )(hm, hd, wt, wb, x, mod, post_g.reshape(1, d), ffn_pre_g.reshape(1, d))


def _swiglu(u, w1_ref, w3_ref, w2_ref):
    return _dot((_silu(_dot(u, w1_ref[...])) * _dot(u, w3_ref[...])).astype(BF16), w2_ref[...])


def _ffn_kernel(u_ref, w1_ref, w3_ref, w2_ref, x_ref, mod0_ref, mod1_ref, pg_ref, ng_ref, pw_ref, pb_ref,
                x2_ref, h_ref):
    y = _swiglu(u_ref[...], w1_ref, w3_ref, w2_ref)
    x2 = x_ref[...] + mod0_ref[5:6, :] * _rms(y, pg_ref[...])
    x2_ref[...] = x2
    u3 = (_rms(x2, ng_ref[...]) * (1.0 + mod1_ref[1:2, :]) + mod1_ref[0:1, :]).astype(BF16)
    ag = _dot(u3, pw_ref[...]) + pb_ref[...]
    h_ref[...] = ag[:, :D_MODEL] * jax.nn.sigmoid(ag[:, D_MODEL:])


def _ffn(u, w1, w3, w2, x1, mod0, mod1, post_g, next_pre_g, pw1, pb1, *, tm):
    b, t, d = x1.shape
    f = w1.shape[1]
    row = pl.BlockSpec((None, tm, d), lambda bi, i: (bi, i, 0))
    modb = pl.BlockSpec((None, 6, d), lambda bi, i: (bi, 0, 0))
    vec = lambda w: pl.BlockSpec((1, w), lambda bi, i: (0, 0))
    wspec = lambda s: pl.BlockSpec(s, lambda bi, i: (0, 0), pipeline_mode=pl.Buffered(1))
    return pl.pallas_call(
        _ffn_kernel,
        grid=(b, t // tm),
        in_specs=[row, wspec((d, f)), wspec((d, f)), wspec((f, d)), row, modb, modb, vec(d), vec(d),
                  wspec((d, 2 * d)), vec(2 * d)],
        out_specs=[row, row],
        out_shape=[jax.ShapeDtypeStruct((b, t, d), F32), jax.ShapeDtypeStruct((b, t, d), F32)],
        compiler_params=_cp("arbitrary", "arbitrary"),
        name="ffn_swiglu_glu",
    )(u, w1, w3, w2, x1, mod0, mod1, post_g.reshape(1, d), next_pre_g.reshape(1, d), pw1, pb1)


HALO = 16
CONV_ROW_BLOCK = 64
CONV_LANES = 512
CONV_ROWS_EXTRA = 24


def _conv_kernel(h_ref, hp_ref, hn_ref, dw_ref, dwb_ref, lng_ref, lnb_ref, w2_ref, b2_ref,
                 x_ref, mod_ref, pg_ref, fg_ref, rw_ref, x3_ref, u4_ref, route_ref, routet_ref, cnt_ref,
                 hs_ref, sh_ref, cv_ref, wb_ref, *, tm, nt):
    i = pl.program_id(1)

    hs_ref[0:HALO, :] = jnp.where(i > 0, hp_ref[...], 0.0)
    hs_ref[HALO:HALO + tm, :] = h_ref[...]
    hs_ref[HALO + tm:, :] = jnp.where(i < nt - 1, hn_ref[...], 0.0)

    for r in range(1, 8):
        sh_ref[r - 1] = hs_ref[pl.ds(r, tm + CONV_ROWS_EXTRA), :]

    for j in range(CONV_WIDTH):
        wb_ref[j] = jnp.broadcast_to(dw_ref[j:j + 1, :], (8, D_MODEL))
    nsub = CONV_ROW_BLOCK // 8

    for l0 in range(0, D_MODEL, CONV_LANES):
        def conv_rows(rb, carry, l0=l0):
            r0 = pl.multiple_of(rb * CONV_ROW_BLOCK, CONV_ROW_BLOCK)
            acc = jnp.broadcast_to(dwb_ref[:, l0:l0 + CONV_LANES].reshape(1, 1, CONV_LANES),
                                   (nsub, 8, CONV_LANES))
            for j in range(CONV_WIDTH):
                r, a = (j + 1) % 8, (j + 1) // 8
                if r == 0:
                    win = hs_ref[pl.ds(r0 + 8 * a, CONV_ROW_BLOCK), l0:l0 + CONV_LANES]
                else:
                    win = sh_ref[r - 1, pl.ds(r0 + 8 * a, CONV_ROW_BLOCK), l0:l0 + CONV_LANES]
                acc = acc + wb_ref[j, :, l0:l0 + CONV_LANES][None] * win.reshape(nsub, 8, CONV_LANES)
            cv_ref[pl.ds(r0, CONV_ROW_BLOCK), l0:l0 + CONV_LANES] = acc.reshape(CONV_ROW_BLOCK, CONV_LANES)
            return carry

        lax.fori_loop(0, tm // CONV_ROW_BLOCK, conv_rows, 0)
    acc = cv_ref[...]
    mu = jnp.mean(acc, axis=-1, keepdims=True)
    cen = acc - mu
    var = jnp.mean(cen * cen, axis=-1, keepdims=True)
    hn = _silu(cen * lax.rsqrt(var + EPS) * lng_ref[...] + lnb_ref[...])
    y = _dot(hn.astype(BF16), w2_ref[...]) + b2_ref[...]
    x3 = x_ref[...] + mod_ref[2:3, :] * _rms(y, pg_ref[...])
    x3_ref[...] = x3
    u4 = _rms(x3, fg_ref[...]) * (1.0 + mod_ref[4:5, :]) + mod_ref[3:4, :]
    u4b = u4.astype(BF16)
    u4_ref[...] = u4b

    lane = lax.broadcasted_iota(jnp.int32, (tm, LANES), 1).astype(F32)
    logits = jnp.where(lane < N_EXPERTS, _dot(u4b, rw_ref[...]), NEG)
    m1 = jnp.max(logits, axis=-1, keepdims=True)
    i1 = jnp.min(jnp.where(logits == m1, lane, float(LANES)), axis=-1, keepdims=True)
    l2 = jnp.where(lane == i1, NEG, logits)
    m2 = jnp.max(l2, axis=-1, keepdims=True)
    i2 = jnp.min(jnp.where(l2 == m2, lane, float(LANES)), axis=-1, keepdims=True)
    e21 = jnp.exp(m2 - m1)
    g1 = 1.0 / (1.0 + e21)
    g2 = e21 * g1
    sel = ((lane == i1) | (lane == i2)).astype(F32)
    ri = lax.broadcasted_iota(jnp.int32, (tm, tm), 0)
    ci = lax.broadcasted_iota(jnp.int32, (tm, tm), 1)
    tri = (ci <= ri).astype(BF16)
    csum = _dot(tri, sel.astype(BF16))
    rank = csum - sel
    r1 = jnp.sum(jnp.where(lane == i1, rank, 0.0), axis=-1, keepdims=True)
    r2 = jnp.sum(jnp.where(lane == i2, rank, 0.0), axis=-1, keepdims=True)
    cnt_ref[...] = jnp.broadcast_to(csum[tm - 1:tm, :], (8, LANES))
    route = jnp.zeros((tm, LANES), F32)
    for n, v in enumerate((i1, i2, g1, g2, r1, r2)):
        route = jnp.where(lane == float(n), v, route)
    route_ref[...] = route
    routet_ref[...] = route.T[0:8, :]


def _convmod(h, dw, dwb, lng, lnb, w2, b2, x2, mod1, post_g, ffn_pre_g, rw, *, tm):
    b, t, d = x2.shape
    nt = t // tm
    hb = tm // HALO
    nbh = t // HALO
    row = lambda w: pl.BlockSpec((None, tm, w), lambda bi, i: (bi, i, 0))
    cst = lambda s: pl.BlockSpec(s, lambda bi, i: (0,) * len(s))
    return pl.pallas_call(
        functools.partial(_conv_kernel, tm=tm, nt=nt),
        grid=(b, nt),
        in_specs=[row(d),
                  pl.BlockSpec((None, HALO, d), lambda bi, i: (bi, jnp.maximum(i * hb - 1, 0), 0)),
                  pl.BlockSpec((None, HALO, d), lambda bi, i: (bi, jnp.minimum((i + 1) * hb, nbh - 1), 0)),
                  cst((32, d)), cst((1, d)), cst((1, d)), cst((1, d)),
                  cst((d, d)), cst((1, d)), row(d),
                  pl.BlockSpec((None, 6, d), lambda bi, i: (bi, 0, 0)), cst((1, d)), cst((1, d)),
                  cst((d, LANES))],
        out_specs=[row(d), row(d), row(LANES),
                   pl.BlockSpec((None, 8, tm), lambda bi, i: (bi * nt + i, 0, 0)),
                   pl.BlockSpec((None, 8, LANES), lambda bi, i: (bi * nt + i, 0, 0))],
        out_shape=[jax.ShapeDtypeStruct((b, t, d), F32), jax.ShapeDtypeStruct((b, t, d), BF16),
                   jax.ShapeDtypeStruct((b, t, LANES), F32), jax.ShapeDtypeStruct((b * nt, 8, tm), F32),
                   jax.ShapeDtypeStruct((b * nt, 8, LANES), F32)],
        scratch_shapes=[pltpu.VMEM((tm + 2 * HALO, d), F32), pltpu.VMEM((7, tm + CONV_ROWS_EXTRA, d), F32),
                        pltpu.VMEM((tm, d), F32), pltpu.VMEM((CONV_WIDTH, 8, d), F32)],
        compiler_params=_cp("arbitrary", "arbitrary"),
        name="conv_module_router",
    )(h, h, h, dw, dwb, lng, lnb, w2, b2, x2, mod1, post_g, ffn_pre_g, rw)


def _window_copies(ref_hbm, buf_ref, sem, s0_ref, t, slot, to_hbm):
    out = []
    for e in range(N_EXPERTS):
        hbm = ref_hbm.at[pl.ds(pl.multiple_of(s0_ref[t * N_EXPERTS + e], MOE_ALIGN), MOE_WIN), :]
        vmem = buf_ref.at[slot, e]
        src, dst = (vmem, hbm) if to_hbm else (hbm, vmem)
        out.append(pltpu.make_async_copy(src, dst, sem.at[slot, e]))
    return out


def _dispatch_kernel(s0_ref, hoff_ref, cnt_ref, zoff_ref, nu_ref, u_ref, rt_ref, xs_ref, win_ref, zero_ref,
                     sem, zsem, *, nt, n_blk):
    t = pl.program_id(0)
    slot = lax.rem(t, 2)
    half = MOE_TILE // 2

    @pl.when(t == 0)
    def _():
        zero_ref[...] = jnp.zeros(zero_ref.shape, BF16)
        zc = [pltpu.make_async_copy(
            zero_ref, xs_ref.at[pl.ds(pl.multiple_of(zoff_ref[e], MOE_ALIGN), MOE_WIN), :], zsem.at[e])
            for e in range(N_EXPERTS)]
        for c in zc:
            c.start()
        for c in zc:
            c.wait()

        def clear_block(bk, carry):
            c = pltpu.make_async_copy(
                zero_ref, xs_ref.at[pl.ds(pl.multiple_of(bk * MOE_ROWS, MOE_ROWS), MOE_ROWS), :], zsem.at[0])
            c.start()
            c.wait()
            return carry

        lax.fori_loop(nu_ref[0], n_blk, clear_block, 0)

    rt = jnp.concatenate([rt_ref[0], rt_ref[1]], axis=1)
    e1, e2, r1, r2 = rt[0:1], rt[1:2], rt[4:5], rt[5:6]
    second = lax.broadcasted_iota(jnp.int32, (1, MOE_TILE), 1) >= half
    srow = lax.broadcasted_iota(jnp.int32, (half, MOE_TILE), 0).astype(F32)
    u = u_ref[...]
    for e in range(N_EXPERTS):
        off = jnp.where(second, hoff_ref[t * N_EXPERTS + e].astype(F32), 0.0)
        lr = jnp.where(e1 == float(e), r1 + off, jnp.where(e2 == float(e), r2 + off, -1.0))
        win_ref[slot, e, 0:half, :] = _dot((srow == lr).astype(BF16), u).astype(BF16)
        big = cnt_ref[t * N_EXPERTS + e] > half

        @pl.when(big)
        def _():
            win_ref[slot, e, half:MOE_WIN, :] = _dot((srow + float(half) == lr).astype(BF16), u).astype(BF16)

        @pl.when(jnp.logical_not(big))
        def _():
            win_ref[slot, e, half:MOE_WIN, :] = jnp.zeros((half, D_MODEL), BF16)

    @pl.when(t > 0)
    def _():
        for c in _window_copies(xs_ref, win_ref, sem, s0_ref, t - 1, 1 - slot, True):
            c.wait()

    for c in _window_copies(xs_ref, win_ref, sem, s0_ref, t, slot, True):
        c.start()

    @pl.when(t == nt - 1)
    def _():
        for c in _window_copies(xs_ref, win_ref, sem, s0_ref, t, slot, True):
            c.wait()


def _dispatch(u4, routet, s0, hoff, cnt, zoff, n_used, *, n_rows):
    assert MOE_WIN == MOE_ROWS
    n, d = u4.shape
    nt = n // MOE_TILE
    per = MOE_TILE // routet.shape[2]
    grid_spec = pltpu.PrefetchScalarGridSpec(
        num_scalar_prefetch=5,
        grid=(nt,),
        in_specs=[pl.BlockSpec((MOE_TILE, d), lambda t, *_: (t, 0)),
                  pl.BlockSpec((per, 8, routet.shape[2]), lambda t, *_: (t, 0, 0))],
        out_specs=pl.BlockSpec(memory_space=pl.ANY),
        scratch_shapes=[pltpu.VMEM((2, N_EXPERTS, MOE_WIN, d), BF16), pltpu.VMEM((MOE_WIN, d), BF16),
                        pltpu.SemaphoreType.DMA((2, N_EXPERTS)), pltpu.SemaphoreType.DMA((N_EXPERTS,))],
    )
    return pl.pallas_call(
        functools.partial(_dispatch_kernel, nt=nt, n_blk=n_rows // MOE_ROWS),
        grid_spec=grid_spec,
        out_shape=jax.ShapeDtypeStruct((n_rows, d), BF16),
        compiler_params=_cp("arbitrary"),
        name="moe_dispatch",
    )(s0, hoff, cnt, zoff, n_used, u4, routet)


def _moe_kernel(be_ref, used_ref, x_ref, w1_ref, w3_ref, w2_ref, y_ref):
    used = used_ref[pl.program_id(0)] > 0

    @pl.when(used)
    def _():
        y_ref[...] = _swiglu(x_ref[...], w1_ref, w3_ref, w2_ref).astype(BF16)

    @pl.when(jnp.logical_not(used))
    def _():
        y_ref[...] = jnp.zeros(y_ref.shape, BF16)


def _moe_ffn(xs, block_e, block_used, w1, w3, w2, *, rows):
    n_rows, d = xs.shape
    f = w1.shape[2]
    wspec = lambda s: pl.BlockSpec((None,) + s, lambda i, be, bu: (be[i], 0, 0), pipeline_mode=pl.Buffered(1))
    grid_spec = pltpu.PrefetchScalarGridSpec(
        num_scalar_prefetch=2,
        grid=(n_rows // rows,),
        in_specs=[pl.BlockSpec((rows, d), lambda i, be, bu: (i, 0)),
                  wspec((d, f)), wspec((d, f)), wspec((f, d))],
        out_specs=pl.BlockSpec((rows, d), lambda i, be, bu: (i, 0)),
    )
    return pl.pallas_call(
        _moe_kernel,
        grid_spec=grid_spec,
        out_shape=jax.ShapeDtypeStruct((n_rows, d), BF16),
        compiler_params=_cp("arbitrary"),
        name="moe_ffn",
    )(block_e, block_used, xs, w1, w3, w2)


def _combine_kernel(s0_ref, hoff_ref, cnt_ref, ys_ref, route_ref, x_ref, mod_ref, pg_ref, out_ref,
                    buf_ref, acc_ref, sem, *, nti, nt):
    t = pl.program_id(0) * nti + pl.program_id(1)
    slot = lax.rem(t, 2)
    half = MOE_TILE // 2

    @pl.when(t == 0)
    def _():
        for c in _window_copies(ys_ref, buf_ref, sem, s0_ref, 0, 0, False):
            c.start()

    @pl.when(t + 1 < nt)
    def _():
        for c in _window_copies(ys_ref, buf_ref, sem, s0_ref, t + 1, 1 - slot, False):
            c.start()

    for c in _window_copies(ys_ref, buf_ref, sem, s0_ref, t, slot, False):
        c.wait()

    route = route_ref[...]
    e1, e2, g1, g2, r1, r2 = (route[:, n:n + 1] for n in range(6))
    second = lax.broadcasted_iota(jnp.int32, (MOE_TILE, 1), 0) >= half
    scol = lax.broadcasted_iota(jnp.int32, (MOE_TILE, half), 1).astype(F32)
    for e in range(N_EXPERTS):
        off = jnp.where(second, hoff_ref[t * N_EXPERTS + e].astype(F32), 0.0)
        is1 = e1 == float(e)
        is2 = e2 == float(e)
        lr = jnp.where(is1, r1 + off, jnp.where(is2, r2 + off, -1.0))
        ge = jnp.where(is1, g1, jnp.where(is2, g2, 0.0))
        z = ge * _dot((scol == lr).astype(BF16), buf_ref[slot, e, 0:half, :])
        if e == 0:
            acc_ref[...] = z
        else:
            acc_ref[...] += z

        @pl.when(cnt_ref[t * N_EXPERTS + e] > half)
        def _():
            acc_ref[...] += ge * _dot((scol + float(half) == lr).astype(BF16), buf_ref[slot, e, half:MOE_WIN, :])

    out_ref[...] = x_ref[...] + mod_ref[5:6, :] * _rms(acc_ref[...], pg_ref[...])


def _combine(s0, hoff, cnt, ys, route, x3, mod1, post_g):
    b, t, d = x3.shape
    nti = t // MOE_TILE
    row = lambda w: pl.BlockSpec((None, MOE_TILE, w), lambda bi, i, *_: (bi, i, 0))
    grid_spec = pltpu.PrefetchScalarGridSpec(
        num_scalar_prefetch=3,
        grid=(b, nti),
        in_specs=[pl.BlockSpec(memory_space=pl.ANY), row(LANES), row(d),
                  pl.BlockSpec((None, 6, d), lambda bi, i, *_: (bi, 0, 0)),
                  pl.BlockSpec((1, d), lambda bi, i, *_: (0, 0))],
        out_specs=row(d),
        scratch_shapes=[pltpu.VMEM((2, N_EXPERTS, MOE_WIN, d), BF16), pltpu.VMEM((MOE_TILE, d), F32),
                        pltpu.SemaphoreType.DMA((2, N_EXPERTS))],
    )
    return pl.pallas_call(
        functools.partial(_combine_kernel, nti=nti, nt=b * nti),
        grid_spec=grid_spec,
        out_shape=jax.ShapeDtypeStruct((b, t, d), F32),
        compiler_params=_cp("arbitrary", "arbitrary"),
        name="moe_combine",
    )(s0, hoff, cnt, ys, route, x3, mod1, post_g.reshape(1, d))


def _rope_tables(t_len):
    rows = t_len // GRID_W
    row = jnp.repeat(jnp.arange(rows, dtype=F32), GRID_W)
    col = jnp.tile(jnp.arange(GRID_W, dtype=F32), rows)
    axis_dim = DIFF_DQK // 2
    inv = ROPE_BASE ** (-jnp.arange(0, axis_dim, 2, dtype=F32) / axis_dim)
    ang_r = row[:, None] * inv
    ang_c = col[:, None] * inv
    cr, sr, cc, sc = jnp.cos(ang_r), jnp.sin(ang_r), jnp.cos(ang_c), jnp.sin(ang_c)
    cos = jnp.concatenate([cr, cr, cc, cc] * 2, axis=1)
    sin = jnp.concatenate([-sr, sr, -sc, sc] * 2, axis=1)
    return cos, sin


def kernel(x, c, ctx, c_ctx, l0_mod_w, l0_mod_b, l0_mix_pre_g, l0_mix_post_g, l0_w_in, l0_mlstm_gate_b, l0_mlstm_conv_w, l0_mlstm_norm_g, l0_lambda_q1, l0_lambda_k1, l0_lambda_q2, l0_lambda_k2, l0_diff_norm_g, l0_w_out, l0_ffn_pre_g, l0_ffn_post_g, l0_ffn_w1, l0_ffn_w3, l0_ffn_w2, l1_mod_w, l1_mod_b, l1_mix_pre_g, l1_mix_post_g, l1_conv_pw1_w, l1_conv_pw1_b, l1_conv_dw_w, l1_conv_dw_b, l1_conv_ln_g, l1_conv_ln_b, l1_conv_pw2_w, l1_conv_pw2_b, l1_ffn_pre_g, l1_ffn_post_g, l1_router_w, l1_moe_w1, l1_moe_w3, l1_moe_w2):
    b, t, d = x.shape
    n_ctx = ctx.shape[1]
    L = MLSTM_CHUNK
    assert d == D_MODEL and n_ctx == L and t % (2 * L) == 0 and b <= 8

    cpad = jnp.zeros((16, d), F32).at[:b].set(c).at[8].set(c_ctx)
    mod0 = _adaln(cpad, l0_mod_w, l0_mod_b).reshape(16, 6, d)
    mod1 = _adaln(cpad, l1_mod_w, l1_mod_b).reshape(16, 6, d)

    w_perm = jnp.concatenate(
        [l0_w_in[:, :2048], l0_w_in[:, 2080:], l0_w_in[:, 2048:2080], jnp.zeros((d, 96), F32)], axis=1).astype(BF16)
    gate_b = jnp.concatenate([l0_mlstm_gate_b, jnp.zeros((96,), F32)]).reshape(1, LANES)
    cos, sin = _rope_tables(t)
    g0 = l0_mix_pre_g.reshape(1, d)
    mq, mk, mv, mo, dq, dk, dv, gates = _inproj(x, mod0, None, g0, w_perm, l0_mlstm_conv_w, gate_b, cos, sin,
                                                 tm=512, rope=True)
    _, mkc, mvc, _, _, dkc, dvc, gates_c = _inproj(ctx, mod0, 8, g0, w_perm, l0_mlstm_conv_w, gate_b,
                                                   cos[:n_ctx], sin[:n_ctx], tm=n_ctx, rope=False)

    arow, cols = _gateprep(jnp.concatenate([gates_c, gates], axis=1), L=L)
    hm = _mlstm(mq, mk, mv, mo, mkc, mvc, arow, cols, l0_mlstm_norm_g, L=L)

    lam_init = 0.8 - 0.6 * math.exp(-0.3 * 0)
    lam_in = jnp.zeros((8, LANES), F32).at[0, :DIFF_DQK].set(l0_lambda_q1).at[1, :DIFF_DQK].set(l0_lambda_k1)
    lam_in = lam_in.at[2, :DIFF_DQK].set(l0_lambda_q2).at[3, :DIFF_DQK].set(l0_lambda_k2)
    hd = _attn(dq, jnp.concatenate([dkc, dk], axis=1), jnp.concatenate([dvc, dv], axis=1), lam_in,
               l0_diff_norm_g, tq=min(1024, t), sub=256, max_sub_per_step=11, lam_init=lam_init)

    w_out = l0_w_out.astype(BF16)
    x1, u2 = _outproj(hm, hd, w_out[:512], w_out[512:], x, mod0, l0_mix_post_g, l0_ffn_pre_g, tm=512)
    x2, h1 = _ffn(u2, l0_ffn_w1.astype(BF16), l0_ffn_w3.astype(BF16), l0_ffn_w2.astype(BF16), x1, mod0, mod1,
                  l0_ffn_post_g, l1_mix_pre_g, l1_conv_pw1_w.astype(BF16), l1_conv_pw1_b.reshape(1, -1), tm=512)

    dw = jnp.concatenate([l1_conv_dw_w, jnp.zeros((1, d), F32)], axis=0)
    rw = jnp.concatenate([l1_router_w, jnp.zeros((d, LANES - N_EXPERTS), F32)], axis=1).astype(BF16)
    v1 = lambda a: a.reshape(1, -1)
    conv_tm = MOE_TILE // 2
    x3, u4, route, routet, cnt_tile = _convmod(
        h1, dw, v1(l1_conv_dw_b), v1(l1_conv_ln_g),
        v1(l1_conv_ln_b), l1_conv_pw2_w.astype(BF16), v1(l1_conv_pw2_b), x2, mod1, v1(l1_mix_post_g),
        v1(l1_ffn_pre_g), rw, tm=conv_tm)

    n = b * t
    rows = MOE_ROWS
    nt = n // MOE_TILE
    cnt_half = cnt_tile[:, 0, :N_EXPERTS].astype(jnp.int32).reshape(nt, 2, N_EXPERTS)
    cnt = cnt_half[:, 0] + cnt_half[:, 1]
    aligned = ((cnt + MOE_ALIGN - 1) // MOE_ALIGN) * MOE_ALIGN
    base = jnp.cumsum(aligned, axis=0) - aligned
    cap = ((base[-1] + MOE_WIN + rows - 1) // rows) * rows
    pend = jnp.cumsum(cap)
    s0 = (pend - cap)[None, :] + base
    n_blk = (2 * n + N_EXPERTS * (MOE_ALIGN * nt + MOE_WIN + rows)) // rows + 1
    block_e = jnp.clip(jnp.searchsorted(pend, jnp.arange(n_blk, dtype=jnp.int32) * rows, side='right'),
                       0, N_EXPERTS - 1).astype(jnp.int32)
    n_used = (pend[-1:] // rows).astype(jnp.int32)
    blk_row = jnp.arange(n_blk, dtype=jnp.int32) * rows
    slot_end = (pend - cap) + base[-1] + aligned[-1]
    block_used = ((blk_row < slot_end[block_e]) & (blk_row < pend[-1])).astype(jnp.int32)
    flat = lambda a: a.reshape(-1).astype(jnp.int32)

    xs = _dispatch(u4.reshape(n, d), routet, flat(s0), flat(cnt_half[:, 0]), flat(cnt), flat(pend - MOE_WIN),
                   n_used, n_rows=n_blk * rows)
    ys = _moe_ffn(xs, block_e, block_used, l1_moe_w1.astype(BF16), l1_moe_w3.astype(BF16),
                  l1_moe_w2.astype(BF16), rows=rows)
    return _combine(flat(s0), flat(cnt_half[:, 0]), flat(cnt), ys, route, x3, mod1, l1_ffn_post_g)
```

```python
import functools
import math

import jax
import jax.numpy as jnp
from jax import lax
from jax.experimental import pallas as pl
from jax.experimental.pallas import tpu as pltpu

F32 = jnp.float32
BF16 = jnp.bfloat16
EPS = 1e-6
NEG = -1e30
LOG2E = 1.4426950408889634

D_MODEL = 1024
GRID_W = 64
MLSTM_HEADS = 8
MLSTM_D = 64
MLSTM_CHUNK = 256
DIFF_HEADS = 4
DIFF_DQK = 64
ROPE_BASE = 10000.0
CONV_WIDTH = 31
N_EXPERTS = 8
MOE_ROWS = 512
MOE_TILE = 512
MOE_WIN = 512
MOE_ALIGN = 16
LANES = 128
VMEM_LIMIT = 52 * 1024 * 1024


def _cp(*sem):
    return pltpu.CompilerParams(dimension_semantics=sem, vmem_limit_bytes=VMEM_LIMIT)


def _rms(x, g):
    return x * lax.rsqrt(jnp.mean(x * x, axis=-1, keepdims=True) + EPS) * g


def _silu(x):
    return x * jax.nn.sigmoid(x)


def _dot(a, b):
    return jnp.dot(a, b, preferred_element_type=F32)


def _dot_nt(a, b):
    return lax.dot_general(a, b, (((1,), (1,)), ((), ())), preferred_element_type=F32)


def _dot_tn(a, b):
    return lax.dot_general(a, b, (((0,), (0,)), ((), ())), preferred_element_type=F32)


def _adaln_kernel(c_ref, w_ref, b_ref, o_ref):
    s = _silu(c_ref[...])
    o_ref[...] = _dot(s.astype(BF16), w_ref[...].astype(BF16)) + b_ref[...]


def _adaln(cpad, w, b):
    rows, d = cpad.shape
    n = w.shape[1]
    tn = 1536
    return pl.pallas_call(
        _adaln_kernel,
        grid=(n // tn,),
        in_specs=[pl.BlockSpec((rows, d), lambda j: (0, 0)),
                  pl.BlockSpec((d, tn), lambda j: (0, j)),
                  pl.BlockSpec((1, tn), lambda j: (0, j))],
        out_specs=pl.BlockSpec((rows, tn), lambda j: (0, j)),
        out_shape=jax.ShapeDtypeStruct((rows, n), F32),
        compiler_params=_cp("arbitrary"),
        name="adaln",
    )(cpad, w, b.reshape(1, n))


def _inproj_kernel(x_ref, xp_ref, xn_ref, mod_ref, g_ref, w_ref, cw_ref, gb_ref, cos_ref, sin_ref,
                   mq_ref, mk_ref, mv_ref, mo_ref, dq_ref, dk_ref, dv_ref, gt_ref, pext_ref,
                   *, tm, nt, rope):
    i = pl.program_id(1)
    g = g_ref[...]
    sh = mod_ref[0:1, :]
    sc = mod_ref[1:2, :]

    def mod(xv):
        return _rms(xv, g) * (1.0 + sc) + sh

    u = mod(x_ref[...])
    up = jnp.where(i > 0, mod(xp_ref[...]), 0.0)
    un = jnp.where(i < nt - 1, mod(xn_ref[...]), 0.0)
    ub = u.astype(BF16)
    uext = jnp.concatenate([up.astype(BF16), ub, un.astype(BF16)], axis=0)

    pext_ref[...] = _dot(uext, w_ref[:, 0:1024])
    cw = cw_ref[...]
    conv = (cw[0:1, :] * pext_ref[pl.ds(7, tm), :] + cw[1:2, :] * pext_ref[pl.ds(8, tm), :]
            + cw[2:3, :] * pext_ref[pl.ds(9, tm), :])
    act = _silu(conv)
    mq_ref[...] = (act[:, 0:512] * (MLSTM_D ** -0.5)).astype(BF16)
    mk_ref[...] = act[:, 512:1024].astype(BF16)

    p = _dot(ub, w_ref[:, 1024:2048])
    mv_ref[...] = p[:, 0:512].astype(BF16)
    mo_ref[...] = p[:, 512:1024].astype(BF16)

    p = _dot(ub, w_ref[:, 2048:3072])
    if rope:
        lane = lax.broadcasted_iota(jnp.int32, p.shape, 1)
        first_half = ((lane // 16) % 2) == 0
        nl = p.shape[1]
        partner = jnp.where(first_half, pltpu.roll(p, nl - 16, 1), pltpu.roll(p, 16, 1))
        cos = jnp.concatenate([cos_ref[...]] * 8, axis=1)
        sin = jnp.concatenate([sin_ref[...]] * 8, axis=1)
        p = p * cos + partner * sin
    dq_ref[...] = (p[:, 0:512] * (LOG2E * DIFF_DQK ** -0.5)).astype(BF16)
    dk_ref[...] = p[:, 512:1024].astype(BF16)

    p = _dot(ub, w_ref[:, 3072:3712])
    dv_ref[...] = p[:, 0:512].astype(BF16)
    gt_ref[...] = p[:, 512:640] + gb_ref[...]


def _inproj(x, mod, mod_row, g, w_perm, conv_w, gate_b, cos, sin, *, tm, rope):
    b, t, d = x.shape
    nt = t // tm
    hb = tm // 8
    nb8 = t // 8
    bf = lambda: jax.ShapeDtypeStruct((b, t, 512), BF16)
    if mod_row is None:
        mod_map = lambda bi, i: (bi, 0, 0)
    else:
        mod_map = lambda bi, i: (mod_row, 0, 0)
    kern = functools.partial(_inproj_kernel, tm=tm, nt=nt, rope=rope)
    o512 = pl.BlockSpec((None, tm, 512), lambda bi, i: (bi, i, 0))
    return pl.pallas_call(
        kern,
        grid=(b, nt),
        in_specs=[
            pl.BlockSpec((None, tm, d), lambda bi, i: (bi, i, 0)),
            pl.BlockSpec((None, 8, d), lambda bi, i: (bi, jnp.maximum(i * hb - 1, 0), 0)),
            pl.BlockSpec((None, 8, d), lambda bi, i: (bi, jnp.minimum((i + 1) * hb, nb8 - 1), 0)),
            pl.BlockSpec((None, 6, d), mod_map),
            pl.BlockSpec((1, d), lambda bi, i: (0, 0)),
            pl.BlockSpec((d, 3712), lambda bi, i: (0, 0)),
            pl.BlockSpec((3, d), lambda bi, i: (0, 0)),
            pl.BlockSpec((1, LANES), lambda bi, i: (0, 0)),
            pl.BlockSpec((tm, LANES), lambda bi, i: (i, 0)),
            pl.BlockSpec((tm, LANES), lambda bi, i: (i, 0)),
        ],
        out_specs=[o512] * 7 + [pl.BlockSpec((None, tm, LANES), lambda bi, i: (bi, i, 0))],
        out_shape=[bf() for _ in range(7)] + [jax.ShapeDtypeStruct((b, t, LANES), F32)],
        scratch_shapes=[pltpu.VMEM((tm + 16, 1024), F32)],
        compiler_params=_cp("arbitrary", "arbitrary"),
        name="inproj_rope" if rope else "inproj_ctx",
    )(x, x, x, mod, g, w_perm, conv_w, gate_b, cos, sin)


def _gateprep_kernel(g_ref, arow_ref, col_ref, *, L, nch):
    tt = L * nch
    gt = g_ref[...].T
    i_f, f_f, i_b, f_b = gt[0:8], gt[8:16], gt[16:24], gt[24:32]

    def logsig(v):
        return jnp.minimum(v, 0.0) - jnp.log1p(jnp.exp(-jnp.abs(v)))

    pos = lax.broadcasted_iota(jnp.int32, (8, tt), 1) % L

    def scan(v, op, ident, reverse):
        s = 1
        while s < L:
            if reverse:
                shifted = pltpu.roll(v, tt - s, 1)
                valid = pos < L - s
            else:
                shifted = pltpu.roll(v, s, 1)
                valid = pos >= s
            v = op(v, jnp.where(valid, shifted, ident))
            s *= 2
        return v

    outs = []
    for d, (ig, fg) in enumerate(((i_f, f_f), (i_b, f_b))):
        rev = d == 1
        bcum = scan(logsig(fg), jnp.add, 0.0, rev)
        a = ig - bcum
        cm = scan(a, jnp.maximum, NEG, rev)
        order = list(range(nch)) if not rev else [0] + list(range(nch - 1, 0, -1))
        mp = jnp.zeros((8, 1), F32)
        mp_c = [None] * nch
        ml_c = [None] * nch
        for j in order:
            e = j * L if rev else j * L + L - 1
            mlast = jnp.maximum(mp, cm[:, e:e + 1])
            mp_c[j] = jnp.broadcast_to(mp, (8, L))
            ml_c[j] = jnp.broadcast_to(mlast, (8, L))
            mp = bcum[:, e:e + 1] + mlast
        mprev = jnp.concatenate(mp_c, axis=1)
        mlast = jnp.concatenate(ml_c, axis=1)
        m = jnp.maximum(mprev, cm)
        outs.append((a, m, jnp.exp(mprev - m), jnp.exp(a - mlast), jnp.exp(-(bcum + m))))

    arow_ref[...] = jnp.concatenate([outs[0][0], outs[1][0]], axis=0)
    rows = []
    for pair in range(4):
        for q in range(1, 5):
            for d in range(2):
                rows.append(outs[d][q][2 * pair:2 * pair + 2])
    rows.append(jnp.zeros((64, tt), F32))
    col_ref[...] = jnp.concatenate(rows, axis=0).T


def _gateprep(gall, *, L):
    b, tt, _ = gall.shape
    nch = tt // L
    return pl.pallas_call(
        functools.partial(_gateprep_kernel, L=L, nch=nch),
        grid=(b,),
        in_specs=[pl.BlockSpec((None, tt, LANES), lambda bi: (bi, 0, 0))],
        out_specs=[pl.BlockSpec((None, 16, tt), lambda bi: (bi, 0, 0)),
                   pl.BlockSpec((None, tt, LANES), lambda bi: (bi, 0, 0))],
        out_shape=[jax.ShapeDtypeStruct((b, 16, tt), F32), jax.ShapeDtypeStruct((b, tt, LANES), F32)],
        compiler_params=_cp("arbitrary"),
        name="mlstm_gateprep",
    )(gall)


def _mlstm_kernel(q_ref, k_ref, v_ref, o_ref, kc_ref, vc_ref, arow_ref, col_ref, ng_ref, out_ref,
                  c_ref, hf_ref, hb_ref, *, L, nc):
    p = pl.program_id(1)
    half = nc // 2
    lane = lax.broadcasted_iota(jnp.int32, (L, LANES), 1)
    lo = lane < MLSTM_D
    head_mask = (lo, jnp.logical_not(lo))
    ri = lax.broadcasted_iota(jnp.int32, (L, L), 0)
    ci = lax.broadcasted_iota(jnp.int32, (L, L), 1)
    causal = (ci <= ri, ci >= ri)
    ones_t = jnp.ones((L, LANES), BF16)
    shift = lax.rem(LANES - 16 * p, LANES)

    def cols(off):
        return pltpu.roll(col_ref[pl.ds(off, L), :], shift, 1)

    def col(blk, q, d, hh):
        j = q * 4 + d * 2 + hh
        return blk[:, j:j + 1]

    def vext_of(vb, hh):
        return jnp.concatenate([jnp.where(head_mask[hh], vb, jnp.zeros_like(vb)), ones_t], axis=1)

    def state_update(d, hh, kb, vext, blk, arow, dec):
        last = L - 1 if d == 0 else 0
        khm = jnp.where(head_mask[hh], kb, jnp.zeros_like(kb))
        ws = jnp.exp(arow - col(blk, 0, d, hh)[last:last + 1, :])
        ksc = (khm.T.astype(F32) * ws).astype(BF16)
        upd = _dot(ksc, vext)
        if dec is None:
            c_ref[d, hh] = upd
        else:
            c_ref[d, hh] = dec * c_ref[d, hh] + upd

    def arow_of(d, hh, off):
        return arow_ref[pl.ds(d * 8 + 2 * p + hh, 1), pl.ds(off, L)]

    blk0 = cols(0)
    kcb = kc_ref[...]
    vcb = vc_ref[...]
    for d in range(2):
        for hh in range(2):
            state_update(d, hh, kcb, vext_of(vcb, hh), blk0, arow_of(d, hh, 0), None)

    def compute(d, c):
        t0 = pl.multiple_of(c * L, L)
        off = pl.multiple_of(c * L + L, L)
        qb = q_ref[pl.ds(t0, L), :]
        kb = k_ref[pl.ds(t0, L), :]
        vb = v_ref[pl.ds(t0, L), :]
        blk = cols(off)
        last = L - 1 if d == 0 else 0
        hs = []
        for hh in range(2):
            arow = arow_of(d, hh, off)
            khm = jnp.where(head_mask[hh], kb, jnp.zeros_like(kb))
            s = _dot_nt(qb, khm)
            arg = jnp.where(causal[d], arow - col(blk, 0, d, hh), NEG)
            pm = (s * jnp.exp(arg)).astype(BF16)
            vext = vext_of(vb, hh)
            ch = c_ref[d, hh]
            tot = _dot(pm, vext) + col(blk, 1, d, hh) * _dot(qb, ch.astype(BF16))
            den = jnp.maximum(jnp.abs(tot[:, LANES:]), col(blk, 3, d, hh))
            hs.append(tot[:, :LANES] / den)
            dec = col(blk, 1, d, hh)[last:last + 1, :]
            state_update(d, hh, kb, vext, blk, arow, dec)
        return jnp.where(lo, hs[0], hs[1])

    def finalize(c, hsum):
        t0 = pl.multiple_of(c * L, L)
        sq = hsum * hsum
        s0 = jnp.sum(jnp.where(lo, sq, 0.0), axis=-1, keepdims=True)
        s1 = jnp.sum(jnp.where(lo, 0.0, sq), axis=-1, keepdims=True)
        ms = jnp.where(lo, s0, s1) * (1.0 / MLSTM_D)
        y = hsum * lax.rsqrt(ms + EPS) * ng_ref[...]
        gate = jax.nn.sigmoid(o_ref[pl.ds(t0, L), :].astype(F32))
        out_ref[pl.ds(t0, L), :] = (y * gate).astype(BF16)

    def phase_a(i, carry):
        hf_ref[pl.ds(pl.multiple_of(i * L, L), L), :] = compute(0, i)
        cb = nc - 1 - i
        hb_ref[pl.ds(pl.multiple_of((cb - half) * L, L), L), :] = compute(1, cb)
        return carry

    def phase_b(i, carry):
        hf = compute(0, i)
        finalize(i, hf + hb_ref[pl.ds(pl.multiple_of((i - half) * L, L), L), :])
        cb = nc - 1 - i
        hb = compute(1, cb)
        finalize(cb, hb + hf_ref[pl.ds(pl.multiple_of(cb * L, L), L), :])
        return carry

    lax.fori_loop(0, half, phase_a, 0)
    lax.fori_loop(half, nc, phase_b, 0)


def _mlstm(mq, mk, mv, mo, mkc, mvc, arow, cols, norm_g, *, L):
    b, t, _ = mq.shape
    ctx = mkc.shape[1]
    tt = arow.shape[2]
    nc = t // L
    tok = pl.BlockSpec((None, t, LANES), lambda bi, p: (bi, 0, p))
    ctxs = pl.BlockSpec((None, ctx, LANES), lambda bi, p: (bi, 0, p))
    return pl.pallas_call(
        functools.partial(_mlstm_kernel, L=L, nc=nc),
        grid=(b, 4),
        in_specs=[tok, tok, tok, tok, ctxs, ctxs,
                  pl.BlockSpec((None, 16, tt), lambda bi, p: (bi, 0, 0)),
                  pl.BlockSpec((None, tt, LANES), lambda bi, p: (bi, 0, 0)),
                  pl.BlockSpec((1, LANES), lambda bi, p: (0, p))],
        out_specs=tok,
        out_shape=jax.ShapeDtypeStruct((b, t, 512), BF16),
        scratch_shapes=[pltpu.VMEM((2, 2, LANES, 2 * LANES), F32),
                        pltpu.VMEM((t // 2, LANES), F32),
                        pltpu.VMEM((t // 2, LANES), F32)],
        compiler_params=_cp("arbitrary", "arbitrary"),
        name="mlstm_scan",
    )(mq, mk, mv, mo, mkc, mvc, arow, cols, norm_g.reshape(1, 512))


def _attn_kernel(q_ref, k_ref, v_ref, lam_ref, ng_ref, out_ref, kmax_ref, m_ref, acc_ref,
                 *, tq, kb, nkb, sub, lam_init):
    i = pl.program_id(2)
    lane = lax.broadcasted_iota(jnp.int32, (tq, LANES), 1)
    lo = lane < DIFF_DQK
    rr = lax.broadcasted_iota(jnp.int32, (LANES, LANES), 0)
    cc = lax.broadcasted_iota(jnp.int32, (LANES, LANES), 1)
    same_comp = ((rr < DIFF_DQK) == (cc < DIFF_DQK)).astype(BF16)

    def comp_sqnorm(a):
        af = a.astype(F32)
        return _dot((af * af).astype(BF16), same_comp)

    @pl.when(i == 0)
    def _():
        mx = jnp.zeros((1, LANES), F32)

        def kbody(j, mx):
            k0 = pl.multiple_of(j * kb, kb)
            return jnp.maximum(mx, jnp.max(comp_sqnorm(k_ref[pl.ds(k0, kb), :]), axis=0, keepdims=True))

        kmax_ref[...] = lax.fori_loop(0, nkb, kbody, mx)

    q = q_ref[...]
    zq = jnp.zeros_like(q)
    bnd = jnp.sqrt(comp_sqnorm(q) * kmax_ref[...]) * 1.02
    b1 = jnp.where(lane == 0, -bnd, 0.0)
    b2 = jnp.where(lane == 0, -pltpu.roll(bnd, DIFF_DQK, 1), 0.0)
    qs = jnp.concatenate(
        [jnp.concatenate([jnp.where(lo, q, zq), jnp.where(lo, zq, q)], axis=0),
         jnp.concatenate([b1, b2], axis=0).astype(BF16)], axis=1)
    acc_ref[...] = jnp.zeros(acc_ref.shape, F32)

    def ext(blk):
        return jnp.concatenate([blk, jnp.ones_like(blk)], axis=1)

    def run(step, blk):
        def body(j, carry):
            k0 = pl.multiple_of(j * blk, blk)
            step(k_ref[pl.ds(k0, blk), :], v_ref[pl.ds(k0, blk), :])
            return carry

        lax.fori_loop(0, (nkb * kb) // blk, body, 0)

    def fast_step(kblk, vblk):
        tot = None
        for c in range(0, kblk.shape[0], sub):
            pm = jnp.exp2(_dot_nt(qs, ext(kblk[c:c + sub]))).astype(BF16)
            part = _dot(pm, ext(vblk[c:c + sub]))
            tot = part if tot is None else tot + part
        acc_ref[...] += tot

    def slow_step(kblk, vblk):
        s = _dot_nt(qs, ext(kblk))
        m_old = m_ref[...]
        m_new = jnp.maximum(m_old, jnp.max(s, axis=-1, keepdims=True))
        pm = jnp.exp2(s - m_new).astype(BF16)
        acc_ref[...] = jnp.exp2(m_old - m_new) * acc_ref[...] + _dot(pm, ext(vblk))
        m_ref[...] = m_new

    fast = jnp.max(bnd) <= 56.0

    @pl.when(fast)
    def _():
        run(fast_step, kb)

    @pl.when(jnp.logical_not(fast))
    def _():
        m_ref[...] = jnp.full(m_ref.shape, NEG, F32)
        run(slow_step, sub)

    lq = lam_ref[...]
    lam = (jnp.exp(jnp.sum(lq[0:1, :] * lq[1:2, :], axis=-1, keepdims=True))
           - jnp.exp(jnp.sum(lq[2:3, :] * lq[3:4, :], axis=-1, keepdims=True)) + lam_init)
    a1 = acc_ref[0:tq, :]
    a2 = acc_ref[tq:2 * tq, :]
    o = a1[:, :LANES] / a1[:, LANES:] - lam * (a2[:, :LANES] / a2[:, LANES:])
    out_ref[...] = (_rms(o, ng_ref[...]) * (1.0 - lam_init)).astype(BF16)


def _attn(dq, dk, dv, lam_in, norm_g, *, tq, sub, max_sub_per_step, lam_init):
    b, t, _ = dq.shape
    tk = dk.shape[1]
    nsub = tk // sub
    per = max(g for g in range(1, max_sub_per_step + 1) if nsub % g == 0)
    kb = per * sub
    full = pl.BlockSpec((None, tk, LANES), lambda bi, h, i: (bi, 0, h))
    qs = pl.BlockSpec((None, tq, LANES), lambda bi, h, i: (bi, i, h))
    return pl.pallas_call(
        functools.partial(_attn_kernel, tq=tq, kb=kb, nkb=tk // kb, sub=sub, lam_init=lam_init),
        grid=(b, DIFF_HEADS, t // tq),
        in_specs=[qs, full, full,
                  pl.BlockSpec((8, LANES), lambda bi, h, i: (0, 0)),
                  pl.BlockSpec((1, LANES), lambda bi, h, i: (0, h))],
        out_specs=qs,
        out_shape=jax.ShapeDtypeStruct((b, t, 512), BF16),
        scratch_shapes=[pltpu.VMEM((1, LANES), F32), pltpu.VMEM((2 * tq, 1), F32),
                        pltpu.VMEM((2 * tq, 2 * LANES), F32)],
        compiler_params=_cp("arbitrary", "arbitrary", "arbitrary"),
        name="diff_attn",
    )(dq, dk, dv, lam_in, norm_g.reshape(1, 512))


def _outproj_kernel(hm_ref, hd_ref, wt_ref, wb_ref, x_ref, mod_ref, pg_ref, fg_ref, x1_ref, u_ref):
    mix = _dot(hm_ref[...], wt_ref[...]) + _dot(hd_ref[...], wb_ref[...])
    x1 = x_ref[...] + mod_ref[2:3, :] * _rms(mix, pg_ref[...])
    x1_ref[...] = x1
    u_ref[...] = (_rms(x1, fg_ref[...]) * (1.0 + mod_ref[4:5, :]) + mod_ref[3:4, :]).astype(BF16)


def _outproj(hm, hd, wt, wb, x, mod, post_g, ffn_pre_g, *, tm):
    b, t, d = x.shape
    row = lambda w: pl.BlockSpec((None, tm, w), lambda bi, i: (bi, i, 0))
    cst = lambda s: pl.BlockSpec(s, lambda bi, i: (0,) * len(s))
    return pl.pallas_call(
        _outproj_kernel,
        grid=(b, t // tm),
        in_specs=[row(512), row(512), cst((512, d)), cst((512, d)), row(d),
                  pl.BlockSpec((None, 6, d), lambda bi, i: (bi, 0, 0)), cst((1, d)), cst((1, d))],
        out_specs=[row(d), row(d)],
        out_shape=[jax.ShapeDtypeStruct((b, t, d), F32), jax.ShapeDtypeStruct((b, t, d), BF16)],
        compiler_params=_cp("arbitrary", "arbitrary"),
        name="outproj",
    )(hm, hd, wt, wb, x, mod, post_g.reshape(1, d), ffn_pre_g.reshape(1, d))


def _swiglu(u, w1_ref, w3_ref, w2_ref):
    return _dot((_silu(_dot(u, w1_ref[...])) * _dot(u, w3_ref[...])).astype(BF16), w2_ref[...])


def _ffn_kernel(u_ref, w1_ref, w3_ref, w2_ref, x_ref, mod0_ref, mod1_ref, pg_ref, ng_ref, pw_ref, pb_ref,
                x2_ref, h_ref):
    y = _swiglu(u_ref[...], w1_ref, w3_ref, w2_ref)
    x2 = x_ref[...] + mod0_ref[5:6, :] * _rms(y, pg_ref[...])
    x2_ref[...] = x2
    u3 = (_rms(x2, ng_ref[...]) * (1.0 + mod1_ref[1:2, :]) + mod1_ref[0:1, :]).astype(BF16)
    ag = _dot(u3, pw_ref[...]) + pb_ref[...]
    h_ref[...] = ag[:, :D_MODEL] * jax.nn.sigmoid(ag[:, D_MODEL:])


def _ffn(u, w1, w3, w2, x1, mod0, mod1, post_g, next_pre_g, pw1, pb1, *, tm):
    b, t, d = x1.shape
    f = w1.shape[1]
    row = pl.BlockSpec((None, tm, d), lambda bi, i: (bi, i, 0))
    modb = pl.BlockSpec((None, 6, d), lambda bi, i: (bi, 0, 0))
    vec = lambda w: pl.BlockSpec((1, w), lambda bi, i: (0, 0))
    wspec = lambda s: pl.BlockSpec(s, lambda bi, i: (0, 0), pipeline_mode=pl.Buffered(1))
    return pl.pallas_call(
        _ffn_kernel,
        grid=(b, t // tm),
        in_specs=[row, wspec((d, f)), wspec((d, f)), wspec((f, d)), row, modb, modb, vec(d), vec(d),
                  wspec((d, 2 * d)), vec(2 * d)],
        out_specs=[row, row],
        out_shape=[jax.ShapeDtypeStruct((b, t, d), F32), jax.ShapeDtypeStruct((b, t, d), F32)],
        compiler_params=_cp("arbitrary", "arbitrary"),
        name="ffn_swiglu_glu",
    )(u, w1, w3, w2, x1, mod0, mod1, post_g.reshape(1, d), next_pre_g.reshape(1, d), pw1, pb1)


HALO = 16
CONV_ROW_BLOCK = 64
CONV_LANES = 512
CONV_ROWS_EXTRA = 24


def _conv_kernel(h_ref, hp_ref, hn_ref, dw_ref, dwb_ref, lng_ref, lnb_ref, w2_ref, b2_ref,
                 x_ref, mod_ref, pg_ref, fg_ref, rw_ref, x3_ref, u4_ref, route_ref, routet_ref, cnt_ref,
                 hs_ref, sh_ref, cv_ref, wb_ref, *, tm, nt):
    i = pl.program_id(1)

    hs_ref[0:HALO, :] = jnp.where(i > 0, hp_ref[...], 0.0)
    hs_ref[HALO:HALO + tm, :] = h_ref[...]
    hs_ref[HALO + tm:, :] = jnp.where(i < nt - 1, hn_ref[...], 0.0)

    for r in range(1, 8):
        sh_ref[r - 1] = hs_ref[pl.ds(r, tm + CONV_ROWS_EXTRA), :]

    for j in range(CONV_WIDTH):
        wb_ref[j] = jnp.broadcast_to(dw_ref[j:j + 1, :], (8, D_MODEL))
    nsub = CONV_ROW_BLOCK // 8

    for l0 in range(0, D_MODEL, CONV_LANES):
        def conv_rows(rb, carry, l0=l0):
            r0 = pl.multiple_of(rb * CONV_ROW_BLOCK, CONV_ROW_BLOCK)
            acc = jnp.broadcast_to(dwb_ref[:, l0:l0 + CONV_LANES].reshape(1, 1, CONV_LANES),
                                   (nsub, 8, CONV_LANES))
            for j in range(CONV_WIDTH):
                r, a = (j + 1) % 8, (j + 1) // 8
                if r == 0:
                    win = hs_ref[pl.ds(r0 + 8 * a, CONV_ROW_BLOCK), l0:l0 + CONV_LANES]
                else:
                    win = sh_ref[r - 1, pl.ds(r0 + 8 * a, CONV_ROW_BLOCK), l0:l0 + CONV_LANES]
                acc = acc + wb_ref[j, :, l0:l0 + CONV_LANES][None] * win.reshape(nsub, 8, CONV_LANES)
            cv_ref[pl.ds(r0, CONV_ROW_BLOCK), l0:l0 + CONV_LANES] = acc.reshape(CONV_ROW_BLOCK, CONV_LANES)
            return carry

        lax.fori_loop(0, tm // CONV_ROW_BLOCK, conv_rows, 0)
    acc = cv_ref[...]
    mu = jnp.mean(acc, axis=-1, keepdims=True)
    cen = acc - mu
    var = jnp.mean(cen * cen, axis=-1, keepdims=True)
    hn = _silu(cen * lax.rsqrt(var + EPS) * lng_ref[...] + lnb_ref[...])
    y = _dot(hn.astype(BF16), w2_ref[...]) + b2_ref[...]
    x3 = x_ref[...] + mod_ref[2:3, :] * _rms(y, pg_ref[...])
    x3_ref[...] = x3
    u4 = _rms(x3, fg_ref[...]) * (1.0 + mod_ref[4:5, :]) + mod_ref[3:4, :]
    u4b = u4.astype(BF16)
    u4_ref[...] = u4b

    lane = lax.broadcasted_iota(jnp.int32, (tm, LANES), 1).astype(F32)
    logits = jnp.where(lane < N_EXPERTS, _dot(u4b, rw_ref[...]), NEG)
    m1 = jnp.max(logits, axis=-1, keepdims=True)
    i1 = jnp.min(jnp.where(logits == m1, lane, float(LANES)), axis=-1, keepdims=True)
    l2 = jnp.where(lane == i1, NEG, logits)
    m2 = jnp.max(l2, axis=-1, keepdims=True)
    i2 = jnp.min(jnp.where(l2 == m2, lane, float(LANES)), axis=-1, keepdims=True)
    e21 = jnp.exp(m2 - m1)
    g1 = 1.0 / (1.0 + e21)
    g2 = e21 * g1
    sel = ((lane == i1) | (lane == i2)).astype(F32)
    ri = lax.broadcasted_iota(jnp.int32, (tm, tm), 0)
    ci = lax.broadcasted_iota(jnp.int32, (tm, tm), 1)
    tri = (ci <= ri).astype(BF16)
    csum = _dot(tri, sel.astype(BF16))
    rank = csum - sel
    r1 = jnp.sum(jnp.where(lane == i1, rank, 0.0), axis=-1, keepdims=True)
    r2 = jnp.sum(jnp.where(lane == i2, rank, 0.0), axis=-1, keepdims=True)
    cnt_ref[...] = jnp.broadcast_to(csum[tm - 1:tm, :], (8, LANES))
    route = jnp.zeros((tm, LANES), F32)
    for n, v in enumerate((i1, i2, g1, g2, r1, r2)):
        route = jnp.where(lane == float(n), v, route)
    routet_ref[...] = route.T[0:8, :]
    gsel = jnp.where(lane == i1, g1, jnp.where(lane == i2, g2, 0.0))
    ghi = gsel.astype(BF16)
    glo = (gsel - ghi.astype(F32)).astype(BF16)
    route_ref[...] = jnp.concatenate([jnp.where(sel > 0.0, rank, -1.0).astype(BF16), ghi, glo], axis=1)


def _convmod(h, dw, dwb, lng, lnb, w2, b2, x2, mod1, post_g, ffn_pre_g, rw, *, tm):
    b, t, d = x2.shape
    nt = t // tm
    hb = tm // HALO
    nbh = t // HALO
    row = lambda w: pl.BlockSpec((None, tm, w), lambda bi, i: (bi, i, 0))
    cst = lambda s: pl.BlockSpec(s, lambda bi, i: (0,) * len(s))
    return pl.pallas_call(
        functools.partial(_conv_kernel, tm=tm, nt=nt),
        grid=(b, nt),
        in_specs=[row(d),
                  pl.BlockSpec((None, HALO, d), lambda bi, i: (bi, jnp.maximum(i * hb - 1, 0), 0)),
                  pl.BlockSpec((None, HALO, d), lambda bi, i: (bi, jnp.minimum((i + 1) * hb, nbh - 1), 0)),
                  cst((32, d)), cst((1, d)), cst((1, d)), cst((1, d)),
                  cst((d, d)), cst((1, d)), row(d),
                  pl.BlockSpec((None, 6, d), lambda bi, i: (bi, 0, 0)), cst((1, d)), cst((1, d)),
                  cst((d, LANES))],
        out_specs=[row(d), row(d), row(3 * LANES),
                   pl.BlockSpec((None, 8, tm), lambda bi, i: (bi * nt + i, 0, 0)),
                   pl.BlockSpec((None, 8, LANES), lambda bi, i: (bi * nt + i, 0, 0))],
        out_shape=[jax.ShapeDtypeStruct((b, t, d), F32), jax.ShapeDtypeStruct((b, t, d), BF16),
                   jax.ShapeDtypeStruct((b, t, 3 * LANES), BF16), jax.ShapeDtypeStruct((b * nt, 8, tm), F32),
                   jax.ShapeDtypeStruct((b * nt, 8, LANES), F32)],
        scratch_shapes=[pltpu.VMEM((tm + 2 * HALO, d), F32), pltpu.VMEM((7, tm + CONV_ROWS_EXTRA, d), F32),
                        pltpu.VMEM((tm, d), F32), pltpu.VMEM((CONV_WIDTH, 8, d), F32)],
        compiler_params=_cp("arbitrary", "arbitrary"),
        name="conv_module_router",
    )(h, h, h, dw, dwb, lng, lnb, w2, b2, x2, mod1, post_g, ffn_pre_g, rw)


def _window_copies(ref_hbm, buf_ref, sem, s0_ref, t, slot, to_hbm):
    out = []
    for e in range(N_EXPERTS):
        hbm = ref_hbm.at[pl.ds(pl.multiple_of(s0_ref[t * N_EXPERTS + e], MOE_ALIGN), MOE_WIN), :]
        vmem = buf_ref.at[slot, e]
        src, dst = (vmem, hbm) if to_hbm else (hbm, vmem)
        out.append(pltpu.make_async_copy(src, dst, sem.at[slot, e]))
    return out


def _dispatch_kernel(s0_ref, hoff_ref, cnt_ref, zoff_ref, nu_ref, u_ref, rt_ref, xs_ref, win_ref, zero_ref,
                     sem, zsem, *, nt, n_blk):
    t = pl.program_id(0)
    slot = lax.rem(t, 2)
    half = MOE_TILE // 2

    @pl.when(t == 0)
    def _():
        zero_ref[...] = jnp.zeros(zero_ref.shape, BF16)
        zc = [pltpu.make_async_copy(
            zero_ref, xs_ref.at[pl.ds(pl.multiple_of(zoff_ref[e], MOE_ALIGN), MOE_WIN), :], zsem.at[e])
            for e in range(N_EXPERTS)]
        for c in zc:
            c.start()
        for c in zc:
            c.wait()

        def clear_block(bk, carry):
            c = pltpu.make_async_copy(
                zero_ref, xs_ref.at[pl.ds(pl.multiple_of(bk * MOE_ROWS, MOE_ROWS), MOE_ROWS), :], zsem.at[0])
            c.start()
            c.wait()
            return carry

        lax.fori_loop(nu_ref[0], n_blk, clear_block, 0)

    rt = jnp.concatenate([rt_ref[0], rt_ref[1]], axis=1)
    e1, e2, r1, r2 = rt[0:1], rt[1:2], rt[4:5], rt[5:6]
    second = lax.broadcasted_iota(jnp.int32, (1, MOE_TILE), 1) >= half
    srow = lax.broadcasted_iota(jnp.int32, (half, MOE_TILE), 0).astype(F32)
    u = u_ref[...]
    for e in range(N_EXPERTS):
        off = jnp.where(second, hoff_ref[t * N_EXPERTS + e].astype(F32), 0.0)
        lr = jnp.where(e1 == float(e), r1 + off, jnp.where(e2 == float(e), r2 + off, -1.0))
        win_ref[slot, e, 0:half, :] = _dot((srow == lr).astype(BF16), u).astype(BF16)
        big = cnt_ref[t * N_EXPERTS + e] > half

        @pl.when(big)
        def _():
            win_ref[slot, e, half:MOE_WIN, :] = _dot((srow + float(half) == lr).astype(BF16), u).astype(BF16)

        @pl.when(jnp.logical_not(big))
        def _():
            win_ref[slot, e, half:MOE_WIN, :] = jnp.zeros((half, D_MODEL), BF16)

    @pl.when(t > 0)
    def _():
        for c in _window_copies(xs_ref, win_ref, sem, s0_ref, t - 1, 1 - slot, True):
            c.wait()

    for c in _window_copies(xs_ref, win_ref, sem, s0_ref, t, slot, True):
        c.start()

    @pl.when(t == nt - 1)
    def _():
        for c in _window_copies(xs_ref, win_ref, sem, s0_ref, t, slot, True):
            c.wait()


def _dispatch(u4, routet, s0, hoff, cnt, zoff, n_used, *, n_rows):
    assert MOE_WIN == MOE_ROWS
    n, d = u4.shape
    nt = n // MOE_TILE
    per = MOE_TILE // routet.shape[2]
    grid_spec = pltpu.PrefetchScalarGridSpec(
        num_scalar_prefetch=5,
        grid=(nt,),
        in_specs=[pl.BlockSpec((MOE_TILE, d), lambda t, *_: (t, 0)),
                  pl.BlockSpec((per, 8, routet.shape[2]), lambda t, *_: (t, 0, 0))],
        out_specs=pl.BlockSpec(memory_space=pl.ANY),
        scratch_shapes=[pltpu.VMEM((2, N_EXPERTS, MOE_WIN, d), BF16), pltpu.VMEM((MOE_WIN, d), BF16),
                        pltpu.SemaphoreType.DMA((2, N_EXPERTS)), pltpu.SemaphoreType.DMA((N_EXPERTS,))],
    )
    return pl.pallas_call(
        functools.partial(_dispatch_kernel, nt=nt, n_blk=n_rows // MOE_ROWS),
        grid_spec=grid_spec,
        out_shape=jax.ShapeDtypeStruct((n_rows, d), BF16),
        compiler_params=_cp("arbitrary"),
        name="moe_dispatch",
    )(s0, hoff, cnt, zoff, n_used, u4, routet)


def _moe_kernel(be_ref, used_ref, x_ref, w1_ref, w3_ref, w2_ref, y_ref):
    used = used_ref[pl.program_id(0)] > 0

    @pl.when(used)
    def _():
        y_ref[...] = _swiglu(x_ref[...], w1_ref, w3_ref, w2_ref).astype(BF16)

    @pl.when(jnp.logical_not(used))
    def _():
        y_ref[...] = jnp.zeros(y_ref.shape, BF16)


def _moe_ffn(xs, block_e, block_used, w1, w3, w2, *, rows):
    n_rows, d = xs.shape
    f = w1.shape[2]
    wspec = lambda s: pl.BlockSpec((None,) + s, lambda i, be, bu: (be[i], 0, 0), pipeline_mode=pl.Buffered(1))
    grid_spec = pltpu.PrefetchScalarGridSpec(
        num_scalar_prefetch=2,
        grid=(n_rows // rows,),
        in_specs=[pl.BlockSpec((rows, d), lambda i, be, bu: (i, 0)),
                  wspec((d, f)), wspec((d, f)), wspec((f, d))],
        out_specs=pl.BlockSpec((rows, d), lambda i, be, bu: (i, 0)),
    )
    return pl.pallas_call(
        _moe_kernel,
        grid_spec=grid_spec,
        out_shape=jax.ShapeDtypeStruct((n_rows, d), BF16),
        compiler_params=_cp("arbitrary"),
        name="moe_ffn",
    )(block_e, block_used, xs, w1, w3, w2)


def _combine_kernel(s0_ref, hoff_ref, cnt_ref, ys_ref, route_ref, x_ref, mod_ref, pg_ref, out_ref,
                    buf_ref, acc_ref, sem, *, nti, nt):
    t = pl.program_id(0) * nti + pl.program_id(1)
    slot = lax.rem(t, 2)
    half = MOE_TILE // 2

    @pl.when(t == 0)
    def _():
        for c in _window_copies(ys_ref, buf_ref, sem, s0_ref, 0, 0, False):
            c.start()

    @pl.when(t + 1 < nt)
    def _():
        for c in _window_copies(ys_ref, buf_ref, sem, s0_ref, t + 1, 1 - slot, False):
            c.start()

    for c in _window_copies(ys_ref, buf_ref, sem, s0_ref, t, slot, False):
        c.wait()

    route = route_ref[...]
    sr = lax.broadcasted_iota(jnp.int32, (LANES, N_EXPERTS * LANES), 0)
    sc = lax.broadcasted_iota(jnp.int32, (LANES, N_EXPERTS * LANES), 1)
    spread = (sc // LANES == sr).astype(BF16)
    rank_b = _dot(route[:, 0:LANES], spread)
    gate_b = _dot(route[:, LANES:2 * LANES], spread) + _dot(route[:, 2 * LANES:3 * LANES], spread)
    second = lax.broadcasted_iota(jnp.int32, (MOE_TILE, LANES), 0) >= half
    scol = lax.broadcasted_iota(jnp.int32, (MOE_TILE, LANES), 1).astype(F32)

    def onehot(lr, first):
        return jnp.concatenate([(scol + float(first + c) == lr) for c in range(0, half, LANES)],
                               axis=1).astype(BF16)

    for e in range(N_EXPERTS):
        rk = rank_b[:, e * LANES:(e + 1) * LANES]
        off = jnp.where(second, hoff_ref[t * N_EXPERTS + e].astype(F32), 0.0)
        lr = jnp.where(rk < 0.0, -1.0, rk + off)
        ge = jnp.concatenate([gate_b[:, e * LANES:(e + 1) * LANES]] * (D_MODEL // LANES), axis=1)
        z = ge * _dot(onehot(lr, 0), buf_ref[slot, e, 0:half, :])
        if e == 0:
            acc_ref[...] = z
        else:
            acc_ref[...] += z

        @pl.when(cnt_ref[t * N_EXPERTS + e] > half)
        def _():
            acc_ref[...] += ge * _dot(onehot(lr, half), buf_ref[slot, e, half:MOE_WIN, :])

    out_ref[...] = x_ref[...] + mod_ref[5:6, :] * _rms(acc_ref[...], pg_ref[...])


def _combine(s0, hoff, cnt, ys, route, x3, mod1, post_g):
    b, t, d = x3.shape
    nti = t // MOE_TILE
    row = lambda w: pl.BlockSpec((None, MOE_TILE, w), lambda bi, i, *_: (bi, i, 0))
    grid_spec = pltpu.PrefetchScalarGridSpec(
        num_scalar_prefetch=3,
        grid=(b, nti),
        in_specs=[pl.BlockSpec(memory_space=pl.ANY), row(3 * LANES), row(d),
                  pl.BlockSpec((None, 6, d), lambda bi, i, *_: (bi, 0, 0)),
                  pl.BlockSpec((1, d), lambda bi, i, *_: (0, 0))],
        out_specs=row(d),
        scratch_shapes=[pltpu.VMEM((2, N_EXPERTS, MOE_WIN, d), BF16), pltpu.VMEM((MOE_TILE, d), F32),
                        pltpu.SemaphoreType.DMA((2, N_EXPERTS))],
    )
    return pl.pallas_call(
        functools.partial(_combine_kernel, nti=nti, nt=b * nti),
        grid_spec=grid_spec,
        out_shape=jax.ShapeDtypeStruct((b, t, d), F32),
        compiler_params=_cp("arbitrary", "arbitrary"),
        name="moe_combine",
    )(s0, hoff, cnt, ys, route, x3, mod1, post_g.reshape(1, d))


def _rope_tables(t_len):
    rows = t_len // GRID_W
    row = jnp.repeat(jnp.arange(rows, dtype=F32), GRID_W)
    col = jnp.tile(jnp.arange(GRID_W, dtype=F32), rows)
    axis_dim = DIFF_DQK // 2
    inv = ROPE_BASE ** (-jnp.arange(0, axis_dim, 2, dtype=F32) / axis_dim)
    ang_r = row[:, None] * inv
    ang_c = col[:, None] * inv
    cr, sr, cc, sc = jnp.cos(ang_r), jnp.sin(ang_r), jnp.cos(ang_c), jnp.sin(ang_c)
    cos = jnp.concatenate([cr, cr, cc, cc] * 2, axis=1)
    sin = jnp.concatenate([-sr, sr, -sc, sc] * 2, axis=1)
    return cos, sin


def kernel(x, c, ctx, c_ctx, l0_mod_w, l0_mod_b, l0_mix_pre_g, l0_mix_post_g, l0_w_in, l0_mlstm_gate_b, l0_mlstm_conv_w, l0_mlstm_norm_g, l0_lambda_q1, l0_lambda_k1, l0_lambda_q2, l0_lambda_k2, l0_diff_norm_g, l0_w_out, l0_ffn_pre_g, l0_ffn_post_g, l0_ffn_w1, l0_ffn_w3, l0_ffn_w2, l1_mod_w, l1_mod_b, l1_mix_pre_g, l1_mix_post_g, l1_conv_pw1_w, l1_conv_pw1_b, l1_conv_dw_w, l1_conv_dw_b, l1_conv_ln_g, l1_conv_ln_b, l1_conv_pw2_w, l1_conv_pw2_b, l1_ffn_pre_g, l1_ffn_post_g, l1_router_w, l1_moe_w1, l1_moe_w3, l1_moe_w2):
    b, t, d = x.shape
    n_ctx = ctx.shape[1]
    L = MLSTM_CHUNK
    assert d == D_MODEL and n_ctx == L and t % (2 * L) == 0 and b <= 8

    cpad = jnp.zeros((16, d), F32).at[:b].set(c).at[8].set(c_ctx)
    mod0 = _adaln(cpad, l0_mod_w, l0_mod_b).reshape(16, 6, d)
    mod1 = _adaln(cpad, l1_mod_w, l1_mod_b).reshape(16, 6, d)

    w_perm = jnp.concatenate(
        [l0_w_in[:, :2048], l0_w_in[:, 2080:], l0_w_in[:, 2048:2080], jnp.zeros((d, 96), F32)], axis=1).astype(BF16)
    gate_b = jnp.concatenate([l0_mlstm_gate_b, jnp.zeros((96,), F32)]).reshape(1, LANES)
    cos, sin = _rope_tables(t)
    g0 = l0_mix_pre_g.reshape(1, d)
    mq, mk, mv, mo, dq, dk, dv, gates = _inproj(x, mod0, None, g0, w_perm, l0_mlstm_conv_w, gate_b, cos, sin,
                                                 tm=512, rope=True)
    _, mkc, mvc, _, _, dkc, dvc, gates_c = _inproj(ctx, mod0, 8, g0, w_perm, l0_mlstm_conv_w, gate_b,
                                                   cos[:n_ctx], sin[:n_ctx], tm=n_ctx, rope=False)

    arow, cols = _gateprep(jnp.concatenate([gates_c, gates], axis=1), L=L)
    hm = _mlstm(mq, mk, mv, mo, mkc, mvc, arow, cols, l0_mlstm_norm_g, L=L)

    lam_init = 0.8 - 0.6 * math.exp(-0.3 * 0)
    lam_in = jnp.zeros((8, LANES), F32).at[0, :DIFF_DQK].set(l0_lambda_q1).at[1, :DIFF_DQK].set(l0_lambda_k1)
    lam_in = lam_in.at[2, :DIFF_DQK].set(l0_lambda_q2).at[3, :DIFF_DQK].set(l0_lambda_k2)
    hd = _attn(dq, jnp.concatenate([dkc, dk], axis=1), jnp.concatenate([dvc, dv], axis=1), lam_in,
               l0_diff_norm_g, tq=min(1024, t), sub=256, max_sub_per_step=33, lam_init=lam_init)

    w_out = l0_w_out.astype(BF16)
    x1, u2 = _outproj(hm, hd, w_out[:512], w_out[512:], x, mod0, l0_mix_post_g, l0_ffn_pre_g, tm=512)
    x2, h1 = _ffn(u2, l0_ffn_w1.astype(BF16), l0_ffn_w3.astype(BF16), l0_ffn_w2.astype(BF16), x1, mod0, mod1,
                  l0_ffn_post_g, l1_mix_pre_g, l1_conv_pw1_w.astype(BF16), l1_conv_pw1_b.reshape(1, -1), tm=512)

    dw = jnp.concatenate([l1_conv_dw_w, jnp.zeros((1, d), F32)], axis=0)
    rw = jnp.concatenate([l1_router_w, jnp.zeros((d, LANES - N_EXPERTS), F32)], axis=1).astype(BF16)
    v1 = lambda a: a.reshape(1, -1)
    conv_tm = MOE_TILE // 2
    x3, u4, route, routet, cnt_tile = _convmod(
        h1, dw, v1(l1_conv_dw_b), v1(l1_conv_ln_g),
        v1(l1_conv_ln_b), l1_conv_pw2_w.astype(BF16), v1(l1_conv_pw2_b), x2, mod1, v1(l1_mix_post_g),
        v1(l1_ffn_pre_g), rw, tm=conv_tm)

    n = b * t
    rows = MOE_ROWS
    nt = n // MOE_TILE
    cnt_half = cnt_tile[:, 0, :N_EXPERTS].astype(jnp.int32).reshape(nt, 2, N_EXPERTS)
    cnt = cnt_half[:, 0] + cnt_half[:, 1]
    aligned = ((cnt + MOE_ALIGN - 1) // MOE_ALIGN) * MOE_ALIGN
    base = jnp.cumsum(aligned, axis=0) - aligned
    cap = ((base[-1] + MOE_WIN + rows - 1) // rows) * rows
    pend = jnp.cumsum(cap)
    s0 = (pend - cap)[None, :] + base
    n_blk = (2 * n + N_EXPERTS * (MOE_ALIGN * nt + MOE_WIN + rows)) // rows + 1
    block_e = jnp.clip(jnp.searchsorted(pend, jnp.arange(n_blk, dtype=jnp.int32) * rows, side='right'),
                       0, N_EXPERTS - 1).astype(jnp.int32)
    n_used = (pend[-1:] // rows).astype(jnp.int32)
    blk_row = jnp.arange(n_blk, dtype=jnp.int32) * rows
    slot_end = (pend - cap) + base[-1] + aligned[-1]
    block_used = ((blk_row < slot_end[block_e]) & (blk_row < pend[-1])).astype(jnp.int32)
    flat = lambda a: a.reshape(-1).astype(jnp.int32)

    xs = _dispatch(u4.reshape(n, d), routet, flat(s0), flat(cnt_half[:, 0]), flat(cnt), flat(pend - MOE_WIN),
                   n_used, n_rows=n_blk * rows)
    ys = _moe_ffn(xs, block_e, block_used, l1_moe_w1.astype(BF16), l1_moe_w3.astype(BF16),
                  l1_moe_w2.astype(BF16), rows=rows)
    return _combine(flat(s0), flat(cnt_half[:, 0]), flat(cnt), ys, route, x3, mod1, l1_ffn_post_g)
```

```python
import functools
import math

import jax
import jax.numpy as jnp
from jax import lax
from jax.experimental import pallas as pl
from jax.experimental.pallas import tpu as pltpu

F32 = jnp.float32
BF16 = jnp.bfloat16
EPS = 1e-6
NEG = -1e30
LOG2E = 1.4426950408889634

D_MODEL = 1024
GRID_W = 64
MLSTM_HEADS = 8
MLSTM_D = 64
MLSTM_CHUNK = 256
DIFF_HEADS = 4
DIFF_DQK = 64
ROPE_BASE = 10000.0
CONV_WIDTH = 31
N_EXPERTS = 8
MOE_ROWS = 512
MOE_TILE = 512
MOE_WIN = 512
MOE_ALIGN = 16
LANES = 128
MXU_DIM = 256
V7X_VMEM_BYTES = 64 * 1024 * 1024
VMEM_LIMIT = V7X_VMEM_BYTES - 12 * 1024 * 1024

ROW_TILE = 512
ATTN_Q_TILE = 1024


def _cp(*sem):
    return pltpu.CompilerParams(dimension_semantics=sem, vmem_limit_bytes=VMEM_LIMIT)


def _rms(x, g):
    return x * lax.rsqrt(jnp.mean(x * x, axis=-1, keepdims=True) + EPS) * g


def _silu(x):
    return x * jax.nn.sigmoid(x)


def _dot(a, b):
    return jnp.dot(a, b, preferred_element_type=F32)


def _dot_nt(a, b):
    return lax.dot_general(a, b, (((1,), (1,)), ((), ())), preferred_element_type=F32)


def _adaln_kernel(c_ref, w_ref, b_ref, o_ref):
    s = _silu(c_ref[...])
    o_ref[...] = _dot(s.astype(BF16), w_ref[...].astype(BF16)) + b_ref[...]


def _adaln(cpad, w, b):
    rows, d = cpad.shape
    n = w.shape[1]
    tn = 1536
    return pl.pallas_call(
        _adaln_kernel,
        grid=(n // tn,),
        in_specs=[pl.BlockSpec((rows, d), lambda j: (0, 0)),
                  pl.BlockSpec((d, tn), lambda j: (0, j)),
                  pl.BlockSpec((1, tn), lambda j: (0, j))],
        out_specs=pl.BlockSpec((rows, tn), lambda j: (0, j)),
        out_shape=jax.ShapeDtypeStruct((rows, n), F32),
        compiler_params=_cp("arbitrary"),
        name="adaln",
    )(cpad, w, b.reshape(1, n))


def _inproj_kernel(x_ref, xp_ref, xn_ref, mod_ref, g_ref, w_ref, cw_ref, gb_ref, cos_ref, sin_ref,
                   mq_ref, mk_ref, mv_ref, mo_ref, dq_ref, dk_ref, dv_ref, gt_ref, pext_ref,
                   *, tm, nt, rope):
    i = pl.program_id(1)
    g = g_ref[...]
    sh = mod_ref[0:1, :]
    sc = mod_ref[1:2, :]

    def mod(xv):
        return _rms(xv, g) * (1.0 + sc) + sh

    u = mod(x_ref[...])
    up = jnp.where(i > 0, mod(xp_ref[...]), 0.0)
    un = jnp.where(i < nt - 1, mod(xn_ref[...]), 0.0)
    ub = u.astype(BF16)
    uext = jnp.concatenate([up.astype(BF16), ub, un.astype(BF16)], axis=0)

    pext_ref[...] = _dot(uext, w_ref[:, 0:1024])
    cw = cw_ref[...]
    conv = (cw[0:1, :] * pext_ref[pl.ds(7, tm), :] + cw[1:2, :] * pext_ref[pl.ds(8, tm), :]
            + cw[2:3, :] * pext_ref[pl.ds(9, tm), :])
    act = _silu(conv)
    mq_ref[...] = (act[:, 0:512] * (MLSTM_D ** -0.5)).astype(BF16)
    mk_ref[...] = act[:, 512:1024].astype(BF16)

    p = _dot(ub, w_ref[:, 1024:2048])
    mv_ref[...] = p[:, 0:512].astype(BF16)
    mo_ref[...] = p[:, 512:1024].astype(BF16)

    p = _dot(ub, w_ref[:, 2048:3072])
    if rope:
        lane = lax.broadcasted_iota(jnp.int32, p.shape, 1)
        first_half = ((lane // 16) % 2) == 0
        nl = p.shape[1]
        partner = jnp.where(first_half, pltpu.roll(p, nl - 16, 1), pltpu.roll(p, 16, 1))
        cos = jnp.concatenate([cos_ref[...]] * 8, axis=1)
        sin = jnp.concatenate([sin_ref[...]] * 8, axis=1)
        p = p * cos + partner * sin
    dq_ref[...] = (p[:, 0:512] * (LOG2E * DIFF_DQK ** -0.5)).astype(BF16)
    dk_ref[...] = p[:, 512:1024].astype(BF16)

    p = _dot(ub, w_ref[:, 3072:3712])
    dv_ref[...] = p[:, 0:512].astype(BF16)
    gt_ref[...] = p[:, 512:640] + gb_ref[...]


def _inproj(x, mod, mod_row, g, w_perm, conv_w, gate_b, cos, sin, *, tm, rope):
    b, t, d = x.shape
    nt = t // tm
    hb = tm // 8
    nb8 = t // 8
    bf = lambda: jax.ShapeDtypeStruct((b, t, 512), BF16)
    if mod_row is None:
        mod_map = lambda bi, i: (bi, 0, 0)
    else:
        mod_map = lambda bi, i: (mod_row, 0, 0)
    kern = functools.partial(_inproj_kernel, tm=tm, nt=nt, rope=rope)
    o512 = pl.BlockSpec((None, tm, 512), lambda bi, i: (bi, i, 0))
    return pl.pallas_call(
        kern,
        grid=(b, nt),
        in_specs=[
            pl.BlockSpec((None, tm, d), lambda bi, i: (bi, i, 0)),
            pl.BlockSpec((None, 8, d), lambda bi, i: (bi, jnp.maximum(i * hb - 1, 0), 0)),
            pl.BlockSpec((None, 8, d), lambda bi, i: (bi, jnp.minimum((i + 1) * hb, nb8 - 1), 0)),
            pl.BlockSpec((None, 6, d), mod_map),
            pl.BlockSpec((1, d), lambda bi, i: (0, 0)),
            pl.BlockSpec((d, 3712), lambda bi, i: (0, 0)),
            pl.BlockSpec((3, d), lambda bi, i: (0, 0)),
            pl.BlockSpec((1, LANES), lambda bi, i: (0, 0)),
            pl.BlockSpec((tm, LANES), lambda bi, i: (i, 0)),
            pl.BlockSpec((tm, LANES), lambda bi, i: (i, 0)),
        ],
        out_specs=[o512] * 7 + [pl.BlockSpec((None, tm, LANES), lambda bi, i: (bi, i, 0))],
        out_shape=[bf() for _ in range(7)] + [jax.ShapeDtypeStruct((b, t, LANES), F32)],
        scratch_shapes=[pltpu.VMEM((tm + 16, 1024), F32)],
        compiler_params=_cp("arbitrary", "arbitrary"),
        name="inproj_rope" if rope else "inproj_ctx",
    )(x, x, x, mod, g, w_perm, conv_w, gate_b, cos, sin)


def _gateprep_kernel(g_ref, arow_ref, col_ref, *, L, nch):
    tt = L * nch
    gt = g_ref[...].T
    i_f, f_f, i_b, f_b = gt[0:8], gt[8:16], gt[16:24], gt[24:32]

    def logsig(v):
        return jnp.minimum(v, 0.0) - jnp.log1p(jnp.exp(-jnp.abs(v)))

    pos = lax.broadcasted_iota(jnp.int32, (8, tt), 1) % L

    def scan(v, op, ident, reverse):
        s = 1
        while s < L:
            if reverse:
                shifted = pltpu.roll(v, tt - s, 1)
                valid = pos < L - s
            else:
                shifted = pltpu.roll(v, s, 1)
                valid = pos >= s
            v = op(v, jnp.where(valid, shifted, ident))
            s *= 2
        return v

    outs = []
    for d, (ig, fg) in enumerate(((i_f, f_f), (i_b, f_b))):
        rev = d == 1
        bcum = scan(logsig(fg), jnp.add, 0.0, rev)
        a = ig - bcum
        cm = scan(a, jnp.maximum, NEG, rev)
        order = list(range(nch)) if not rev else [0] + list(range(nch - 1, 0, -1))
        mp = jnp.zeros((8, 1), F32)
        mp_c = [None] * nch
        for j in order:
            e = j * L if rev else j * L + L - 1
            mp_c[j] = jnp.broadcast_to(mp, (8, L))
            mp = bcum[:, e:e + 1] + jnp.maximum(mp, cm[:, e:e + 1])
        mprev = jnp.concatenate(mp_c, axis=1)
        m = jnp.maximum(mprev, cm)
        outs.append((a, m, jnp.exp(mprev - m), jnp.exp(-(bcum + m))))

    arow_ref[...] = jnp.concatenate([outs[0][0], outs[1][0]], axis=0)
    rows = []
    for pair in range(4):
        for q in range(1, 5):
            for d in range(2):
                rows.append(outs[d][q][2 * pair:2 * pair + 2] if q < 4 else jnp.zeros((2, tt), F32))
    rows.append(jnp.zeros((64, tt), F32))
    col_ref[...] = jnp.concatenate(rows, axis=0).T


def _gateprep(gall, *, L):
    b, tt, _ = gall.shape
    nch = tt // L
    return pl.pallas_call(
        functools.partial(_gateprep_kernel, L=L, nch=nch),
        grid=(b,),
        in_specs=[pl.BlockSpec((None, tt, LANES), lambda bi: (bi, 0, 0))],
        out_specs=[pl.BlockSpec((None, 16, tt), lambda bi: (bi, 0, 0)),
                   pl.BlockSpec((None, tt, LANES), lambda bi: (bi, 0, 0))],
        out_shape=[jax.ShapeDtypeStruct((b, 16, tt), F32), jax.ShapeDtypeStruct((b, tt, LANES), F32)],
        compiler_params=_cp("arbitrary"),
        name="mlstm_gateprep",
    )(gall)


def _mlstm_kernel(q_ref, k_ref, v_ref, o_ref, kc_ref, vc_ref, arow_ref, col_ref, ng_ref, out_ref,
                  c_ref, hf_ref, hb_ref, *, L, nc):
    p = pl.program_id(1)
    half = nc // 2
    lane = lax.broadcasted_iota(jnp.int32, (L, LANES), 1)
    lo = lane < MLSTM_D
    head_mask = (lo, jnp.logical_not(lo))
    ri = lax.broadcasted_iota(jnp.int32, (L, L), 0)
    ci = lax.broadcasted_iota(jnp.int32, (L, L), 1)
    causal = (ci <= ri, ci >= ri)
    ones_t = jnp.ones((L, LANES), BF16)
    shift = lax.rem(LANES - 16 * p, LANES)

    def cols(off):
        return pltpu.roll(col_ref[pl.ds(off, L), :], shift, 1)

    def col(blk, q, d, hh):
        j = q * 4 + d * 2 + hh
        return blk[:, j:j + 1]

    def vext_of(vb, hh):
        return jnp.concatenate([jnp.where(head_mask[hh], vb, jnp.zeros_like(vb)), ones_t], axis=1)

    def state_update(d, hh, kb, vext, blk, arow, dec):
        last = L - 1 if d == 0 else 0
        khm = jnp.where(head_mask[hh], kb, jnp.zeros_like(kb))
        ws = jnp.exp(arow - col(blk, 0, d, hh)[last:last + 1, :])
        ksc = (khm.T.astype(F32) * ws).astype(BF16)
        upd = _dot(ksc, vext)
        if dec is None:
            c_ref[d, hh] = upd
        else:
            c_ref[d, hh] = dec * c_ref[d, hh] + upd

    def arow_of(d, hh, off):
        return arow_ref[pl.ds(d * 8 + 2 * p + hh, 1), pl.ds(off, L)]

    blk0 = cols(0)
    kcb = kc_ref[...]
    vcb = vc_ref[...]
    for d in range(2):
        for hh in range(2):
            state_update(d, hh, kcb, vext_of(vcb, hh), blk0, arow_of(d, hh, 0), None)

    def compute(d, c):
        t0 = pl.multiple_of(c * L, L)
        off = pl.multiple_of(c * L + L, L)
        qb = q_ref[pl.ds(t0, L), :]
        kb = k_ref[pl.ds(t0, L), :]
        vb = v_ref[pl.ds(t0, L), :]
        blk = cols(off)
        last = L - 1 if d == 0 else 0
        hs = []
        for hh in range(2):
            arow = arow_of(d, hh, off)
            khm = jnp.where(head_mask[hh], kb, jnp.zeros_like(kb))
            s = _dot_nt(qb, khm)
            arg = jnp.where(causal[d], arow - col(blk, 0, d, hh), NEG)
            pm = (s * jnp.exp(arg)).astype(BF16)
            vext = vext_of(vb, hh)
            ch = c_ref[d, hh]
            tot = _dot(pm, vext) + col(blk, 1, d, hh) * _dot(qb, ch.astype(BF16))
            den = jnp.maximum(jnp.abs(tot[:, LANES:]), col(blk, 2, d, hh))
            hs.append(tot[:, :LANES] / den)
            dec = col(blk, 1, d, hh)[last:last + 1, :]
            state_update(d, hh, kb, vext, blk, arow, dec)
        return jnp.where(lo, hs[0], hs[1])

    def finalize(c, hsum):
        t0 = pl.multiple_of(c * L, L)
        sq = hsum * hsum
        s0 = jnp.sum(jnp.where(lo, sq, 0.0), axis=-1, keepdims=True)
        s1 = jnp.sum(jnp.where(lo, 0.0, sq), axis=-1, keepdims=True)
        ms = jnp.where(lo, s0, s1) * (1.0 / MLSTM_D)
        y = hsum * lax.rsqrt(ms + EPS) * ng_ref[...]
        gate = jax.nn.sigmoid(o_ref[pl.ds(t0, L), :].astype(F32))
        out_ref[pl.ds(t0, L), :] = (y * gate).astype(BF16)

    def phase_a(i, carry):
        hf_ref[pl.ds(pl.multiple_of(i * L, L), L), :] = compute(0, i)
        cb = nc - 1 - i
        hb_ref[pl.ds(pl.multiple_of((cb - half) * L, L), L), :] = compute(1, cb)
        return carry

    def phase_b(i, carry):
        hf = compute(0, i)
        finalize(i, hf + hb_ref[pl.ds(pl.multiple_of((i - half) * L, L), L), :])
        cb = nc - 1 - i
        hb = compute(1, cb)
        finalize(cb, hb + hf_ref[pl.ds(pl.multiple_of(cb * L, L), L), :])
        return carry

    lax.fori_loop(0, half, phase_a, 0)
    lax.fori_loop(half, nc, phase_b, 0)


def _mlstm(mq, mk, mv, mo, mkc, mvc, arow, cols, norm_g, *, L):
    b, t, _ = mq.shape
    ctx = mkc.shape[1]
    tt = arow.shape[2]
    nc = t // L
    tok = pl.BlockSpec((None, t, LANES), lambda bi, p: (bi, 0, p))
    ctxs = pl.BlockSpec((None, ctx, LANES), lambda bi, p: (bi, 0, p))
    return pl.pallas_call(
        functools.partial(_mlstm_kernel, L=L, nc=nc),
        grid=(b, 4),
        in_specs=[tok, tok, tok, tok, ctxs, ctxs,
                  pl.BlockSpec((None, 16, tt), lambda bi, p: (bi, 0, 0)),
                  pl.BlockSpec((None, tt, LANES), lambda bi, p: (bi, 0, 0)),
                  pl.BlockSpec((1, LANES), lambda bi, p: (0, p))],
        out_specs=tok,
        out_shape=jax.ShapeDtypeStruct((b, t, 512), BF16),
        scratch_shapes=[pltpu.VMEM((2, 2, LANES, 2 * LANES), F32),
                        pltpu.VMEM((t // 2, LANES), F32),
                        pltpu.VMEM((t // 2, LANES), F32)],
        compiler_params=_cp("arbitrary", "arbitrary"),
        name="mlstm_scan",
    )(mq, mk, mv, mo, mkc, mvc, arow, cols, norm_g.reshape(1, 512))


def _attn_kernel(q_ref, k_ref, v_ref, lam_ref, ng_ref, out_ref, kmax_ref, m_ref, acc_ref,
                 *, tq, kb, nkb, sub, lam_init):
    i = pl.program_id(2)
    lane = lax.broadcasted_iota(jnp.int32, (tq, LANES), 1)
    lo = lane < DIFF_DQK
    rr = lax.broadcasted_iota(jnp.int32, (LANES, LANES), 0)
    cc = lax.broadcasted_iota(jnp.int32, (LANES, LANES), 1)
    same_comp = ((rr < DIFF_DQK) == (cc < DIFF_DQK)).astype(BF16)

    def comp_sqnorm(a):
        af = a.astype(F32)
        return _dot((af * af).astype(BF16), same_comp)

    @pl.when(i == 0)
    def _():
        mx = jnp.zeros((1, LANES), F32)

        def kbody(j, mx):
            k0 = pl.multiple_of(j * kb, kb)
            return jnp.maximum(mx, jnp.max(comp_sqnorm(k_ref[pl.ds(k0, kb), :]), axis=0, keepdims=True))

        kmax_ref[...] = lax.fori_loop(0, nkb, kbody, mx)

    q = q_ref[...]
    zq = jnp.zeros_like(q)
    bnd = jnp.sqrt(comp_sqnorm(q) * kmax_ref[...]) * 1.02
    b1 = jnp.where(lane == 0, -bnd, 0.0)
    b2 = jnp.where(lane == 0, -pltpu.roll(bnd, DIFF_DQK, 1), 0.0)
    qs = jnp.concatenate(
        [jnp.concatenate([jnp.where(lo, q, zq), jnp.where(lo, zq, q)], axis=0),
         jnp.concatenate([b1, b2], axis=0).astype(BF16)], axis=1)
    acc_ref[...] = jnp.zeros(acc_ref.shape, F32)

    def ext(blk):
        return jnp.concatenate([blk, jnp.ones_like(blk)], axis=1)

    def run(step, blk):
        def body(j, carry):
            k0 = pl.multiple_of(j * blk, blk)
            step(k_ref[pl.ds(k0, blk), :], v_ref[pl.ds(k0, blk), :])
            return carry

        lax.fori_loop(0, (nkb * kb) // blk, body, 0)

    def fast_step(kblk, vblk):
        tot = None
        for c in range(0, kblk.shape[0], sub):
            pm = jnp.exp2(_dot_nt(qs, ext(kblk[c:c + sub]))).astype(BF16)
            part = _dot(pm, ext(vblk[c:c + sub]))
            tot = part if tot is None else tot + part
        acc_ref[...] += tot

    def slow_step(kblk, vblk):
        s = _dot_nt(qs, ext(kblk))
        m_old = m_ref[...]
        m_new = jnp.maximum(m_old, jnp.max(s, axis=-1, keepdims=True))
        pm = jnp.exp2(s - m_new).astype(BF16)
        acc_ref[...] = jnp.exp2(m_old - m_new) * acc_ref[...] + _dot(pm, ext(vblk))
        m_ref[...] = m_new

    fast = jnp.max(bnd) <= 56.0

    @pl.when(fast)
    def _():
        run(fast_step, kb)

    @pl.when(jnp.logical_not(fast))
    def _():
        m_ref[...] = jnp.full(m_ref.shape, NEG, F32)
        run(slow_step, sub)

    lq = lam_ref[...]
    lam = (jnp.exp(jnp.sum(lq[0:1, :] * lq[1:2, :], axis=-1, keepdims=True))
           - jnp.exp(jnp.sum(lq[2:3, :] * lq[3:4, :], axis=-1, keepdims=True)) + lam_init)
    a1 = acc_ref[0:tq, :]
    a2 = acc_ref[tq:2 * tq, :]
    o = a1[:, :LANES] / a1[:, LANES:] - lam * (a2[:, :LANES] / a2[:, LANES:])
    out_ref[...] = (_rms(o, ng_ref[...]) * (1.0 - lam_init)).astype(BF16)


def _attn(dq, dk, dv, lam_in, norm_g, *, tq, sub, max_sub_per_step, lam_init):
    b, t, _ = dq.shape
    tk = dk.shape[1]
    nsub = tk // sub
    per = max(g for g in range(1, max_sub_per_step + 1) if nsub % g == 0)
    kb = per * sub
    full = pl.BlockSpec((None, tk, LANES), lambda bi, h, i: (bi, 0, h))
    qs = pl.BlockSpec((None, tq, LANES), lambda bi, h, i: (bi, i, h))
    return pl.pallas_call(
        functools.partial(_attn_kernel, tq=tq, kb=kb, nkb=tk // kb, sub=sub, lam_init=lam_init),
        grid=(b, DIFF_HEADS, t // tq),
        in_specs=[qs, full, full,
                  pl.BlockSpec((8, LANES), lambda bi, h, i: (0, 0)),
                  pl.BlockSpec((1, LANES), lambda bi, h, i: (0, h))],
        out_specs=qs,
        out_shape=jax.ShapeDtypeStruct((b, t, 512), BF16),
        scratch_shapes=[pltpu.VMEM((1, LANES), F32), pltpu.VMEM((2 * tq, 1), F32),
                        pltpu.VMEM((2 * tq, 2 * LANES), F32)],
        compiler_params=_cp("arbitrary", "arbitrary", "arbitrary"),
        name="diff_attn",
    )(dq, dk, dv, lam_in, norm_g.reshape(1, 512))


def _swiglu(u, w1_ref, w3_ref, w2_ref):
    return _dot((_silu(_dot(u, w1_ref[...])) * _dot(u, w3_ref[...])).astype(BF16), w2_ref[...])


def _ffn_kernel(hm_ref, hd_ref, wt_ref, wb_ref, mg_ref, fg_ref, w1_ref, w3_ref, w2_ref, x_ref, mod0_ref, mod1_ref,
                pg_ref, ng_ref, pw_ref, pb_ref, x2_ref, h_ref):
    mix = _dot(hm_ref[...], wt_ref[...]) + _dot(hd_ref[...], wb_ref[...])
    x1 = x_ref[...] + mod0_ref[2:3, :] * _rms(mix, mg_ref[...])
    u = (_rms(x1, fg_ref[...]) * (1.0 + mod0_ref[4:5, :]) + mod0_ref[3:4, :]).astype(BF16)
    y = _swiglu(u, w1_ref, w3_ref, w2_ref)
    x2 = x1 + mod0_ref[5:6, :] * _rms(y, pg_ref[...])
    x2_ref[...] = x2
    u3 = (_rms(x2, ng_ref[...]) * (1.0 + mod1_ref[1:2, :]) + mod1_ref[0:1, :]).astype(BF16)
    ag = _dot(u3, pw_ref[...]) + pb_ref[...]
    h_ref[...] = ag[:, :D_MODEL] * jax.nn.sigmoid(ag[:, D_MODEL:])


def _ffn(hm, hd, wt, wb, mix_post_g, ffn_pre_g, w1, w3, w2, x, mod0, mod1, post_g, next_pre_g, pw1, pb1, *, tm):
    b, t, d = x.shape
    f = w1.shape[1]
    half = hm.shape[2]
    row = lambda w: pl.BlockSpec((None, tm, w), lambda bi, i: (bi, i, 0))
    modb = pl.BlockSpec((None, 6, d), lambda bi, i: (bi, 0, 0))
    vec = lambda w: pl.BlockSpec((1, w), lambda bi, i: (0, 0))
    wspec = lambda s: pl.BlockSpec(s, lambda bi, i: (0, 0), pipeline_mode=pl.Buffered(1))
    return pl.pallas_call(
        _ffn_kernel,
        grid=(b, t // tm),
        in_specs=[row(half), row(half), wspec((half, d)), wspec((half, d)), vec(d), vec(d),
                  wspec((d, f)), wspec((d, f)), wspec((f, d)), row(d), modb, modb, vec(d), vec(d),
                  wspec((d, 2 * d)), vec(2 * d)],
        out_specs=[row(d), row(d)],
        out_shape=[jax.ShapeDtypeStruct((b, t, d), F32), jax.ShapeDtypeStruct((b, t, d), F32)],
        compiler_params=_cp("arbitrary", "arbitrary"),
        name="outproj_ffn_glu",
    )(hm, hd, wt, wb, mix_post_g.reshape(1, d), ffn_pre_g.reshape(1, d), w1, w3, w2, x, mod0, mod1,
      post_g.reshape(1, d), next_pre_g.reshape(1, d), pw1, pb1)


HALO = 16
CONV_ROW_BLOCK = 128
CONV_LANES = 256
CONV_ROWS_EXTRA = 24


def _conv_kernel(h_ref, hp_ref, hn_ref, dw_ref, dwb_ref, lng_ref, lnb_ref, w2_ref, b2_ref,
                 x_ref, mod_ref, pg_ref, fg_ref, rw_ref, x3_ref, u4_ref, route_ref, routet_ref, cnt_ref,
                 hs_ref, sh_ref, cv_ref, wb_ref, *, tm, nt):
    i = pl.program_id(1)

    hs_ref[0:HALO, :] = jnp.where(i > 0, hp_ref[...], 0.0)
    hs_ref[HALO:HALO + tm, :] = h_ref[...]
    hs_ref[HALO + tm:, :] = jnp.where(i < nt - 1, hn_ref[...], 0.0)

    for r in range(1, 8):
        sh_ref[r - 1] = hs_ref[pl.ds(r, tm + CONV_ROWS_EXTRA), :]

    @pl.when((pl.program_id(0) == 0) & (i == 0))
    def _():
        for j in range(CONV_WIDTH):
            wb_ref[j] = jnp.broadcast_to(dw_ref[j:j + 1, :], (8, D_MODEL))

    nsub = CONV_ROW_BLOCK // 8

    for l0 in range(0, D_MODEL, CONV_LANES):
        def conv_rows(rb, carry, l0=l0):
            r0 = pl.multiple_of(rb * CONV_ROW_BLOCK, CONV_ROW_BLOCK)
            bias = jnp.broadcast_to(dwb_ref[:, l0:l0 + CONV_LANES], (8, CONV_LANES))
            acc = [bias] * nsub
            for j in range(CONV_WIDTH):
                r, a = (j + 1) % 8, (j + 1) // 8
                w = wb_ref[j, :, l0:l0 + CONV_LANES]
                for s in range(nsub):
                    rows = pl.ds(r0 + 8 * (a + s), 8)
                    if r == 0:
                        win = hs_ref[rows, l0:l0 + CONV_LANES]
                    else:
                        win = sh_ref[r - 1, rows, l0:l0 + CONV_LANES]
                    acc[s] = acc[s] + w * win
            for s in range(nsub):
                cv_ref[pl.ds(r0 + 8 * s, 8), l0:l0 + CONV_LANES] = acc[s]
            return carry

        lax.fori_loop(0, tm // CONV_ROW_BLOCK, conv_rows, 0)
    acc = cv_ref[...]
    mu = jnp.mean(acc, axis=-1, keepdims=True)
    cen = acc - mu
    var = jnp.mean(cen * cen, axis=-1, keepdims=True)
    hn = _silu(cen * lax.rsqrt(var + EPS) * lng_ref[...] + lnb_ref[...])
    y = _dot(hn.astype(BF16), w2_ref[...]) + b2_ref[...]
    x3 = x_ref[...] + mod_ref[2:3, :] * _rms(y, pg_ref[...])
    x3_ref[...] = x3
    u4 = _rms(x3, fg_ref[...]) * (1.0 + mod_ref[4:5, :]) + mod_ref[3:4, :]
    u4b = u4.astype(BF16)
    u4_ref[...] = u4b

    lane = lax.broadcasted_iota(jnp.int32, (tm, LANES), 1).astype(F32)
    logits = jnp.where(lane < N_EXPERTS, _dot(u4b, rw_ref[...]), NEG)
    m1 = jnp.max(logits, axis=-1, keepdims=True)
    i1 = jnp.min(jnp.where(logits == m1, lane, float(LANES)), axis=-1, keepdims=True)
    l2 = jnp.where(lane == i1, NEG, logits)
    m2 = jnp.max(l2, axis=-1, keepdims=True)
    i2 = jnp.min(jnp.where(l2 == m2, lane, float(LANES)), axis=-1, keepdims=True)
    e21 = jnp.exp(m2 - m1)
    g1 = 1.0 / (1.0 + e21)
    g2 = e21 * g1
    sel = ((lane == i1) | (lane == i2)).astype(F32)
    ri = lax.broadcasted_iota(jnp.int32, (tm, tm), 0)
    ci = lax.broadcasted_iota(jnp.int32, (tm, tm), 1)
    tri = (ci <= ri).astype(BF16)
    csum = _dot(tri, sel.astype(BF16))
    rank = csum - sel
    r1 = jnp.sum(jnp.where(lane == i1, rank, 0.0), axis=-1, keepdims=True)
    r2 = jnp.sum(jnp.where(lane == i2, rank, 0.0), axis=-1, keepdims=True)
    cnt_ref[...] = jnp.broadcast_to(csum[tm - 1:tm, :], (8, LANES))
    route = jnp.zeros((tm, LANES), F32)
    for n, v in enumerate((i1, i2, g1, g2, r1, r2)):
        route = jnp.where(lane == float(n), v, route)
    routet_ref[...] = route.T[0:8, :]
    gsel = jnp.where(lane == i1, g1, jnp.where(lane == i2, g2, 0.0))
    ghi = gsel.astype(BF16)
    glo = (gsel - ghi.astype(F32)).astype(BF16)
    route_ref[...] = jnp.concatenate([jnp.where(sel > 0.0, rank, -1.0).astype(BF16), ghi, glo], axis=1)


def _convmod(h, dw, dwb, lng, lnb, w2, b2, x2, mod1, post_g, ffn_pre_g, rw, *, tm):
    b, t, d = x2.shape
    nt = t // tm
    hb = tm // HALO
    nbh = t // HALO
    row = lambda w: pl.BlockSpec((None, tm, w), lambda bi, i: (bi, i, 0))
    cst = lambda s: pl.BlockSpec(s, lambda bi, i: (0,) * len(s))
    return pl.pallas_call(
        functools.partial(_conv_kernel, tm=tm, nt=nt),
        grid=(b, nt),
        in_specs=[row(d),
                  pl.BlockSpec((None, HALO, d), lambda bi, i: (bi, jnp.maximum(i * hb - 1, 0), 0)),
                  pl.BlockSpec((None, HALO, d), lambda bi, i: (bi, jnp.minimum((i + 1) * hb, nbh - 1), 0)),
                  cst((32, d)), cst((1, d)), cst((1, d)), cst((1, d)),
                  cst((d, d)), cst((1, d)), row(d),
                  pl.BlockSpec((None, 6, d), lambda bi, i: (bi, 0, 0)), cst((1, d)), cst((1, d)),
                  cst((d, LANES))],
        out_specs=[row(d), row(d), row(3 * LANES),
                   pl.BlockSpec((None, 8, tm), lambda bi, i: (bi * nt + i, 0, 0)),
                   pl.BlockSpec((None, 8, LANES), lambda bi, i: (bi * nt + i, 0, 0))],
        out_shape=[jax.ShapeDtypeStruct((b, t, d), F32), jax.ShapeDtypeStruct((b, t, d), BF16),
                   jax.ShapeDtypeStruct((b, t, 3 * LANES), BF16), jax.ShapeDtypeStruct((b * nt, 8, tm), F32),
                   jax.ShapeDtypeStruct((b * nt, 8, LANES), F32)],
        scratch_shapes=[pltpu.VMEM((tm + 2 * HALO, d), F32), pltpu.VMEM((7, tm + CONV_ROWS_EXTRA, d), F32),
                        pltpu.VMEM((tm, d), F32), pltpu.VMEM((CONV_WIDTH, 8, d), F32)],
        compiler_params=_cp("arbitrary", "arbitrary"),
        name="conv_module_router",
    )(h, h, h, dw, dwb, lng, lnb, w2, b2, x2, mod1, post_g, ffn_pre_g, rw)


def _window_copies(ref_hbm, buf_ref, sem, s0_ref, t, slot, to_hbm):
    out = []
    for e in range(N_EXPERTS):
        hbm = ref_hbm.at[pl.ds(pl.multiple_of(s0_ref[t * N_EXPERTS + e], MOE_ALIGN), MOE_WIN), :]
        vmem = buf_ref.at[slot, e]
        src, dst = (vmem, hbm) if to_hbm else (hbm, vmem)
        out.append(pltpu.make_async_copy(src, dst, sem.at[slot, e]))
    return out


def _dispatch_kernel(s0_ref, hoff_ref, cnt_ref, zoff_ref, nu_ref, u_ref, rt_ref, xs_ref, win_ref, zero_ref,
                     sem, zsem, *, nt, n_blk):
    t = pl.program_id(0)
    slot = lax.rem(t, 2)
    half = MOE_TILE // 2

    @pl.when(t == 0)
    def _():
        zero_ref[...] = jnp.zeros(zero_ref.shape, BF16)
        zc = [pltpu.make_async_copy(
            zero_ref, xs_ref.at[pl.ds(pl.multiple_of(zoff_ref[e], MOE_ALIGN), MOE_WIN), :], zsem.at[e])
            for e in range(N_EXPERTS)]
        for c in zc:
            c.start()
        for c in zc:
            c.wait()

        def clear_block(bk, carry):
            c = pltpu.make_async_copy(
                zero_ref, xs_ref.at[pl.ds(pl.multiple_of(bk * MOE_ROWS, MOE_ROWS), MOE_ROWS), :], zsem.at[0])
            c.start()
            c.wait()
            return carry

        lax.fori_loop(nu_ref[0], n_blk, clear_block, 0)

    rt = jnp.concatenate([rt_ref[0], rt_ref[1]], axis=1)
    e1, e2, r1, r2 = rt[0:1], rt[1:2], rt[4:5], rt[5:6]
    second = lax.broadcasted_iota(jnp.int32, (1, MOE_TILE), 1) >= half
    srow = lax.broadcasted_iota(jnp.int32, (half, MOE_TILE), 0).astype(F32)
    u = u_ref[...]
    for e in range(N_EXPERTS):
        off = jnp.where(second, hoff_ref[t * N_EXPERTS + e].astype(F32), 0.0)
        lr = jnp.where(e1 == float(e), r1 + off, jnp.where(e2 == float(e), r2 + off, -1.0))
        win_ref[slot, e, 0:half, :] = _dot((srow == lr).astype(BF16), u).astype(BF16)
        big = cnt_ref[t * N_EXPERTS + e] > half

        @pl.when(big)
        def _():
            win_ref[slot, e, half:MOE_WIN, :] = _dot((srow + float(half) == lr).astype(BF16), u).astype(BF16)

        @pl.when(jnp.logical_not(big))
        def _():
            win_ref[slot, e, half:MOE_WIN, :] = jnp.zeros((half, D_MODEL), BF16)

    @pl.when(t > 0)
    def _():
        for c in _window_copies(xs_ref, win_ref, sem, s0_ref, t - 1, 1 - slot, True):
            c.wait()

    for c in _window_copies(xs_ref, win_ref, sem, s0_ref, t, slot, True):
        c.start()

    @pl.when(t == nt - 1)
    def _():
        for c in _window_copies(xs_ref, win_ref, sem, s0_ref, t, slot, True):
            c.wait()


def _dispatch(u4, routet, s0, hoff, cnt, zoff, n_used, *, n_rows):
    assert MOE_WIN == MOE_ROWS
    n, d = u4.shape
    nt = n // MOE_TILE
    per = MOE_TILE // routet.shape[2]
    grid_spec = pltpu.PrefetchScalarGridSpec(
        num_scalar_prefetch=5,
        grid=(nt,),
        in_specs=[pl.BlockSpec((MOE_TILE, d), lambda t, *_: (t, 0)),
                  pl.BlockSpec((per, 8, routet.shape[2]), lambda t, *_: (t, 0, 0))],
        out_specs=pl.BlockSpec(memory_space=pl.ANY),
        scratch_shapes=[pltpu.VMEM((2, N_EXPERTS, MOE_WIN, d), BF16), pltpu.VMEM((MOE_WIN, d), BF16),
                        pltpu.SemaphoreType.DMA((2, N_EXPERTS)), pltpu.SemaphoreType.DMA((N_EXPERTS,))],
    )
    return pl.pallas_call(
        functools.partial(_dispatch_kernel, nt=nt, n_blk=n_rows // MOE_ROWS),
        grid_spec=grid_spec,
        out_shape=jax.ShapeDtypeStruct((n_rows, d), BF16),
        compiler_params=_cp("arbitrary"),
        name="moe_dispatch",
    )(s0, hoff, cnt, zoff, n_used, u4, routet)


def _moe_kernel(be_ref, used_ref, x_ref, w1_ref, w3_ref, w2_ref, y_ref):
    used = used_ref[pl.program_id(0)] > 0

    @pl.when(used)
    def _():
        y_ref[...] = _swiglu(x_ref[...], w1_ref, w3_ref, w2_ref).astype(BF16)

    @pl.when(jnp.logical_not(used))
    def _():
        y_ref[...] = jnp.zeros(y_ref.shape, BF16)


def _moe_ffn(xs, block_e, block_used, w1, w3, w2, *, rows):
    n_rows, d = xs.shape
    f = w1.shape[2]
    wspec = lambda s: pl.BlockSpec((None,) + s, lambda i, be, bu: (be[i], 0, 0), pipeline_mode=pl.Buffered(1))
    grid_spec = pltpu.PrefetchScalarGridSpec(
        num_scalar_prefetch=2,
        grid=(n_rows // rows,),
        in_specs=[pl.BlockSpec((rows, d), lambda i, be, bu: (i, 0)),
                  wspec((d, f)), wspec((d, f)), wspec((f, d))],
        out_specs=pl.BlockSpec((rows, d), lambda i, be, bu: (i, 0)),
    )
    return pl.pallas_call(
        _moe_kernel,
        grid_spec=grid_spec,
        out_shape=jax.ShapeDtypeStruct((n_rows, d), BF16),
        compiler_params=_cp("arbitrary"),
        name="moe_ffn",
    )(block_e, block_used, xs, w1, w3, w2)


def _combine_kernel(s0_ref, hoff_ref, cnt_ref, ys_ref, route_ref, x_ref, mod_ref, pg_ref, out_ref,
                    buf_ref, acc_ref, sem, *, nti, nt):
    t = pl.program_id(0) * nti + pl.program_id(1)
    slot = lax.rem(t, 2)
    half = MOE_TILE // 2

    @pl.when(t == 0)
    def _():
        for c in _window_copies(ys_ref, buf_ref, sem, s0_ref, 0, 0, False):
            c.start()

    @pl.when(t + 1 < nt)
    def _():
        for c in _window_copies(ys_ref, buf_ref, sem, s0_ref, t + 1, 1 - slot, False):
            c.start()

    for c in _window_copies(ys_ref, buf_ref, sem, s0_ref, t, slot, False):
        c.wait()

    route = route_ref[...]
    sr = lax.broadcasted_iota(jnp.int32, (LANES, N_EXPERTS * LANES), 0)
    sc = lax.broadcasted_iota(jnp.int32, (LANES, N_EXPERTS * LANES), 1)
    spread = (sc // LANES == sr).astype(BF16)
    rank_b = _dot(route[:, 0:LANES], spread)
    gate_b = _dot(route[:, LANES:2 * LANES], spread) + _dot(route[:, 2 * LANES:3 * LANES], spread)
    second = lax.broadcasted_iota(jnp.int32, (MOE_TILE, LANES), 0) >= half
    scol = lax.broadcasted_iota(jnp.int32, (MOE_TILE, LANES), 1).astype(F32)

    def onehot(lr, first):
        return jnp.concatenate([(scol + float(first + c) == lr) for c in range(0, half, LANES)],
                               axis=1).astype(BF16)

    for e in range(N_EXPERTS):
        rk = rank_b[:, e * LANES:(e + 1) * LANES]
        off = jnp.where(second, hoff_ref[t * N_EXPERTS + e].astype(F32), 0.0)
        lr = jnp.where(rk < 0.0, -1.0, rk + off)
        ge = jnp.concatenate([gate_b[:, e * LANES:(e + 1) * LANES]] * (D_MODEL // LANES), axis=1)
        z = ge * _dot(onehot(lr, 0), buf_ref[slot, e, 0:half, :])
        if e == 0:
            acc_ref[...] = z
        else:
            acc_ref[...] += z

        @pl.when(cnt_ref[t * N_EXPERTS + e] > half)
        def _():
            acc_ref[...] += ge * _dot(onehot(lr, half), buf_ref[slot, e, half:MOE_WIN, :])

    out_ref[...] = x_ref[...] + mod_ref[5:6, :] * _rms(acc_ref[...], pg_ref[...])


def _combine(s0, hoff, cnt, ys, route, x3, mod1, post_g):
    b, t, d = x3.shape
    nti = t // MOE_TILE
    row = lambda w: pl.BlockSpec((None, MOE_TILE, w), lambda bi, i, *_: (bi, i, 0))
    grid_spec = pltpu.PrefetchScalarGridSpec(
        num_scalar_prefetch=3,
        grid=(b, nti),
        in_specs=[pl.BlockSpec(memory_space=pl.ANY), row(3 * LANES), row(d),
                  pl.BlockSpec((None, 6, d), lambda bi, i, *_: (bi, 0, 0)),
                  pl.BlockSpec((1, d), lambda bi, i, *_: (0, 0))],
        out_specs=row(d),
        scratch_shapes=[pltpu.VMEM((2, N_EXPERTS, MOE_WIN, d), BF16), pltpu.VMEM((MOE_TILE, d), F32),
                        pltpu.SemaphoreType.DMA((2, N_EXPERTS))],
    )
    return pl.pallas_call(
        functools.partial(_combine_kernel, nti=nti, nt=b * nti),
        grid_spec=grid_spec,
        out_shape=jax.ShapeDtypeStruct((b, t, d), F32),
        compiler_params=_cp("arbitrary", "arbitrary"),
        name="moe_combine",
    )(s0, hoff, cnt, ys, route, x3, mod1, post_g.reshape(1, d))


def _rope_tables(t_len):
    rows = t_len // GRID_W
    row = jnp.repeat(jnp.arange(rows, dtype=F32), GRID_W)
    col = jnp.tile(jnp.arange(GRID_W, dtype=F32), rows)
    axis_dim = DIFF_DQK // 2
    inv = ROPE_BASE ** (-jnp.arange(0, axis_dim, 2, dtype=F32) / axis_dim)
    ang_r = row[:, None] * inv
    ang_c = col[:, None] * inv
    cr, sr, cc, sc = jnp.cos(ang_r), jnp.sin(ang_r), jnp.cos(ang_c), jnp.sin(ang_c)
    cos = jnp.concatenate([cr, cr, cc, cc] * 2, axis=1)
    sin = jnp.concatenate([-sr, sr, -sc, sc] * 2, axis=1)
    return cos, sin


def kernel(x, c, ctx, c_ctx, l0_mod_w, l0_mod_b, l0_mix_pre_g, l0_mix_post_g, l0_w_in, l0_mlstm_gate_b, l0_mlstm_conv_w, l0_mlstm_norm_g, l0_lambda_q1, l0_lambda_k1, l0_lambda_q2, l0_lambda_k2, l0_diff_norm_g, l0_w_out, l0_ffn_pre_g, l0_ffn_post_g, l0_ffn_w1, l0_ffn_w3, l0_ffn_w2, l1_mod_w, l1_mod_b, l1_mix_pre_g, l1_mix_post_g, l1_conv_pw1_w, l1_conv_pw1_b, l1_conv_dw_w, l1_conv_dw_b, l1_conv_ln_g, l1_conv_ln_b, l1_conv_pw2_w, l1_conv_pw2_b, l1_ffn_pre_g, l1_ffn_post_g, l1_router_w, l1_moe_w1, l1_moe_w3, l1_moe_w2):
    b, t, d = x.shape
    n_ctx = ctx.shape[1]
    L = MLSTM_CHUNK
    assert d == D_MODEL and n_ctx == L and t % (2 * L) == 0 and b <= 8

    cpad = jnp.zeros((16, d), F32).at[:b].set(c).at[8].set(c_ctx)
    mod0 = _adaln(cpad, l0_mod_w, l0_mod_b).reshape(16, 6, d)
    mod1 = _adaln(cpad, l1_mod_w, l1_mod_b).reshape(16, 6, d)

    w_perm = jnp.concatenate(
        [l0_w_in[:, :2048], l0_w_in[:, 2080:], l0_w_in[:, 2048:2080], jnp.zeros((d, 96), F32)], axis=1).astype(BF16)
    gate_b = jnp.concatenate([l0_mlstm_gate_b, jnp.zeros((96,), F32)]).reshape(1, LANES)
    cos, sin = _rope_tables(t)
    g0 = l0_mix_pre_g.reshape(1, d)
    mq, mk, mv, mo, dq, dk, dv, gates = _inproj(x, mod0, None, g0, w_perm, l0_mlstm_conv_w, gate_b, cos, sin,
                                                 tm=ROW_TILE, rope=True)
    _, mkc, mvc, _, _, dkc, dvc, gates_c = _inproj(ctx, mod0, 8, g0, w_perm, l0_mlstm_conv_w, gate_b,
                                                   cos[:n_ctx], sin[:n_ctx], tm=n_ctx, rope=False)

    arow, cols = _gateprep(jnp.concatenate([gates_c, gates], axis=1), L=L)
    hm = _mlstm(mq, mk, mv, mo, mkc, mvc, arow, cols, l0_mlstm_norm_g, L=L)

    lam_init = 0.8 - 0.6 * math.exp(-0.3 * 0)
    lam_in = jnp.zeros((8, LANES), F32).at[0, :DIFF_DQK].set(l0_lambda_q1).at[1, :DIFF_DQK].set(l0_lambda_k1)
    lam_in = lam_in.at[2, :DIFF_DQK].set(l0_lambda_q2).at[3, :DIFF_DQK].set(l0_lambda_k2)
    hd = _attn(dq, jnp.concatenate([dkc, dk], axis=1), jnp.concatenate([dvc, dv], axis=1), lam_in,
               l0_diff_norm_g, tq=min(ATTN_Q_TILE, t), sub=MXU_DIM, max_sub_per_step=(t + n_ctx) // MXU_DIM,
               lam_init=lam_init)

    w_out = l0_w_out.astype(BF16)
    x2, h1 = _ffn(hm, hd, w_out[:512], w_out[512:], l0_mix_post_g, l0_ffn_pre_g,
                  l0_ffn_w1.astype(BF16), l0_ffn_w3.astype(BF16), l0_ffn_w2.astype(BF16), x, mod0, mod1,
                  l0_ffn_post_g, l1_mix_pre_g, l1_conv_pw1_w.astype(BF16), l1_conv_pw1_b.reshape(1, -1),
                  tm=ROW_TILE)

    dw = jnp.concatenate([l1_conv_dw_w, jnp.zeros((1, d), F32)], axis=0)
    rw = jnp.concatenate([l1_router_w, jnp.zeros((d, LANES - N_EXPERTS), F32)], axis=1).astype(BF16)
    v1 = lambda a: a.reshape(1, -1)
    conv_tm = MOE_TILE // 2
    x3, u4, route, routet, cnt_tile = _convmod(
        h1, dw, v1(l1_conv_dw_b), v1(l1_conv_ln_g),
        v1(l1_conv_ln_b), l1_conv_pw2_w.astype(BF16), v1(l1_conv_pw2_b), x2, mod1, v1(l1_mix_post_g),
        v1(l1_ffn_pre_g), rw, tm=conv_tm)

    n = b * t
    rows = MOE_ROWS
    nt = n // MOE_TILE
    cnt_half = cnt_tile[:, 0, :N_EXPERTS].astype(jnp.int32).reshape(nt, 2, N_EXPERTS)
    cnt = cnt_half[:, 0] + cnt_half[:, 1]
    aligned = ((cnt + MOE_ALIGN - 1) // MOE_ALIGN) * MOE_ALIGN
    base = jnp.cumsum(aligned, axis=0) - aligned
    cap = ((base[-1] + MOE_WIN + rows - 1) // rows) * rows
    pend = jnp.cumsum(cap)
    s0 = (pend - cap)[None, :] + base
    n_blk = (2 * n + N_EXPERTS * (MOE_ALIGN * nt + MOE_WIN + rows)) // rows + 1
    blk_row = jnp.arange(n_blk, dtype=jnp.int32) * rows
    block_e = jnp.minimum(jnp.sum((blk_row[:, None] >= pend[None, :]).astype(jnp.int32), axis=1), N_EXPERTS - 1)
    n_used = (pend[-1:] // rows).astype(jnp.int32)
    slot_end = (pend - cap) + base[-1] + aligned[-1]
    block_used = ((blk_row < slot_end[block_e]) & (blk_row < pend[-1])).astype(jnp.int32)
    flat = lambda a: a.reshape(-1).astype(jnp.int32)

    xs = _dispatch(u4.reshape(n, d), routet, flat(s0), flat(cnt_half[:, 0]), flat(cnt), flat(pend - MOE_WIN),
                   n_used, n_rows=n_blk * rows)
    ys = _moe_ffn(xs, block_e, block_used, l1_moe_w1.astype(BF16), l1_moe_w3.astype(BF16),
                  l1_moe_w2.astype(BF16), rows=rows)
    return _combine(flat(s0), flat(cnt_half[:, 0]), flat(cnt), ys, route, x3, mod1, l1_ffn_post_g)
```

```python
import functools
import math

import jax
import jax.numpy as jnp
import numpy as np
from jax import lax
from jax.experimental import pallas as pl
from jax.experimental.pallas import tpu as pltpu

F32 = jnp.float32
BF16 = jnp.bfloat16
EPS = 1e-6
NEG = -1e30
LOG2E = 1.4426950408889634

D_MODEL = 1024
GRID_W = 64
MLSTM_HEADS = 8
MLSTM_D = 64
MLSTM_CHUNK = 256
DIFF_HEADS = 4
DIFF_DQK = 64
ROPE_BASE = 10000.0
CONV_WIDTH = 31
N_EXPERTS = 8
MOE_ROWS = 512
MOE_TILE = 512
MOE_WIN = 512
MOE_ALIGN = 16
MOE_CHUNKS = (0, 160, 256, 512)
LANES = 128
MXU_DIM = 256
V7X_VMEM_BYTES = 64 * 1024 * 1024
VMEM_LIMIT = V7X_VMEM_BYTES - 12 * 1024 * 1024

ROW_TILE = 512
ATTN_Q_TILE = 1024
FFN_ROW_GROUP = 512


def _cp(*sem):
    return pltpu.CompilerParams(dimension_semantics=sem, vmem_limit_bytes=VMEM_LIMIT)


def _rms(x, g):
    return x * lax.rsqrt(jnp.mean(x * x, axis=-1, keepdims=True) + EPS) * g


def _silu(x):
    return x * jax.nn.sigmoid(x)


def _dot(a, b):
    return jnp.dot(a, b, preferred_element_type=F32)


def _dot_nt(a, b):
    return lax.dot_general(a, b, (((1,), (1,)), ((), ())), preferred_element_type=F32)


def _adaln_kernel(c_ref, w_ref, b_ref, o_ref):
    s = _silu(c_ref[...])
    o_ref[...] = _dot(s.astype(BF16), w_ref[...].astype(BF16)) + b_ref[...]


def _adaln(cpad, w, b):
    rows, d = cpad.shape
    n = w.shape[1]
    tn = 1536
    return pl.pallas_call(
        _adaln_kernel,
        grid=(n // tn,),
        in_specs=[pl.BlockSpec((rows, d), lambda j: (0, 0)),
                  pl.BlockSpec((d, tn), lambda j: (0, j)),
                  pl.BlockSpec((1, tn), lambda j: (0, j))],
        out_specs=pl.BlockSpec((rows, tn), lambda j: (0, j)),
        out_shape=jax.ShapeDtypeStruct((rows, n), F32),
        compiler_params=_cp("arbitrary"),
        name="adaln",
    )(cpad, w, b.reshape(1, n))


def _inproj_kernel(x_ref, xp_ref, xn_ref, mod_ref, g_ref, w_ref, cw_ref, gb_ref, cos_ref, sin_ref,
                   mq_ref, mk_ref, mv_ref, mo_ref, dq_ref, dk_ref, dv_ref, gt_ref, pext_ref,
                   *, tm, nt, rope):
    i = pl.program_id(1)
    g = g_ref[...]
    sh = mod_ref[0:1, :]
    sc = mod_ref[1:2, :]

    def mod(xv):
        return _rms(xv, g) * (1.0 + sc) + sh

    u = mod(x_ref[...])
    up = jnp.where(i > 0, mod(xp_ref[...]), 0.0)
    un = jnp.where(i < nt - 1, mod(xn_ref[...]), 0.0)
    ub = u.astype(BF16)
    uext = jnp.concatenate([up.astype(BF16), ub, un.astype(BF16)], axis=0)

    pext_ref[...] = _dot(uext, w_ref[:, 0:1024])
    cw = cw_ref[...]
    conv = (cw[0:1, :] * pext_ref[pl.ds(7, tm), :] + cw[1:2, :] * pext_ref[pl.ds(8, tm), :]
            + cw[2:3, :] * pext_ref[pl.ds(9, tm), :])
    act = _silu(conv)
    mq_ref[...] = (act[:, 0:512] * (MLSTM_D ** -0.5)).astype(BF16)
    mk_ref[...] = act[:, 512:1024].astype(BF16)

    p = _dot(ub, w_ref[:, 1024:2048])
    mv_ref[...] = p[:, 0:512].astype(BF16)
    mo_ref[...] = p[:, 512:1024].astype(BF16)

    p = _dot(ub, w_ref[:, 2048:3072])
    if rope:
        lane = lax.broadcasted_iota(jnp.int32, p.shape, 1)
        first_half = ((lane // 16) % 2) == 0
        nl = p.shape[1]
        partner = jnp.where(first_half, pltpu.roll(p, nl - 16, 1), pltpu.roll(p, 16, 1))
        cos = jnp.concatenate([cos_ref[...]] * 8, axis=1)
        sin = jnp.concatenate([sin_ref[...]] * 8, axis=1)
        p = p * cos + partner * sin
    dq_ref[...] = (p[:, 0:512] * (LOG2E * DIFF_DQK ** -0.5)).astype(BF16)
    dk_ref[...] = p[:, 512:1024].astype(BF16)

    p = _dot(ub, w_ref[:, 3072:3712])
    dv_ref[...] = p[:, 0:512].astype(BF16)
    gt_ref[...] = p[:, 512:640] + gb_ref[...]


def _inproj(x, mod, mod_row, g, w_perm, conv_w, gate_b, cos, sin, *, tm, rope):
    b, t, d = x.shape
    nt = t // tm
    hb = tm // 8
    nb8 = t // 8
    bf = lambda: jax.ShapeDtypeStruct((b, t, 512), BF16)
    if mod_row is None:
        mod_map = lambda bi, i: (bi, 0, 0)
    else:
        mod_map = lambda bi, i: (mod_row, 0, 0)
    kern = functools.partial(_inproj_kernel, tm=tm, nt=nt, rope=rope)
    o512 = pl.BlockSpec((None, tm, 512), lambda bi, i: (bi, i, 0))
    return pl.pallas_call(
        kern,
        grid=(b, nt),
        in_specs=[
            pl.BlockSpec((None, tm, d), lambda bi, i: (bi, i, 0)),
            pl.BlockSpec((None, 8, d), lambda bi, i: (bi, jnp.maximum(i * hb - 1, 0), 0)),
            pl.BlockSpec((None, 8, d), lambda bi, i: (bi, jnp.minimum((i + 1) * hb, nb8 - 1), 0)),
            pl.BlockSpec((None, 6, d), mod_map),
            pl.BlockSpec((1, d), lambda bi, i: (0, 0)),
            pl.BlockSpec((d, 3712), lambda bi, i: (0, 0)),
            pl.BlockSpec((3, d), lambda bi, i: (0, 0)),
            pl.BlockSpec((1, LANES), lambda bi, i: (0, 0)),
            pl.BlockSpec((tm, LANES), lambda bi, i: (i, 0)),
            pl.BlockSpec((tm, LANES), lambda bi, i: (i, 0)),
        ],
        out_specs=[o512] * 7 + [pl.BlockSpec((None, tm, LANES), lambda bi, i: (bi, i, 0))],
        out_shape=[bf() for _ in range(7)] + [jax.ShapeDtypeStruct((b, t, LANES), F32)],
        scratch_shapes=[pltpu.VMEM((tm + 16, 1024), F32)],
        compiler_params=_cp("arbitrary", "arbitrary"),
        name="inproj_rope" if rope else "inproj_ctx",
    )(x, x, x, mod, g, w_perm, conv_w, gate_b, cos, sin)


def _gateprep_kernel(gc_ref, g_ref, arow_ref, col_ref, *, L, nch):
    tt = L * nch
    gt = jnp.concatenate([gc_ref[...].T, g_ref[...].T], axis=1)
    i_f, f_f, i_b, f_b = gt[0:8], gt[8:16], gt[16:24], gt[24:32]

    def logsig(v):
        return jnp.minimum(v, 0.0) - jnp.log1p(jnp.exp(-jnp.abs(v)))

    pos = lax.broadcasted_iota(jnp.int32, (8, tt), 1) % L

    def scan(v, op, ident, reverse):
        s = 1
        while s < L:
            if reverse:
                shifted = pltpu.roll(v, tt - s, 1)
                valid = pos < L - s
            else:
                shifted = pltpu.roll(v, s, 1)
                valid = pos >= s
            v = op(v, jnp.where(valid, shifted, ident))
            s *= 2
        return v

    outs = []
    for d, (ig, fg) in enumerate(((i_f, f_f), (i_b, f_b))):
        rev = d == 1
        bcum = scan(logsig(fg), jnp.add, 0.0, rev)
        a = ig - bcum
        cm = scan(a, jnp.maximum, NEG, rev)
        order = list(range(nch)) if not rev else [0] + list(range(nch - 1, 0, -1))
        mp = jnp.zeros((8, 1), F32)
        mp_c = [None] * nch
        for j in order:
            e = j * L if rev else j * L + L - 1
            mp_c[j] = jnp.broadcast_to(mp, (8, L))
            mp = bcum[:, e:e + 1] + jnp.maximum(mp, cm[:, e:e + 1])
        mprev = jnp.concatenate(mp_c, axis=1)
        m = jnp.maximum(mprev, cm)
        outs.append((a, m, jnp.exp(mprev - m), jnp.exp(-(bcum + m))))

    arow_ref[...] = jnp.concatenate([outs[0][0], outs[1][0]], axis=0)
    rows = []
    for pair in range(4):
        for q in range(1, 5):
            for d in range(2):
                rows.append(outs[d][q][2 * pair:2 * pair + 2] if q < 4 else jnp.zeros((2, tt), F32))
    rows.append(jnp.zeros((64, tt), F32))
    col_ref[...] = jnp.concatenate(rows, axis=0).T


def _gateprep(gates_c, gates, *, L):
    b, t, _ = gates.shape
    n_ctx = gates_c.shape[1]
    tt = n_ctx + t
    nch = tt // L
    return pl.pallas_call(
        functools.partial(_gateprep_kernel, L=L, nch=nch),
        grid=(b,),
        in_specs=[pl.BlockSpec((None, n_ctx, LANES), lambda bi: (bi, 0, 0)),
                  pl.BlockSpec((None, t, LANES), lambda bi: (bi, 0, 0))],
        out_specs=[pl.BlockSpec((None, 16, tt), lambda bi: (bi, 0, 0)),
                   pl.BlockSpec((None, tt, LANES), lambda bi: (bi, 0, 0))],
        out_shape=[jax.ShapeDtypeStruct((b, 16, tt), F32), jax.ShapeDtypeStruct((b, tt, LANES), F32)],
        compiler_params=_cp("arbitrary"),
        name="mlstm_gateprep",
    )(gates_c, gates)


def _mlstm_kernel(q_ref, k_ref, v_ref, o_ref, kc_ref, vc_ref, arow_ref, col_ref, ng_ref, out_ref,
                  c_ref, hf_ref, hb_ref, *, L, nc):
    p = pl.program_id(1)
    half = nc // 2
    lane = lax.broadcasted_iota(jnp.int32, (L, LANES), 1)
    lo = lane < MLSTM_D
    head_mask = (lo, jnp.logical_not(lo))
    ri = lax.broadcasted_iota(jnp.int32, (L, L), 0)
    ci = lax.broadcasted_iota(jnp.int32, (L, L), 1)
    causal = (ci <= ri, ci >= ri)
    ones_t = jnp.ones((L, LANES), BF16)
    shift = lax.rem(LANES - 16 * p, LANES)

    def cols(off):
        return pltpu.roll(col_ref[pl.ds(off, L), :], shift, 1)

    def col(blk, q, d, hh):
        j = q * 4 + d * 2 + hh
        return blk[:, j:j + 1]

    def vext_of(vb, hh):
        return jnp.concatenate([jnp.where(head_mask[hh], vb, jnp.zeros_like(vb)), ones_t], axis=1)

    def state_update(d, hh, kb, vext, blk, arow, dec):
        last = L - 1 if d == 0 else 0
        khm = jnp.where(head_mask[hh], kb, jnp.zeros_like(kb))
        ws = jnp.exp(arow - col(blk, 0, d, hh)[last:last + 1, :])
        ksc = (khm.T.astype(F32) * ws).astype(BF16)
        upd = _dot(ksc, vext)
        if dec is None:
            c_ref[d, hh] = upd
        else:
            c_ref[d, hh] = dec * c_ref[d, hh] + upd

    def arow_of(d, hh, off):
        return arow_ref[pl.ds(d * 8 + 2 * p + hh, 1), pl.ds(off, L)]

    blk0 = cols(0)
    kcb = kc_ref[...]
    vcb = vc_ref[...]
    for d in range(2):
        for hh in range(2):
            state_update(d, hh, kcb, vext_of(vcb, hh), blk0, arow_of(d, hh, 0), None)

    def compute(d, c):
        t0 = pl.multiple_of(c * L, L)
        off = pl.multiple_of(c * L + L, L)
        qb = q_ref[pl.ds(t0, L), :]
        kb = k_ref[pl.ds(t0, L), :]
        vb = v_ref[pl.ds(t0, L), :]
        blk = cols(off)
        last = L - 1 if d == 0 else 0
        hs = []
        for hh in range(2):
            arow = arow_of(d, hh, off)
            khm = jnp.where(head_mask[hh], kb, jnp.zeros_like(kb))
            s = _dot_nt(qb, khm)
            arg = jnp.where(causal[d], arow - col(blk, 0, d, hh), NEG)
            pm = (s * jnp.exp(arg)).astype(BF16)
            vext = vext_of(vb, hh)
            ch = c_ref[d, hh]
            tot = _dot(pm, vext) + col(blk, 1, d, hh) * _dot(qb, ch.astype(BF16))
            den = jnp.maximum(jnp.abs(tot[:, LANES:]), col(blk, 2, d, hh))
            hs.append(tot[:, :LANES] / den)
            dec = col(blk, 1, d, hh)[last:last + 1, :]
            state_update(d, hh, kb, vext, blk, arow, dec)
        return jnp.where(lo, hs[0], hs[1])

    def finalize(c, hsum):
        t0 = pl.multiple_of(c * L, L)
        sq = hsum * hsum
        s0 = jnp.sum(jnp.where(lo, sq, 0.0), axis=-1, keepdims=True)
        s1 = jnp.sum(jnp.where(lo, 0.0, sq), axis=-1, keepdims=True)
        ms = jnp.where(lo, s0, s1) * (1.0 / MLSTM_D)
        y = hsum * lax.rsqrt(ms + EPS) * ng_ref[...]
        gate = jax.nn.sigmoid(o_ref[pl.ds(t0, L), :].astype(F32))
        out_ref[pl.ds(t0, L), :] = (y * gate).astype(BF16)

    def phase_a(i, carry):
        hf_ref[pl.ds(pl.multiple_of(i * L, L), L), :] = compute(0, i)
        cb = nc - 1 - i
        hb_ref[pl.ds(pl.multiple_of((cb - half) * L, L), L), :] = compute(1, cb)
        return carry

    def phase_b(i, carry):
        hf = compute(0, i)
        finalize(i, hf + hb_ref[pl.ds(pl.multiple_of((i - half) * L, L), L), :])
        cb = nc - 1 - i
        hb = compute(1, cb)
        finalize(cb, hb + hf_ref[pl.ds(pl.multiple_of(cb * L, L), L), :])
        return carry

    lax.fori_loop(0, half, phase_a, 0)
    lax.fori_loop(half, nc, phase_b, 0)


def _mlstm(mq, mk, mv, mo, mkc, mvc, arow, cols, norm_g, *, L):
    b, t, _ = mq.shape
    ctx = mkc.shape[1]
    tt = arow.shape[2]
    nc = t // L
    tok = pl.BlockSpec((None, t, LANES), lambda bi, p: (bi, 0, p))
    ctxs = pl.BlockSpec((None, ctx, LANES), lambda bi, p: (bi, 0, p))
    return pl.pallas_call(
        functools.partial(_mlstm_kernel, L=L, nc=nc),
        grid=(b, 4),
        in_specs=[tok, tok, tok, tok, ctxs, ctxs,
                  pl.BlockSpec((None, 16, tt), lambda bi, p: (bi, 0, 0)),
                  pl.BlockSpec((None, tt, LANES), lambda bi, p: (bi, 0, 0)),
                  pl.BlockSpec((1, LANES), lambda bi, p: (0, p))],
        out_specs=tok,
        out_shape=jax.ShapeDtypeStruct((b, t, 512), BF16),
        scratch_shapes=[pltpu.VMEM((2, 2, LANES, 2 * LANES), F32),
                        pltpu.VMEM((t // 2, LANES), F32),
                        pltpu.VMEM((t // 2, LANES), F32)],
        compiler_params=_cp("arbitrary", "arbitrary"),
        name="mlstm_scan",
    )(mq, mk, mv, mo, mkc, mvc, arow, cols, norm_g.reshape(1, 512))


def _attn_kernel(q_ref, k_ref, v_ref, kc_ref, vc_ref, lam_ref, ng_ref, out_ref, kmax_ref, m_ref, acc_ref,
                 *, tq, kb, nkb, sub, lam_init):
    i = pl.program_id(2)
    lane = lax.broadcasted_iota(jnp.int32, (tq, LANES), 1)
    lo = lane < DIFF_DQK
    rr = lax.broadcasted_iota(jnp.int32, (LANES, LANES), 0)
    cc = lax.broadcasted_iota(jnp.int32, (LANES, LANES), 1)
    same_comp = ((rr < DIFF_DQK) == (cc < DIFF_DQK)).astype(BF16)

    def comp_sqnorm(a):
        af = a.astype(F32)
        return _dot((af * af).astype(BF16), same_comp)

    @pl.when(i == 0)
    def _():
        mx = jnp.max(comp_sqnorm(kc_ref[...]), axis=0, keepdims=True)

        def kbody(j, mx):
            k0 = pl.multiple_of(j * kb, kb)
            return jnp.maximum(mx, jnp.max(comp_sqnorm(k_ref[pl.ds(k0, kb), :]), axis=0, keepdims=True))

        kmax_ref[...] = lax.fori_loop(0, nkb, kbody, mx)

    q = q_ref[...]
    zq = jnp.zeros_like(q)
    bnd = jnp.sqrt(comp_sqnorm(q) * kmax_ref[...]) * 1.02
    b1 = jnp.where(lane == 0, -bnd, 0.0)
    b2 = jnp.where(lane == 0, -pltpu.roll(bnd, DIFF_DQK, 1), 0.0)
    qs = jnp.concatenate(
        [jnp.concatenate([jnp.where(lo, q, zq), jnp.where(lo, zq, q)], axis=0),
         jnp.concatenate([b1, b2], axis=0).astype(BF16)], axis=1)
    n_ctx = kc_ref.shape[0]

    def ext(blk):
        return jnp.concatenate([blk, jnp.ones_like(blk)], axis=1)

    def sum_blocks(kr, vr, k0, n, tot):
        for c in range(0, n, sub):
            pm = jnp.exp2(_dot_nt(qs, ext(kr[pl.ds(k0 + c, sub), :]))).astype(BF16)
            part = _dot(pm, ext(vr[pl.ds(k0 + c, sub), :]))
            tot = part if tot is None else tot + part
        return tot

    def slow_step(kblk, vblk):
        s = _dot_nt(qs, ext(kblk))
        m_old = m_ref[...]
        m_new = jnp.maximum(m_old, jnp.max(s, axis=-1, keepdims=True))
        pm = jnp.exp2(s - m_new).astype(BF16)
        acc_ref[...] = jnp.exp2(m_old - m_new) * acc_ref[...] + _dot(pm, ext(vblk))
        m_ref[...] = m_new

    fast = jnp.max(bnd) <= 56.0

    @pl.when(fast)
    def _():
        tot = sum_blocks(kc_ref, vc_ref, 0, n_ctx, None)
        if nkb == 1:
            acc_ref[...] = sum_blocks(k_ref, v_ref, 0, kb, tot)
        else:
            acc_ref[...] = tot

            def body(j, carry):
                acc_ref[...] += sum_blocks(k_ref, v_ref, pl.multiple_of(j * kb, kb), kb, None)
                return carry

            lax.fori_loop(0, nkb, body, 0)

    @pl.when(jnp.logical_not(fast))
    def _():
        m_ref[...] = jnp.full(m_ref.shape, NEG, F32)
        acc_ref[...] = jnp.zeros(acc_ref.shape, F32)
        slow_step(kc_ref[...], vc_ref[...])

        def body(j, carry):
            k0 = pl.multiple_of(j * sub, sub)
            slow_step(k_ref[pl.ds(k0, sub), :], v_ref[pl.ds(k0, sub), :])
            return carry

        lax.fori_loop(0, (nkb * kb) // sub, body, 0)

    lq = lam_ref[...]
    lam = (jnp.exp(jnp.sum(lq[0:1, :] * lq[1:2, :], axis=-1, keepdims=True))
           - jnp.exp(jnp.sum(lq[2:3, :] * lq[3:4, :], axis=-1, keepdims=True)) + lam_init)
    a1 = acc_ref[0:tq, :]
    a2 = acc_ref[tq:2 * tq, :]
    o = a1[:, :LANES] / a1[:, LANES:] - lam * (a2[:, :LANES] / a2[:, LANES:])
    out_ref[...] = (_rms(o, ng_ref[...]) * (1.0 - lam_init)).astype(BF16)


def _attn(dq, dk, dv, dkc, dvc, lam_in, norm_g, *, tq, sub, max_sub_per_step, lam_init):
    b, t, _ = dq.shape
    n_ctx = dkc.shape[1]
    assert n_ctx % sub == 0 and t % sub == 0
    nsub = t // sub
    per = max(g for g in range(1, max_sub_per_step + 1) if nsub % g == 0)
    kb = per * sub
    full = pl.BlockSpec((None, t, LANES), lambda bi, h, i: (bi, 0, h))
    ctxs = pl.BlockSpec((None, n_ctx, LANES), lambda bi, h, i: (bi, 0, h))
    qs = pl.BlockSpec((None, tq, LANES), lambda bi, h, i: (bi, i, h))
    return pl.pallas_call(
        functools.partial(_attn_kernel, tq=tq, kb=kb, nkb=t // kb, sub=sub, lam_init=lam_init),
        grid=(b, DIFF_HEADS, t // tq),
        in_specs=[qs, full, full, ctxs, ctxs,
                  pl.BlockSpec((8, LANES), lambda bi, h, i: (0, 0)),
                  pl.BlockSpec((1, LANES), lambda bi, h, i: (0, h))],
        out_specs=qs,
        out_shape=jax.ShapeDtypeStruct((b, t, 512), BF16),
        scratch_shapes=[pltpu.VMEM((1, LANES), F32), pltpu.VMEM((2 * tq, 1), F32),
                        pltpu.VMEM((2 * tq, 2 * LANES), F32)],
        compiler_params=_cp("arbitrary", "arbitrary", "arbitrary"),
        name="diff_attn",
    )(dq, dk, dv, dkc, dvc, lam_in, norm_g.reshape(1, 512))


def _swiglu(u, w1_ref, w3_ref, w2_ref):
    return _dot((_silu(_dot(u, w1_ref[...])) * _dot(u, w3_ref[...])).astype(BF16), w2_ref[...])


def _ffn_kernel(hm_ref, hd_ref, wt_ref, wb_ref, mg_ref, fg_ref, w1_ref, w3_ref, w2_ref, x_ref, mod0_ref, mod1_ref,
                pg_ref, ng_ref, pw_ref, pb_ref, x2_ref, h_ref):
    for r0 in range(0, x_ref.shape[0], FFN_ROW_GROUP):
        rows = slice(r0, r0 + FFN_ROW_GROUP)
        mix = _dot(hm_ref[rows, :], wt_ref[...]) + _dot(hd_ref[rows, :], wb_ref[...])
        x1 = x_ref[rows, :] + mod0_ref[2:3, :] * _rms(mix, mg_ref[...])
        u = (_rms(x1, fg_ref[...]) * (1.0 + mod0_ref[4:5, :]) + mod0_ref[3:4, :]).astype(BF16)
        y = _swiglu(u, w1_ref, w3_ref, w2_ref)
        x2 = x1 + mod0_ref[5:6, :] * _rms(y, pg_ref[...])
        x2_ref[rows, :] = x2
        u3 = (_rms(x2, ng_ref[...]) * (1.0 + mod1_ref[1:2, :]) + mod1_ref[0:1, :]).astype(BF16)
        ag = _dot(u3, pw_ref[...]) + pb_ref[...]
        h_ref[rows, :] = ag[:, :D_MODEL] * jax.nn.sigmoid(ag[:, D_MODEL:])


def _ffn(hm, hd, wt, wb, mix_post_g, ffn_pre_g, w1, w3, w2, x, mod0, mod1, post_g, next_pre_g, pw1, pb1, *, tm):
    b, t, d = x.shape
    f = w1.shape[1]
    half = hm.shape[2]
    row = lambda w: pl.BlockSpec((None, tm, w), lambda bi, i: (bi, i, 0))
    modb = pl.BlockSpec((None, 6, d), lambda bi, i: (bi, 0, 0))
    vec = lambda w: pl.BlockSpec((1, w), lambda bi, i: (0, 0))
    wspec = lambda s: pl.BlockSpec(s, lambda bi, i: (0, 0), pipeline_mode=pl.Buffered(1))
    return pl.pallas_call(
        _ffn_kernel,
        grid=(b, t // tm),
        in_specs=[row(half), row(half), wspec((half, d)), wspec((half, d)), vec(d), vec(d),
                  wspec((d, f)), wspec((d, f)), wspec((f, d)), row(d), modb, modb, vec(d), vec(d),
                  wspec((d, 2 * d)), vec(2 * d)],
        out_specs=[row(d), row(d)],
        out_shape=[jax.ShapeDtypeStruct((b, t, d), F32), jax.ShapeDtypeStruct((b, t, d), F32)],
        compiler_params=_cp("arbitrary", "arbitrary"),
        name="outproj_ffn_glu",
    )(hm, hd, wt, wb, mix_post_g.reshape(1, d), ffn_pre_g.reshape(1, d), w1, w3, w2, x, mod0, mod1,
      post_g.reshape(1, d), next_pre_g.reshape(1, d), pw1, pb1)


HALO = 16
CONV_ROW_BLOCK = 128
CONV_LANES = 256
CONV_ROWS_EXTRA = 24


def _conv_kernel(h_ref, hp_ref, hn_ref, dw_ref, dwb_ref, lng_ref, lnb_ref, w2_ref, b2_ref,
                 x_ref, mod_ref, pg_ref, fg_ref, rw_ref, x3_ref, u4_ref, route_ref, routet_ref, cnt_ref,
                 hs_ref, sh_ref, cv_ref, wb_ref, *, tm, nt):
    i = pl.program_id(1)

    hs_ref[0:HALO, :] = jnp.where(i > 0, hp_ref[...], 0.0)
    hs_ref[HALO:HALO + tm, :] = h_ref[...]
    hs_ref[HALO + tm:, :] = jnp.where(i < nt - 1, hn_ref[...], 0.0)

    for r in range(1, 8):
        sh_ref[r - 1] = hs_ref[pl.ds(r, tm + CONV_ROWS_EXTRA), :]

    @pl.when((pl.program_id(0) == 0) & (i == 0))
    def _():
        for j in range(CONV_WIDTH):
            wb_ref[j] = jnp.broadcast_to(dw_ref[j:j + 1, :], (8, D_MODEL))

    nsub = CONV_ROW_BLOCK // 8

    for l0 in range(0, D_MODEL, CONV_LANES):
        def conv_rows(rb, carry, l0=l0):
            r0 = pl.multiple_of(rb * CONV_ROW_BLOCK, CONV_ROW_BLOCK)
            bias = jnp.broadcast_to(dwb_ref[:, l0:l0 + CONV_LANES], (8, CONV_LANES))
            acc = [bias] * nsub
            for j in range(CONV_WIDTH):
                r, a = (j + 1) % 8, (j + 1) // 8
                w = wb_ref[j, :, l0:l0 + CONV_LANES]
                for s in range(nsub):
                    rows = pl.ds(r0 + 8 * (a + s), 8)
                    if r == 0:
                        win = hs_ref[rows, l0:l0 + CONV_LANES]
                    else:
                        win = sh_ref[r - 1, rows, l0:l0 + CONV_LANES]
                    acc[s] = acc[s] + w * win
            for s in range(nsub):
                cv_ref[pl.ds(r0 + 8 * s, 8), l0:l0 + CONV_LANES] = acc[s]
            return carry

        lax.fori_loop(0, tm // CONV_ROW_BLOCK, conv_rows, 0)
    acc = cv_ref[...]
    mu = jnp.mean(acc, axis=-1, keepdims=True)
    cen = acc - mu
    var = jnp.mean(cen * cen, axis=-1, keepdims=True)
    hn = _silu(cen * lax.rsqrt(var + EPS) * lng_ref[...] + lnb_ref[...])
    y = _dot(hn.astype(BF16), w2_ref[...]) + b2_ref[...]
    x3 = x_ref[...] + mod_ref[2:3, :] * _rms(y, pg_ref[...])
    x3_ref[...] = x3
    u4 = _rms(x3, fg_ref[...]) * (1.0 + mod_ref[4:5, :]) + mod_ref[3:4, :]
    u4b = u4.astype(BF16)
    u4_ref[...] = u4b

    lane = lax.broadcasted_iota(jnp.int32, (tm, LANES), 1).astype(F32)
    logits = jnp.where(lane < N_EXPERTS, _dot(u4b, rw_ref[...]), NEG)
    m1 = jnp.max(logits, axis=-1, keepdims=True)
    i1 = jnp.min(jnp.where(logits == m1, lane, float(LANES)), axis=-1, keepdims=True)
    l2 = jnp.where(lane == i1, NEG, logits)
    m2 = jnp.max(l2, axis=-1, keepdims=True)
    i2 = jnp.min(jnp.where(l2 == m2, lane, float(LANES)), axis=-1, keepdims=True)
    e21 = jnp.exp(m2 - m1)
    g1 = 1.0 / (1.0 + e21)
    g2 = e21 * g1
    sel = ((lane == i1) | (lane == i2)).astype(F32)
    ri = lax.broadcasted_iota(jnp.int32, (tm, tm), 0)
    ci = lax.broadcasted_iota(jnp.int32, (tm, tm), 1)
    tri = (ci <= ri).astype(BF16)
    csum = _dot(tri, sel.astype(BF16))
    rank = csum - sel
    r1 = jnp.sum(jnp.where(lane == i1, rank, 0.0), axis=-1, keepdims=True)
    r2 = jnp.sum(jnp.where(lane == i2, rank, 0.0), axis=-1, keepdims=True)
    cnt_ref[...] = jnp.broadcast_to(csum[tm - 1:tm, :], (8, LANES))
    route = jnp.zeros((tm, LANES), F32)
    for n, v in enumerate((i1, i2, g1, g2, r1, r2)):
        route = jnp.where(lane == float(n), v, route)
    routet_ref[...] = route.T[0:8, :]
    gsel = jnp.where(lane == i1, g1, jnp.where(lane == i2, g2, 0.0))
    ghi = gsel.astype(BF16)
    glo = (gsel - ghi.astype(F32)).astype(BF16)
    route_ref[...] = jnp.concatenate([jnp.where(sel > 0.0, rank, -1.0).astype(BF16), ghi, glo], axis=1)


def _convmod(h, dw, dwb, lng, lnb, w2, b2, x2, mod1, post_g, ffn_pre_g, rw, *, tm):
    b, t, d = x2.shape
    nt = t // tm
    hb = tm // HALO
    nbh = t // HALO
    row = lambda w: pl.BlockSpec((None, tm, w), lambda bi, i: (bi, i, 0))
    cst = lambda s: pl.BlockSpec(s, lambda bi, i: (0,) * len(s))
    return pl.pallas_call(
        functools.partial(_conv_kernel, tm=tm, nt=nt),
        grid=(b, nt),
        in_specs=[row(d),
                  pl.BlockSpec((None, HALO, d), lambda bi, i: (bi, jnp.maximum(i * hb - 1, 0), 0)),
                  pl.BlockSpec((None, HALO, d), lambda bi, i: (bi, jnp.minimum((i + 1) * hb, nbh - 1), 0)),
                  cst((32, d)), cst((1, d)), cst((1, d)), cst((1, d)),
                  cst((d, d)), cst((1, d)), row(d),
                  pl.BlockSpec((None, 6, d), lambda bi, i: (bi, 0, 0)), cst((1, d)), cst((1, d)),
                  cst((d, LANES))],
        out_specs=[row(d), row(d), row(3 * LANES),
                   pl.BlockSpec((None, 8, tm), lambda bi, i: (bi * nt + i, 0, 0)),
                   pl.BlockSpec((None, 8, LANES), lambda bi, i: (bi * nt + i, 0, 0))],
        out_shape=[jax.ShapeDtypeStruct((b, t, d), F32), jax.ShapeDtypeStruct((b, t, d), BF16),
                   jax.ShapeDtypeStruct((b, t, 3 * LANES), BF16), jax.ShapeDtypeStruct((b * nt, 8, tm), F32),
                   jax.ShapeDtypeStruct((b * nt, 8, LANES), F32)],
        scratch_shapes=[pltpu.VMEM((tm + 2 * HALO, d), F32), pltpu.VMEM((7, tm + CONV_ROWS_EXTRA, d), F32),
                        pltpu.VMEM((tm, d), F32), pltpu.VMEM((CONV_WIDTH, 8, d), F32)],
        compiler_params=_cp("arbitrary", "arbitrary"),
        name="conv_module_router",
    )(h, h, h, dw, dwb, lng, lnb, w2, b2, x2, mod1, post_g, ffn_pre_g, rw)


def _window_copies(ref_hbm, buf_ref, sem, s0_ref, t, slot, to_hbm):
    out = []
    for e in range(N_EXPERTS):
        hbm = ref_hbm.at[pl.ds(pl.multiple_of(s0_ref[t * N_EXPERTS + e], MOE_ALIGN), MOE_WIN), :]
        vmem = buf_ref.at[slot, e]
        src, dst = (vmem, hbm) if to_hbm else (hbm, vmem)
        out.append(pltpu.make_async_copy(src, dst, sem.at[slot, e]))
    return out


def _dispatch_kernel(s0_ref, hoff_ref, cnt_ref, zoff_ref, nu_ref, u_ref, rt_ref, xs_ref, win_ref, zero_ref,
                     sem, zsem, *, nt, n_blk):
    t = pl.program_id(0)
    slot = lax.rem(t, 2)
    half = MOE_TILE // 2

    @pl.when(t == 0)
    def _():
        zero_ref[...] = jnp.zeros(zero_ref.shape, BF16)
        zc = [pltpu.make_async_copy(
            zero_ref, xs_ref.at[pl.ds(pl.multiple_of(zoff_ref[e], MOE_ALIGN), MOE_WIN), :], zsem.at[e])
            for e in range(N_EXPERTS)]
        for c in zc:
            c.start()
        for c in zc:
            c.wait()

        def clear_block(bk, carry):
            c = pltpu.make_async_copy(
                zero_ref, xs_ref.at[pl.ds(pl.multiple_of(bk * MOE_ROWS, MOE_ROWS), MOE_ROWS), :], zsem.at[0])
            c.start()
            c.wait()
            return carry

        lax.fori_loop(nu_ref[0], n_blk, clear_block, 0)

    rt = jnp.concatenate([rt_ref[0], rt_ref[1]], axis=1)
    e1, e2, r1, r2 = rt[0:1], rt[1:2], rt[4:5], rt[5:6]
    second = lax.broadcasted_iota(jnp.int32, (1, MOE_TILE), 1) >= half
    u = u_ref[...]
    for e in range(N_EXPERTS):
        off = jnp.where(second, hoff_ref[t * N_EXPERTS + e].astype(F32), 0.0)
        lr = jnp.where(e1 == float(e), r1 + off, jnp.where(e2 == float(e), r2 + off, -1.0))
        cnt = cnt_ref[t * N_EXPERTS + e]
        for lo_row, hi_row in zip(MOE_CHUNKS[:-1], MOE_CHUNKS[1:]):
            n_rows = hi_row - lo_row

            def fill(lo_row=lo_row, n_rows=n_rows):
                srow = lax.broadcasted_iota(jnp.int32, (n_rows, MOE_TILE), 0).astype(F32) + float(lo_row)
                win_ref[slot, e, lo_row:lo_row + n_rows, :] = _dot((srow == lr).astype(BF16), u).astype(BF16)

            if lo_row == 0:
                fill()
            else:
                pl.when(cnt > lo_row)(fill)

                @pl.when(cnt <= lo_row)
                def _(lo_row=lo_row, n_rows=n_rows):
                    win_ref[slot, e, lo_row:lo_row + n_rows, :] = jnp.zeros((n_rows, D_MODEL), BF16)

    @pl.when(t > 0)
    def _():
        for c in _window_copies(xs_ref, win_ref, sem, s0_ref, t - 1, 1 - slot, True):
            c.wait()

    for c in _window_copies(xs_ref, win_ref, sem, s0_ref, t, slot, True):
        c.start()

    @pl.when(t == nt - 1)
    def _():
        for c in _window_copies(xs_ref, win_ref, sem, s0_ref, t, slot, True):
            c.wait()


def _dispatch(u4, routet, s0, hoff, cnt, zoff, n_used, *, n_rows):
    assert MOE_WIN == MOE_ROWS
    n, d = u4.shape
    nt = n // MOE_TILE
    per = MOE_TILE // routet.shape[2]
    grid_spec = pltpu.PrefetchScalarGridSpec(
        num_scalar_prefetch=5,
        grid=(nt,),
        in_specs=[pl.BlockSpec((MOE_TILE, d), lambda t, *_: (t, 0)),
                  pl.BlockSpec((per, 8, routet.shape[2]), lambda t, *_: (t, 0, 0))],
        out_specs=pl.BlockSpec(memory_space=pl.ANY),
        scratch_shapes=[pltpu.VMEM((2, N_EXPERTS, MOE_WIN, d), BF16), pltpu.VMEM((MOE_WIN, d), BF16),
                        pltpu.SemaphoreType.DMA((2, N_EXPERTS)), pltpu.SemaphoreType.DMA((N_EXPERTS,))],
    )
    return pl.pallas_call(
        functools.partial(_dispatch_kernel, nt=nt, n_blk=n_rows // MOE_ROWS),
        grid_spec=grid_spec,
        out_shape=jax.ShapeDtypeStruct((n_rows, d), BF16),
        compiler_params=_cp("arbitrary"),
        name="moe_dispatch",
    )(s0, hoff, cnt, zoff, n_used, u4, routet)


def _moe_kernel(be_ref, used_ref, x_ref, w1_ref, w3_ref, w2_ref, y_ref):
    used = used_ref[pl.program_id(0)] > 0

    @pl.when(used)
    def _():
        for r0 in range(0, y_ref.shape[0], FFN_ROW_GROUP):
            rows = slice(r0, r0 + FFN_ROW_GROUP)
            y_ref[rows, :] = _swiglu(x_ref[rows, :], w1_ref, w3_ref, w2_ref).astype(BF16)

    @pl.when(jnp.logical_not(used))
    def _():
        y_ref[...] = jnp.zeros(y_ref.shape, BF16)


def _moe_ffn(xs, block_e, block_used, w1, w3, w2, *, rows):
    n_rows, d = xs.shape
    f = w1.shape[2]
    wspec = lambda s: pl.BlockSpec((None,) + s, lambda i, be, bu: (be[i], 0, 0), pipeline_mode=pl.Buffered(1))
    grid_spec = pltpu.PrefetchScalarGridSpec(
        num_scalar_prefetch=2,
        grid=(n_rows // rows,),
        in_specs=[pl.BlockSpec((rows, d), lambda i, be, bu: (i, 0)),
                  wspec((d, f)), wspec((d, f)), wspec((f, d))],
        out_specs=pl.BlockSpec((rows, d), lambda i, be, bu: (i, 0)),
    )
    return pl.pallas_call(
        _moe_kernel,
        grid_spec=grid_spec,
        out_shape=jax.ShapeDtypeStruct((n_rows, d), BF16),
        compiler_params=_cp("arbitrary"),
        name="moe_ffn",
    )(block_e, block_used, xs, w1, w3, w2)


def _combine_kernel(s0_ref, hoff_ref, cnt_ref, ys_ref, route_ref, x_ref, mod_ref, pg_ref, out_ref,
                    buf_ref, acc_ref, sem, *, nti, nt):
    t = pl.program_id(0) * nti + pl.program_id(1)
    slot = lax.rem(t, 2)
    half = MOE_TILE // 2

    @pl.when(t == 0)
    def _():
        for c in _window_copies(ys_ref, buf_ref, sem, s0_ref, 0, 0, False):
            c.start()

    @pl.when(t + 1 < nt)
    def _():
        for c in _window_copies(ys_ref, buf_ref, sem, s0_ref, t + 1, 1 - slot, False):
            c.start()

    for c in _window_copies(ys_ref, buf_ref, sem, s0_ref, t, slot, False):
        c.wait()

    route = route_ref[...]
    sr = lax.broadcasted_iota(jnp.int32, (LANES, N_EXPERTS * LANES), 0)
    sc = lax.broadcasted_iota(jnp.int32, (LANES, N_EXPERTS * LANES), 1)
    spread = (sc // LANES == sr).astype(BF16)
    rank_b = _dot(route[:, 0:LANES], spread)
    gate_b = _dot(route[:, LANES:2 * LANES], spread) + _dot(route[:, 2 * LANES:3 * LANES], spread)
    second = lax.broadcasted_iota(jnp.int32, (MOE_TILE, LANES), 0) >= half
    scol = lax.broadcasted_iota(jnp.int32, (MOE_TILE, LANES), 1).astype(F32)

    def onehot(lr, first):
        return jnp.concatenate([(scol + float(first + c) == lr) for c in range(0, half, LANES)],
                               axis=1).astype(BF16)

    for e in range(N_EXPERTS):
        rk = rank_b[:, e * LANES:(e + 1) * LANES]
        off = jnp.where(second, hoff_ref[t * N_EXPERTS + e].astype(F32), 0.0)
        lr = jnp.where(rk < 0.0, -1.0, rk + off)
        ge = jnp.concatenate([gate_b[:, e * LANES:(e + 1) * LANES]] * (D_MODEL // LANES), axis=1)
        z = ge * _dot(onehot(lr, 0), buf_ref[slot, e, 0:half, :])
        if e == 0:
            acc_ref[...] = z
        else:
            acc_ref[...] += z

        @pl.when(cnt_ref[t * N_EXPERTS + e] > half)
        def _():
            acc_ref[...] += ge * _dot(onehot(lr, half), buf_ref[slot, e, half:MOE_WIN, :])

    out_ref[...] = x_ref[...] + mod_ref[5:6, :] * _rms(acc_ref[...], pg_ref[...])


def _combine(s0, hoff, cnt, ys, route, x3, mod1, post_g):
    b, t, d = x3.shape
    nti = t // MOE_TILE
    row = lambda w: pl.BlockSpec((None, MOE_TILE, w), lambda bi, i, *_: (bi, i, 0))
    grid_spec = pltpu.PrefetchScalarGridSpec(
        num_scalar_prefetch=3,
        grid=(b, nti),
        in_specs=[pl.BlockSpec(memory_space=pl.ANY), row(3 * LANES), row(d),
                  pl.BlockSpec((None, 6, d), lambda bi, i, *_: (bi, 0, 0)),
                  pl.BlockSpec((1, d), lambda bi, i, *_: (0, 0))],
        out_specs=row(d),
        scratch_shapes=[pltpu.VMEM((2, N_EXPERTS, MOE_WIN, d), BF16), pltpu.VMEM((MOE_TILE, d), F32),
                        pltpu.SemaphoreType.DMA((2, N_EXPERTS))],
    )
    return pl.pallas_call(
        functools.partial(_combine_kernel, nti=nti, nt=b * nti),
        grid_spec=grid_spec,
        out_shape=jax.ShapeDtypeStruct((b, t, d), F32),
        compiler_params=_cp("arbitrary", "arbitrary"),
        name="moe_combine",
    )(s0, hoff, cnt, ys, route, x3, mod1, post_g.reshape(1, d))


def _rope_tables(t_len):
    f32 = np.float32
    rows = t_len // GRID_W
    row = np.repeat(np.arange(rows, dtype=f32), GRID_W)
    col = np.tile(np.arange(GRID_W, dtype=f32), rows)
    axis_dim = DIFF_DQK // 2
    inv = (f32(ROPE_BASE) ** (-np.arange(0, axis_dim, 2, dtype=f32) / f32(axis_dim))).astype(f32)
    ang_r = (row[:, None] * inv).astype(f32)
    ang_c = (col[:, None] * inv).astype(f32)
    cr, sr, cc, sc = np.cos(ang_r), np.sin(ang_r), np.cos(ang_c), np.sin(ang_c)
    cos = np.concatenate([cr, cr, cc, cc] * 2, axis=1).astype(f32)
    sin = np.concatenate([-sr, sr, -sc, sc] * 2, axis=1).astype(f32)
    return jnp.asarray(cos), jnp.asarray(sin)


def kernel(x, c, ctx, c_ctx, l0_mod_w, l0_mod_b, l0_mix_pre_g, l0_mix_post_g, l0_w_in, l0_mlstm_gate_b, l0_mlstm_conv_w, l0_mlstm_norm_g, l0_lambda_q1, l0_lambda_k1, l0_lambda_q2, l0_lambda_k2, l0_diff_norm_g, l0_w_out, l0_ffn_pre_g, l0_ffn_post_g, l0_ffn_w1, l0_ffn_w3, l0_ffn_w2, l1_mod_w, l1_mod_b, l1_mix_pre_g, l1_mix_post_g, l1_conv_pw1_w, l1_conv_pw1_b, l1_conv_dw_w, l1_conv_dw_b, l1_conv_ln_g, l1_conv_ln_b, l1_conv_pw2_w, l1_conv_pw2_b, l1_ffn_pre_g, l1_ffn_post_g, l1_router_w, l1_moe_w1, l1_moe_w3, l1_moe_w2):
    b, t, d = x.shape
    n_ctx = ctx.shape[1]
    L = MLSTM_CHUNK
    assert d == D_MODEL and n_ctx == L and t % (2 * L) == 0 and b <= 8

    cpad = jnp.zeros((16, d), F32).at[:b].set(c).at[8].set(c_ctx)
    mod0 = _adaln(cpad, l0_mod_w, l0_mod_b).reshape(16, 6, d)
    mod1 = _adaln(cpad, l1_mod_w, l1_mod_b).reshape(16, 6, d)

    w_perm = jnp.concatenate(
        [l0_w_in[:, :2048], l0_w_in[:, 2080:], l0_w_in[:, 2048:2080], jnp.zeros((d, 96), F32)], axis=1).astype(BF16)
    gate_b = jnp.concatenate([l0_mlstm_gate_b, jnp.zeros((96,), F32)]).reshape(1, LANES)
    cos, sin = _rope_tables(t)
    g0 = l0_mix_pre_g.reshape(1, d)
    mq, mk, mv, mo, dq, dk, dv, gates = _inproj(x, mod0, None, g0, w_perm, l0_mlstm_conv_w, gate_b, cos, sin,
                                                 tm=ROW_TILE, rope=True)
    _, mkc, mvc, _, _, dkc, dvc, gates_c = _inproj(ctx, mod0, 8, g0, w_perm, l0_mlstm_conv_w, gate_b,
                                                   cos[:n_ctx], sin[:n_ctx], tm=n_ctx, rope=False)

    arow, cols = _gateprep(gates_c, gates, L=L)
    hm = _mlstm(mq, mk, mv, mo, mkc, mvc, arow, cols, l0_mlstm_norm_g, L=L)

    lam_init = 0.8 - 0.6 * math.exp(-0.3 * 0)
    lam_in = jnp.zeros((8, LANES), F32).at[0, :DIFF_DQK].set(l0_lambda_q1).at[1, :DIFF_DQK].set(l0_lambda_k1)
    lam_in = lam_in.at[2, :DIFF_DQK].set(l0_lambda_q2).at[3, :DIFF_DQK].set(l0_lambda_k2)
    hd = _attn(dq, dk, dv, dkc, dvc, lam_in, l0_diff_norm_g, tq=min(ATTN_Q_TILE, t), sub=MXU_DIM,
               max_sub_per_step=t // MXU_DIM, lam_init=lam_init)

    w_out = l0_w_out.astype(BF16)
    x2, h1 = _ffn(hm, hd, w_out[:512], w_out[512:], l0_mix_post_g, l0_ffn_pre_g,
                  l0_ffn_w1.astype(BF16), l0_ffn_w3.astype(BF16), l0_ffn_w2.astype(BF16), x, mod0, mod1,
                  l0_ffn_post_g, l1_mix_pre_g, l1_conv_pw1_w.astype(BF16), l1_conv_pw1_b.reshape(1, -1),
                  tm=ROW_TILE)

    dw = jnp.concatenate([l1_conv_dw_w, jnp.zeros((1, d), F32)], axis=0)
    rw = jnp.concatenate([l1_router_w, jnp.zeros((d, LANES - N_EXPERTS), F32)], axis=1).astype(BF16)
    v1 = lambda a: a.reshape(1, -1)
    conv_tm = MOE_TILE // 2
    x3, u4, route, routet, cnt_tile = _convmod(
        h1, dw, v1(l1_conv_dw_b), v1(l1_conv_ln_g),
        v1(l1_conv_ln_b), l1_conv_pw2_w.astype(BF16), v1(l1_conv_pw2_b), x2, mod1, v1(l1_mix_post_g),
        v1(l1_ffn_pre_g), rw, tm=conv_tm)

    n = b * t
    rows = MOE_ROWS
    nt = n // MOE_TILE
    cnt_half = cnt_tile[:, 0, :N_EXPERTS].astype(jnp.int32).reshape(nt, 2, N_EXPERTS)
    cnt = cnt_half[:, 0] + cnt_half[:, 1]
    aligned = ((cnt + MOE_ALIGN - 1) // MOE_ALIGN) * MOE_ALIGN
    base = jnp.cumsum(aligned, axis=0) - aligned
    cap = ((base[-1] + MOE_WIN + rows - 1) // rows) * rows
    pend = jnp.cumsum(cap)
    s0 = (pend - cap)[None, :] + base
    n_blk = (2 * n + N_EXPERTS * (MOE_ALIGN * nt + MOE_WIN + rows)) // rows + 1
    blk_row = jnp.arange(n_blk, dtype=jnp.int32) * rows
    block_e = jnp.minimum(jnp.sum((blk_row[:, None] >= pend[None, :]).astype(jnp.int32), axis=1), N_EXPERTS - 1)
    n_used = (pend[-1:] // rows).astype(jnp.int32)
    slot_end = (pend - cap) + base[-1] + aligned[-1]
    block_used = ((blk_row < slot_end[block_e]) & (blk_row < pend[-1])).astype(jnp.int32)
    flat = lambda a: a.reshape(-1).astype(jnp.int32)

    xs = _dispatch(u4.reshape(n, d), routet, flat(s0), flat(cnt_half[:, 0]), flat(cnt), flat(pend - MOE_WIN),
                   n_used, n_rows=n_blk * rows)
    ys = _moe_ffn(xs, block_e, block_used, l1_moe_w1.astype(BF16), l1_moe_w3.astype(BF16),
                  l1_moe_w2.astype(BF16), rows=rows)
    return _combine(flat(s0), flat(cnt_half[:, 0]), flat(cnt), ys, route, x3, mod1, l1_ffn_post_g)
```

```python
import functools
import math

import jax
import jax.numpy as jnp
import numpy as np
from jax import lax
from jax.experimental import pallas as pl
from jax.experimental.pallas import tpu as pltpu

F32 = jnp.float32
BF16 = jnp.bfloat16
EPS = 1e-6
NEG = -1e30
LOG2E = 1.4426950408889634

D_MODEL = 1024
GRID_W = 64
MLSTM_HEADS = 8
MLSTM_D = 64
MLSTM_CHUNK = 256
DIFF_HEADS = 4
DIFF_DQK = 64
ROPE_BASE = 10000.0
CONV_WIDTH = 31
N_EXPERTS = 8
MOE_ROWS = 512
MOE_TILE = 512
MOE_WIN = 512
MOE_ALIGN = 16
MOE_CHUNKS = (0, 160, 256, 512)
LANES = 128
MXU_DIM = 256
V7X_VMEM_BYTES = 64 * 1024 * 1024
VMEM_LIMIT = V7X_VMEM_BYTES - 12 * 1024 * 1024

ROW_TILE = 512
ATTN_Q_TILE = 1024
FFN_ROW_GROUP = 512


def _cp(*sem):
    return pltpu.CompilerParams(dimension_semantics=sem, vmem_limit_bytes=VMEM_LIMIT)


def _rms(x, g):
    return x * lax.rsqrt(jnp.mean(x * x, axis=-1, keepdims=True) + EPS) * g


def _silu(x):
    return x * jax.nn.sigmoid(x)


def _dot(a, b):
    return jnp.dot(a, b, preferred_element_type=F32)


def _dot_nt(a, b):
    return lax.dot_general(a, b, (((1,), (1,)), ((), ())), preferred_element_type=F32)


def _adaln_kernel(c_ref, w_ref, b_ref, o_ref):
    s = _silu(c_ref[...])
    o_ref[...] = _dot(s.astype(BF16), w_ref[...].astype(BF16)) + b_ref[...]


def _adaln(cpad, w, b):
    rows, d = cpad.shape
    n = w.shape[1]
    tn = 1536
    return pl.pallas_call(
        _adaln_kernel,
        grid=(n // tn,),
        in_specs=[pl.BlockSpec((rows, d), lambda j: (0, 0)),
                  pl.BlockSpec((d, tn), lambda j: (0, j)),
                  pl.BlockSpec((1, tn), lambda j: (0, j))],
        out_specs=pl.BlockSpec((rows, tn), lambda j: (0, j)),
        out_shape=jax.ShapeDtypeStruct((rows, n), F32),
        compiler_params=_cp("arbitrary"),
        name="adaln",
    )(cpad, w, b.reshape(1, n))


def _inproj_kernel(x_ref, xp_ref, xn_ref, mod_ref, g_ref, w_ref, cw_ref, gb_ref, cos_ref, sin_ref,
                   mq_ref, mk_ref, mv_ref, mo_ref, dq_ref, dk_ref, dv_ref, gt_ref, pext_ref,
                   *, tm, nt, rope):
    i = pl.program_id(1)
    g = g_ref[...]
    sh = mod_ref[0:1, :]
    sc = mod_ref[1:2, :]

    def mod(xv):
        return _rms(xv, g) * (1.0 + sc) + sh

    u = mod(x_ref[...])
    up = jnp.where(i > 0, mod(xp_ref[...]), 0.0)
    un = jnp.where(i < nt - 1, mod(xn_ref[...]), 0.0)
    ub = u.astype(BF16)
    uext = jnp.concatenate([up.astype(BF16), ub, un.astype(BF16)], axis=0)

    pext_ref[...] = _dot(uext, w_ref[:, 0:1024])
    cw = cw_ref[...]
    conv = (cw[0:1, :] * pext_ref[pl.ds(7, tm), :] + cw[1:2, :] * pext_ref[pl.ds(8, tm), :]
            + cw[2:3, :] * pext_ref[pl.ds(9, tm), :])
    act = _silu(conv)
    mq_ref[...] = (act[:, 0:512] * (MLSTM_D ** -0.5)).astype(BF16)
    mk_ref[...] = act[:, 512:1024].astype(BF16)

    p = _dot(ub, w_ref[:, 1024:2048])
    mv_ref[...] = p[:, 0:512].astype(BF16)
    mo_ref[...] = p[:, 512:1024].astype(BF16)

    p = _dot(ub, w_ref[:, 2048:3072])
    if rope:
        lane = lax.broadcasted_iota(jnp.int32, p.shape, 1)
        first_half = ((lane // 16) % 2) == 0
        nl = p.shape[1]
        partner = jnp.where(first_half, pltpu.roll(p, nl - 16, 1), pltpu.roll(p, 16, 1))
        cos = jnp.concatenate([cos_ref[...]] * 8, axis=1)
        sin = jnp.concatenate([sin_ref[...]] * 8, axis=1)
        p = p * cos + partner * sin
    dq_ref[...] = (p[:, 0:512] * (LOG2E * DIFF_DQK ** -0.5)).astype(BF16)
    dk_ref[...] = p[:, 512:1024].astype(BF16)

    p = _dot(ub, w_ref[:, 3072:3712])
    dv_ref[...] = p[:, 0:512].astype(BF16)
    gt_ref[...] = p[:, 512:640] + gb_ref[...]


def _inproj(x, mod, mod_row, g, w_perm, conv_w, gate_b, cos, sin, *, tm, rope):
    b, t, d = x.shape
    nt = t // tm
    hb = tm // 8
    nb8 = t // 8
    bf = lambda: jax.ShapeDtypeStruct((b, t, 512), BF16)
    if mod_row is None:
        mod_map = lambda bi, i: (bi, 0, 0)
    else:
        mod_map = lambda bi, i: (mod_row, 0, 0)
    kern = functools.partial(_inproj_kernel, tm=tm, nt=nt, rope=rope)
    o512 = pl.BlockSpec((None, tm, 512), lambda bi, i: (bi, i, 0))
    return pl.pallas_call(
        kern,
        grid=(b, nt),
        in_specs=[
            pl.BlockSpec((None, tm, d), lambda bi, i: (bi, i, 0)),
            pl.BlockSpec((None, 8, d), lambda bi, i: (bi, jnp.maximum(i * hb - 1, 0), 0)),
            pl.BlockSpec((None, 8, d), lambda bi, i: (bi, jnp.minimum((i + 1) * hb, nb8 - 1), 0)),
            pl.BlockSpec((None, 6, d), mod_map),
            pl.BlockSpec((1, d), lambda bi, i: (0, 0)),
            pl.BlockSpec((d, 3712), lambda bi, i: (0, 0)),
            pl.BlockSpec((3, d), lambda bi, i: (0, 0)),
            pl.BlockSpec((1, LANES), lambda bi, i: (0, 0)),
            pl.BlockSpec((tm, LANES), lambda bi, i: (i, 0)),
            pl.BlockSpec((tm, LANES), lambda bi, i: (i, 0)),
        ],
        out_specs=[o512] * 7 + [pl.BlockSpec((None, tm, LANES), lambda bi, i: (bi, i, 0))],
        out_shape=[bf() for _ in range(7)] + [jax.ShapeDtypeStruct((b, t, LANES), F32)],
        scratch_shapes=[pltpu.VMEM((tm + 16, 1024), F32)],
        compiler_params=_cp("arbitrary", "arbitrary"),
        name="inproj_rope" if rope else "inproj_ctx",
    )(x, x, x, mod, g, w_perm, conv_w, gate_b, cos, sin)


def _gateprep_kernel(gc_ref, g_ref, arow_ref, col_ref, *, L, nch):
    tt = L * nch
    gt = jnp.concatenate([gc_ref[...].T, g_ref[...].T], axis=1)
    i_f, f_f, i_b, f_b = gt[0:8], gt[8:16], gt[16:24], gt[24:32]

    def logsig(v):
        return jnp.minimum(v, 0.0) - jnp.log1p(jnp.exp(-jnp.abs(v)))

    pos = lax.broadcasted_iota(jnp.int32, (8, tt), 1) % L

    def scan(v, op, ident, reverse):
        s = 1
        while s < L:
            if reverse:
                shifted = pltpu.roll(v, tt - s, 1)
                valid = pos < L - s
            else:
                shifted = pltpu.roll(v, s, 1)
                valid = pos >= s
            v = op(v, jnp.where(valid, shifted, ident))
            s *= 2
        return v

    outs = []
    for d, (ig, fg) in enumerate(((i_f, f_f), (i_b, f_b))):
        rev = d == 1
        bcum = scan(logsig(fg), jnp.add, 0.0, rev)
        a = ig - bcum
        cm = scan(a, jnp.maximum, NEG, rev)
        order = list(range(nch)) if not rev else [0] + list(range(nch - 1, 0, -1))
        mp = jnp.zeros((8, 1), F32)
        mp_c = [None] * nch
        for j in order:
            e = j * L if rev else j * L + L - 1
            mp_c[j] = jnp.broadcast_to(mp, (8, L))
            mp = bcum[:, e:e + 1] + jnp.maximum(mp, cm[:, e:e + 1])
        mprev = jnp.concatenate(mp_c, axis=1)
        m = jnp.maximum(mprev, cm)
        outs.append((a, m, jnp.exp(mprev - m), jnp.exp(-(bcum + m))))

    arow_ref[...] = jnp.concatenate([outs[0][0], outs[1][0]], axis=0)
    rows = []
    for pair in range(4):
        for q in range(1, 5):
            for d in range(2):
                rows.append(outs[d][q][2 * pair:2 * pair + 2] if q < 4 else jnp.zeros((2, tt), F32))
    rows.append(jnp.zeros((64, tt), F32))
    col_ref[...] = jnp.concatenate(rows, axis=0).T


def _gateprep(gates_c, gates, *, L):
    b, t, _ = gates.shape
    n_ctx = gates_c.shape[1]
    tt = n_ctx + t
    nch = tt // L
    return pl.pallas_call(
        functools.partial(_gateprep_kernel, L=L, nch=nch),
        grid=(b,),
        in_specs=[pl.BlockSpec((None, n_ctx, LANES), lambda bi: (bi, 0, 0)),
                  pl.BlockSpec((None, t, LANES), lambda bi: (bi, 0, 0))],
        out_specs=[pl.BlockSpec((None, 16, tt), lambda bi: (bi, 0, 0)),
                   pl.BlockSpec((None, tt, LANES), lambda bi: (bi, 0, 0))],
        out_shape=[jax.ShapeDtypeStruct((b, 16, tt), F32), jax.ShapeDtypeStruct((b, tt, LANES), F32)],
        compiler_params=_cp("arbitrary"),
        name="mlstm_gateprep",
    )(gates_c, gates)


def _mlstm_kernel(q_ref, k_ref, v_ref, o_ref, kc_ref, vc_ref, arow_ref, col_ref, ng_ref, out_ref,
                  c_ref, hf_ref, hb_ref, *, L, nc):
    p = pl.program_id(1)
    half = nc // 2
    lane = lax.broadcasted_iota(jnp.int32, (L, LANES), 1)
    lo = lane < MLSTM_D
    head_mask = (lo, jnp.logical_not(lo))
    ri = lax.broadcasted_iota(jnp.int32, (L, L), 0)
    ci = lax.broadcasted_iota(jnp.int32, (L, L), 1)
    causal = (ci <= ri, ci >= ri)
    ones_t = jnp.ones((L, LANES), BF16)
    shift = lax.rem(LANES - 16 * p, LANES)

    def cols(off):
        return pltpu.roll(col_ref[pl.ds(off, L), :], shift, 1)

    def col(blk, q, d, hh):
        j = q * 4 + d * 2 + hh
        return blk[:, j:j + 1]

    def vext_of(vb, hh):
        return jnp.concatenate([jnp.where(head_mask[hh], vb, jnp.zeros_like(vb)), ones_t], axis=1)

    def state_update(d, hh, kb, vext, blk, arow, dec):
        last = L - 1 if d == 0 else 0
        khm = jnp.where(head_mask[hh], kb, jnp.zeros_like(kb))
        ws = jnp.exp(arow - col(blk, 0, d, hh)[last:last + 1, :])
        ksc = (khm.T.astype(F32) * ws).astype(BF16)
        upd = _dot(ksc, vext)
        if dec is None:
            c_ref[d, hh] = upd
        else:
            c_ref[d, hh] = dec * c_ref[d, hh] + upd

    def arow_of(d, hh, off):
        return arow_ref[pl.ds(d * 8 + 2 * p + hh, 1), pl.ds(off, L)]

    blk0 = cols(0)
    kcb = kc_ref[...]
    vcb = vc_ref[...]
    for d in range(2):
        for hh in range(2):
            state_update(d, hh, kcb, vext_of(vcb, hh), blk0, arow_of(d, hh, 0), None)

    def compute(d, c):
        t0 = pl.multiple_of(c * L, L)
        off = pl.multiple_of(c * L + L, L)
        qb = q_ref[pl.ds(t0, L), :]
        kb = k_ref[pl.ds(t0, L), :]
        vb = v_ref[pl.ds(t0, L), :]
        blk = cols(off)
        last = L - 1 if d == 0 else 0
        hs = []
        for hh in range(2):
            arow = arow_of(d, hh, off)
            khm = jnp.where(head_mask[hh], kb, jnp.zeros_like(kb))
            s = _dot_nt(qb, khm)
            arg = jnp.where(causal[d], arow - col(blk, 0, d, hh), NEG)
            pm = (s * jnp.exp(arg)).astype(BF16)
            vext = vext_of(vb, hh)
            ch = c_ref[d, hh]
            tot = _dot(pm, vext) + col(blk, 1, d, hh) * _dot(qb, ch.astype(BF16))
            den = jnp.maximum(jnp.abs(tot[:, LANES:]), col(blk, 2, d, hh))
            hs.append(tot[:, :LANES] / den)
            dec = col(blk, 1, d, hh)[last:last + 1, :]
            state_update(d, hh, kb, vext, blk, arow, dec)
        return jnp.where(lo, hs[0], hs[1])

    def finalize(c, hsum):
        t0 = pl.multiple_of(c * L, L)
        sq = hsum * hsum
        s0 = jnp.sum(jnp.where(lo, sq, 0.0), axis=-1, keepdims=True)
        s1 = jnp.sum(jnp.where(lo, 0.0, sq), axis=-1, keepdims=True)
        ms = jnp.where(lo, s0, s1) * (1.0 / MLSTM_D)
        y = hsum * lax.rsqrt(ms + EPS) * ng_ref[...]
        gate = jax.nn.sigmoid(o_ref[pl.ds(t0, L), :].astype(F32))
        out_ref[pl.ds(t0, L), :] = (y * gate).astype(BF16)

    def phase_a(i, carry):
        hf_ref[pl.ds(pl.multiple_of(i * L, L), L), :] = compute(0, i)
        cb = nc - 1 - i
        hb_ref[pl.ds(pl.multiple_of((cb - half) * L, L), L), :] = compute(1, cb)
        return carry

    def phase_b(i, carry):
        hf = compute(0, i)
        finalize(i, hf + hb_ref[pl.ds(pl.multiple_of((i - half) * L, L), L), :])
        cb = nc - 1 - i
        hb = compute(1, cb)
        finalize(cb, hb + hf_ref[pl.ds(pl.multiple_of(cb * L, L), L), :])
        return carry

    lax.fori_loop(0, half, phase_a, 0)
    lax.fori_loop(half, nc, phase_b, 0)


def _mlstm(mq, mk, mv, mo, mkc, mvc, arow, cols, norm_g, *, L):
    b, t, _ = mq.shape
    ctx = mkc.shape[1]
    tt = arow.shape[2]
    nc = t // L
    tok = pl.BlockSpec((None, t, LANES), lambda bi, p: (bi, 0, p))
    ctxs = pl.BlockSpec((None, ctx, LANES), lambda bi, p: (bi, 0, p))
    return pl.pallas_call(
        functools.partial(_mlstm_kernel, L=L, nc=nc),
        grid=(b, 4),
        in_specs=[tok, tok, tok, tok, ctxs, ctxs,
                  pl.BlockSpec((None, 16, tt), lambda bi, p: (bi, 0, 0)),
                  pl.BlockSpec((None, tt, LANES), lambda bi, p: (bi, 0, 0)),
                  pl.BlockSpec((1, LANES), lambda bi, p: (0, p))],
        out_specs=tok,
        out_shape=jax.ShapeDtypeStruct((b, t, 512), BF16),
        scratch_shapes=[pltpu.VMEM((2, 2, LANES, 2 * LANES), F32),
                        pltpu.VMEM((t // 2, LANES), F32),
                        pltpu.VMEM((t // 2, LANES), F32)],
        compiler_params=_cp("arbitrary", "arbitrary"),
        name="mlstm_scan",
    )(mq, mk, mv, mo, mkc, mvc, arow, cols, norm_g.reshape(1, 512))


def _attn_kernel(q_ref, k_ref, v_ref, kc_ref, vc_ref, lam_ref, ng_ref, out_ref, kmax_ref, m_ref, acc_ref,
                 *, tq, kb, nkb, sub, lam_init):
    i = pl.program_id(2)
    lane = lax.broadcasted_iota(jnp.int32, (tq, LANES), 1)
    lo = lane < DIFF_DQK
    rr = lax.broadcasted_iota(jnp.int32, (LANES, LANES), 0)
    cc = lax.broadcasted_iota(jnp.int32, (LANES, LANES), 1)
    same_comp = ((rr < DIFF_DQK) == (cc < DIFF_DQK)).astype(BF16)

    def comp_sqnorm(a):
        af = a.astype(F32)
        return _dot((af * af).astype(BF16), same_comp)

    @pl.when(i == 0)
    def _():
        mx = jnp.max(comp_sqnorm(kc_ref[...]), axis=0, keepdims=True)

        def kbody(j, mx):
            k0 = pl.multiple_of(j * kb, kb)
            return jnp.maximum(mx, jnp.max(comp_sqnorm(k_ref[pl.ds(k0, kb), :]), axis=0, keepdims=True))

        kmax_ref[...] = lax.fori_loop(0, nkb, kbody, mx)

    q = q_ref[...]
    zq = jnp.zeros_like(q)
    bnd = jnp.sqrt(comp_sqnorm(q) * kmax_ref[...]) * 1.02
    b1 = jnp.where(lane == 0, -bnd, 0.0)
    b2 = jnp.where(lane == 0, -pltpu.roll(bnd, DIFF_DQK, 1), 0.0)
    qs = jnp.concatenate(
        [jnp.concatenate([jnp.where(lo, q, zq), jnp.where(lo, zq, q)], axis=0),
         jnp.concatenate([b1, b2], axis=0).astype(BF16)], axis=1)
    n_ctx = kc_ref.shape[0]

    def ext(blk):
        return jnp.concatenate([blk, jnp.ones_like(blk)], axis=1)

    def sum_blocks(kr, vr, k0, n, tot):
        for c in range(0, n, sub):
            pm = jnp.exp2(_dot_nt(qs, ext(kr[pl.ds(k0 + c, sub), :]))).astype(BF16)
            part = _dot(pm, ext(vr[pl.ds(k0 + c, sub), :]))
            tot = part if tot is None else tot + part
        return tot

    def slow_step(kblk, vblk):
        s = _dot_nt(qs, ext(kblk))
        m_old = m_ref[...]
        m_new = jnp.maximum(m_old, jnp.max(s, axis=-1, keepdims=True))
        pm = jnp.exp2(s - m_new).astype(BF16)
        acc_ref[...] = jnp.exp2(m_old - m_new) * acc_ref[...] + _dot(pm, ext(vblk))
        m_ref[...] = m_new

    fast = jnp.max(bnd) <= 56.0

    @pl.when(fast)
    def _():
        tot = sum_blocks(kc_ref, vc_ref, 0, n_ctx, None)
        if nkb == 1:
            acc_ref[...] = sum_blocks(k_ref, v_ref, 0, kb, tot)
        else:
            acc_ref[...] = tot

            def body(j, carry):
                acc_ref[...] += sum_blocks(k_ref, v_ref, pl.multiple_of(j * kb, kb), kb, None)
                return carry

            lax.fori_loop(0, nkb, body, 0)

    @pl.when(jnp.logical_not(fast))
    def _():
        m_ref[...] = jnp.full(m_ref.shape, NEG, F32)
        acc_ref[...] = jnp.zeros(acc_ref.shape, F32)
        slow_step(kc_ref[...], vc_ref[...])

        def body(j, carry):
            k0 = pl.multiple_of(j * sub, sub)
            slow_step(k_ref[pl.ds(k0, sub), :], v_ref[pl.ds(k0, sub), :])
            return carry

        lax.fori_loop(0, (nkb * kb) // sub, body, 0)

    lq = lam_ref[...]
    lam = (jnp.exp(jnp.sum(lq[0:1, :] * lq[1:2, :], axis=-1, keepdims=True))
           - jnp.exp(jnp.sum(lq[2:3, :] * lq[3:4, :], axis=-1, keepdims=True)) + lam_init)
    a1 = acc_ref[0:tq, :]
    a2 = acc_ref[tq:2 * tq, :]
    o = a1[:, :LANES] / a1[:, LANES:] - lam * (a2[:, :LANES] / a2[:, LANES:])
    out_ref[...] = (_rms(o, ng_ref[...]) * (1.0 - lam_init)).astype(BF16)


def _attn(dq, dk, dv, dkc, dvc, lam_in, norm_g, *, tq, sub, max_sub_per_step, lam_init):
    b, t, _ = dq.shape
    n_ctx = dkc.shape[1]
    assert n_ctx % sub == 0 and t % sub == 0
    nsub = t // sub
    per = max(g for g in range(1, max_sub_per_step + 1) if nsub % g == 0)
    kb = per * sub
    full = pl.BlockSpec((None, t, LANES), lambda bi, h, i: (bi, 0, h))
    ctxs = pl.BlockSpec((None, n_ctx, LANES), lambda bi, h, i: (bi, 0, h))
    qs = pl.BlockSpec((None, tq, LANES), lambda bi, h, i: (bi, i, h))
    return pl.pallas_call(
        functools.partial(_attn_kernel, tq=tq, kb=kb, nkb=t // kb, sub=sub, lam_init=lam_init),
        grid=(b, DIFF_HEADS, t // tq),
        in_specs=[qs, full, full, ctxs, ctxs,
                  pl.BlockSpec((8, LANES), lambda bi, h, i: (0, 0)),
                  pl.BlockSpec((1, LANES), lambda bi, h, i: (0, h))],
        out_specs=qs,
        out_shape=jax.ShapeDtypeStruct((b, t, 512), BF16),
        scratch_shapes=[pltpu.VMEM((1, LANES), F32), pltpu.VMEM((2 * tq, 1), F32),
                        pltpu.VMEM((2 * tq, 2 * LANES), F32)],
        compiler_params=_cp("arbitrary", "arbitrary", "arbitrary"),
        name="diff_attn",
    )(dq, dk, dv, dkc, dvc, lam_in, norm_g.reshape(1, 512))


def _swiglu(u, w1_ref, w3_ref, w2_ref):
    return _dot((_silu(_dot(u, w1_ref[...])) * _dot(u, w3_ref[...])).astype(BF16), w2_ref[...])


def _ffn_kernel(hm_ref, hd_ref, wt_ref, wb_ref, mg_ref, fg_ref, w1_ref, w3_ref, w2_ref, x_ref, mod0_ref, mod1_ref,
                pg_ref, ng_ref, pw_ref, pb_ref, x2_ref, h_ref):
    for r0 in range(0, x_ref.shape[0], FFN_ROW_GROUP):
        rows = slice(r0, r0 + FFN_ROW_GROUP)
        mix = _dot(hm_ref[rows, :], wt_ref[...]) + _dot(hd_ref[rows, :], wb_ref[...])
        x1 = x_ref[rows, :] + mod0_ref[2:3, :] * _rms(mix, mg_ref[...])
        u = (_rms(x1, fg_ref[...]) * (1.0 + mod0_ref[4:5, :]) + mod0_ref[3:4, :]).astype(BF16)
        y = _swiglu(u, w1_ref, w3_ref, w2_ref)
        x2 = x1 + mod0_ref[5:6, :] * _rms(y, pg_ref[...])
        x2_ref[rows, :] = x2
        u3 = (_rms(x2, ng_ref[...]) * (1.0 + mod1_ref[1:2, :]) + mod1_ref[0:1, :]).astype(BF16)
        ag = _dot(u3, pw_ref[...]) + pb_ref[...]
        h_ref[rows, :] = ag[:, :D_MODEL] * jax.nn.sigmoid(ag[:, D_MODEL:])


def _ffn(hm, hd, wt, wb, mix_post_g, ffn_pre_g, w1, w3, w2, x, mod0, mod1, post_g, next_pre_g, pw1, pb1, *, tm):
    b, t, d = x.shape
    f = w1.shape[1]
    half = hm.shape[2]
    row = lambda w: pl.BlockSpec((None, tm, w), lambda bi, i: (bi, i, 0))
    modb = pl.BlockSpec((None, 6, d), lambda bi, i: (bi, 0, 0))
    vec = lambda w: pl.BlockSpec((1, w), lambda bi, i: (0, 0))
    wspec = lambda s: pl.BlockSpec(s, lambda bi, i: (0, 0), pipeline_mode=pl.Buffered(1))
    return pl.pallas_call(
        _ffn_kernel,
        grid=(b, t // tm),
        in_specs=[row(half), row(half), wspec((half, d)), wspec((half, d)), vec(d), vec(d),
                  wspec((d, f)), wspec((d, f)), wspec((f, d)), row(d), modb, modb, vec(d), vec(d),
                  wspec((d, 2 * d)), vec(2 * d)],
        out_specs=[row(d), row(d)],
        out_shape=[jax.ShapeDtypeStruct((b, t, d), F32), jax.ShapeDtypeStruct((b, t, d), F32)],
        compiler_params=_cp("arbitrary", "arbitrary"),
        name="outproj_ffn_glu",
    )(hm, hd, wt, wb, mix_post_g.reshape(1, d), ffn_pre_g.reshape(1, d), w1, w3, w2, x, mod0, mod1,
      post_g.reshape(1, d), next_pre_g.reshape(1, d), pw1, pb1)


HALO = 16
CONV_ROW_BLOCK = 128
CONV_LANES = 256
CONV_ROWS_EXTRA = 24


def _conv_kernel(h_ref, hp_ref, hn_ref, dw_ref, dwb_ref, lng_ref, lnb_ref, w2_ref, b2_ref,
                 x_ref, mod_ref, pg_ref, fg_ref, rw_ref, x3_ref, u4_ref, route_ref, routet_ref, cnt_ref,
                 hs_ref, sh_ref, cv_ref, wb_ref, *, tm, nt):
    i = pl.program_id(1)

    hs_ref[0:HALO, :] = jnp.where(i > 0, hp_ref[...], 0.0)
    hs_ref[HALO:HALO + tm, :] = h_ref[...]
    hs_ref[HALO + tm:, :] = jnp.where(i < nt - 1, hn_ref[...], 0.0)

    for r in range(1, 8):
        sh_ref[r - 1] = hs_ref[pl.ds(r, tm + CONV_ROWS_EXTRA), :]

    @pl.when((pl.program_id(0) == 0) & (i == 0))
    def _():
        for j in range(CONV_WIDTH):
            wb_ref[j] = jnp.broadcast_to(dw_ref[j:j + 1, :], (8, D_MODEL))

    nsub = CONV_ROW_BLOCK // 8

    for l0 in range(0, D_MODEL, CONV_LANES):
        def conv_rows(rb, carry, l0=l0):
            r0 = pl.multiple_of(rb * CONV_ROW_BLOCK, CONV_ROW_BLOCK)
            bias = jnp.broadcast_to(dwb_ref[:, l0:l0 + CONV_LANES], (8, CONV_LANES))
            acc = [bias] * nsub
            for j in range(CONV_WIDTH):
                r, a = (j + 1) % 8, (j + 1) // 8
                w = wb_ref[j, :, l0:l0 + CONV_LANES]
                for s in range(nsub):
                    rows = pl.ds(r0 + 8 * (a + s), 8)
                    if r == 0:
                        win = hs_ref[rows, l0:l0 + CONV_LANES]
                    else:
                        win = sh_ref[r - 1, rows, l0:l0 + CONV_LANES]
                    acc[s] = acc[s] + w * win
            for s in range(nsub):
                cv_ref[pl.ds(r0 + 8 * s, 8), l0:l0 + CONV_LANES] = acc[s]
            return carry

        lax.fori_loop(0, tm // CONV_ROW_BLOCK, conv_rows, 0)
    acc = cv_ref[...]
    mu = jnp.mean(acc, axis=-1, keepdims=True)
    cen = acc - mu
    var = jnp.mean(cen * cen, axis=-1, keepdims=True)
    hn = _silu(cen * lax.rsqrt(var + EPS) * lng_ref[...] + lnb_ref[...])
    y = _dot(hn.astype(BF16), w2_ref[...]) + b2_ref[...]
    x3 = x_ref[...] + mod_ref[2:3, :] * _rms(y, pg_ref[...])
    x3_ref[...] = x3
    u4 = _rms(x3, fg_ref[...]) * (1.0 + mod_ref[4:5, :]) + mod_ref[3:4, :]
    u4b = u4.astype(BF16)
    u4_ref[...] = u4b

    lane = lax.broadcasted_iota(jnp.int32, (tm, LANES), 1).astype(F32)
    logits = jnp.where(lane < N_EXPERTS, _dot(u4b, rw_ref[...]), NEG)
    m1 = jnp.max(logits, axis=-1, keepdims=True)
    i1 = jnp.min(jnp.where(logits == m1, lane, float(LANES)), axis=-1, keepdims=True)
    l2 = jnp.where(lane == i1, NEG, logits)
    m2 = jnp.max(l2, axis=-1, keepdims=True)
    i2 = jnp.min(jnp.where(l2 == m2, lane, float(LANES)), axis=-1, keepdims=True)
    e21 = jnp.exp(m2 - m1)
    g1 = 1.0 / (1.0 + e21)
    g2 = e21 * g1
    sel = ((lane == i1) | (lane == i2)).astype(F32)
    ri = lax.broadcasted_iota(jnp.int32, (tm, tm), 0)
    ci = lax.broadcasted_iota(jnp.int32, (tm, tm), 1)
    tri = (ci <= ri).astype(BF16)
    csum = _dot(tri, sel.astype(BF16))
    rank = csum - sel
    r1 = jnp.sum(jnp.where(lane == i1, rank, 0.0), axis=-1, keepdims=True)
    r2 = jnp.sum(jnp.where(lane == i2, rank, 0.0), axis=-1, keepdims=True)
    cnt_ref[...] = jnp.broadcast_to(csum[tm - 1:tm, :], (8, LANES))
    route = jnp.zeros((tm, LANES), F32)
    for n, v in enumerate((i1, i2, g1, g2, r1, r2)):
        route = jnp.where(lane == float(n), v, route)
    routet_ref[...] = route.T[0:8, :]
    gsel = jnp.where(lane == i1, g1, jnp.where(lane == i2, g2, 0.0))
    ghi = gsel.astype(BF16)
    glo = (gsel - ghi.astype(F32)).astype(BF16)
    route_ref[...] = jnp.concatenate([jnp.where(sel > 0.0, rank, -1.0).astype(BF16), ghi, glo], axis=1)


def _convmod(h, dw, dwb, lng, lnb, w2, b2, x2, mod1, post_g, ffn_pre_g, rw, *, tm):
    b, t, d = x2.shape
    nt = t // tm
    hb = tm // HALO
    nbh = t // HALO
    row = lambda w: pl.BlockSpec((None, tm, w), lambda bi, i: (bi, i, 0))
    cst = lambda s: pl.BlockSpec(s, lambda bi, i: (0,) * len(s))
    return pl.pallas_call(
        functools.partial(_conv_kernel, tm=tm, nt=nt),
        grid=(b, nt),
        in_specs=[row(d),
                  pl.BlockSpec((None, HALO, d), lambda bi, i: (bi, jnp.maximum(i * hb - 1, 0), 0)),
                  pl.BlockSpec((None, HALO, d), lambda bi, i: (bi, jnp.minimum((i + 1) * hb, nbh - 1), 0)),
                  cst((32, d)), cst((1, d)), cst((1, d)), cst((1, d)),
                  cst((d, d)), cst((1, d)), row(d),
                  pl.BlockSpec((None, 6, d), lambda bi, i: (bi, 0, 0)), cst((1, d)), cst((1, d)),
                  cst((d, LANES))],
        out_specs=[row(d), row(d), row(3 * LANES),
                   pl.BlockSpec((None, 8, tm), lambda bi, i: (bi * nt + i, 0, 0)),
                   pl.BlockSpec((None, 8, LANES), lambda bi, i: (bi * nt + i, 0, 0))],
        out_shape=[jax.ShapeDtypeStruct((b, t, d), F32), jax.ShapeDtypeStruct((b, t, d), BF16),
                   jax.ShapeDtypeStruct((b, t, 3 * LANES), BF16), jax.ShapeDtypeStruct((b * nt, 8, tm), F32),
                   jax.ShapeDtypeStruct((b * nt, 8, LANES), F32)],
        scratch_shapes=[pltpu.VMEM((tm + 2 * HALO, d), F32), pltpu.VMEM((7, tm + CONV_ROWS_EXTRA, d), F32),
                        pltpu.VMEM((tm, d), F32), pltpu.VMEM((CONV_WIDTH, 8, d), F32)],
        compiler_params=_cp("arbitrary", "arbitrary"),
        name="conv_module_router",
    )(h, h, h, dw, dwb, lng, lnb, w2, b2, x2, mod1, post_g, ffn_pre_g, rw)


def _window_dma(op, ref_hbm, buf_ref, sem, s0_ref, cnt_ref, t, last_t, slot, to_hbm):
    half = MOE_WIN // 2
    for e in range(N_EXPERTS):
        first = pl.multiple_of(s0_ref[t * N_EXPERTS + e], MOE_ALIGN)
        full = (cnt_ref[t * N_EXPERTS + e] > half) | (t == last_t)
        for rows, cond in ((MOE_WIN, full), (half, jnp.logical_not(full))):
            hbm = ref_hbm.at[pl.ds(first, rows), :]
            vmem = buf_ref.at[slot, e, pl.ds(0, rows), :]
            src, dst = (vmem, hbm) if to_hbm else (hbm, vmem)
            copy = pltpu.make_async_copy(src, dst, sem.at[slot, e])
            pl.when(cond)(copy.start if op == "start" else copy.wait)


def _dispatch_kernel(s0_ref, hoff_ref, cnt_ref, zoff_ref, nu_ref, u_ref, rt_ref, xs_ref, win_ref, zero_ref,
                     sem, zsem, *, nt, n_blk):
    t = pl.program_id(0)
    slot = lax.rem(t, 2)
    half = MOE_TILE // 2

    @pl.when(t == 0)
    def _():
        zero_ref[...] = jnp.zeros(zero_ref.shape, BF16)
        zc = [pltpu.make_async_copy(
            zero_ref, xs_ref.at[pl.ds(pl.multiple_of(zoff_ref[e], MOE_ALIGN), MOE_WIN), :], zsem.at[e])
            for e in range(N_EXPERTS)]
        for c in zc:
            c.start()
        for c in zc:
            c.wait()

        def clear_block(bk, carry):
            c = pltpu.make_async_copy(
                zero_ref, xs_ref.at[pl.ds(pl.multiple_of(bk * MOE_ROWS, MOE_ROWS), MOE_ROWS), :], zsem.at[0])
            c.start()
            c.wait()
            return carry

        lax.fori_loop(nu_ref[0], n_blk, clear_block, 0)

    rt = jnp.concatenate([rt_ref[0], rt_ref[1]], axis=1)
    e1, e2, r1, r2 = rt[0:1], rt[1:2], rt[4:5], rt[5:6]
    second = lax.broadcasted_iota(jnp.int32, (1, MOE_TILE), 1) >= half
    u = u_ref[...]
    for e in range(N_EXPERTS):
        off = jnp.where(second, hoff_ref[t * N_EXPERTS + e].astype(F32), 0.0)
        lr = jnp.where(e1 == float(e), r1 + off, jnp.where(e2 == float(e), r2 + off, -1.0))
        cnt = cnt_ref[t * N_EXPERTS + e]
        for lo_row, hi_row in zip(MOE_CHUNKS[:-1], MOE_CHUNKS[1:]):
            n_rows = hi_row - lo_row

            def fill(lo_row=lo_row, n_rows=n_rows):
                srow = lax.broadcasted_iota(jnp.int32, (n_rows, MOE_TILE), 0).astype(F32) + float(lo_row)
                win_ref[slot, e, lo_row:lo_row + n_rows, :] = _dot((srow == lr).astype(BF16), u).astype(BF16)

            if lo_row == 0:
                fill()
            else:
                pl.when(cnt > lo_row)(fill)

                @pl.when(cnt <= lo_row)
                def _(lo_row=lo_row, n_rows=n_rows):
                    win_ref[slot, e, lo_row:lo_row + n_rows, :] = jnp.zeros((n_rows, D_MODEL), BF16)

    @pl.when(t > 0)
    def _():
        _window_dma("wait", xs_ref, win_ref, sem, s0_ref, cnt_ref, t - 1, nt - 1, 1 - slot, True)

    _window_dma("start", xs_ref, win_ref, sem, s0_ref, cnt_ref, t, nt - 1, slot, True)

    @pl.when(t == nt - 1)
    def _():
        _window_dma("wait", xs_ref, win_ref, sem, s0_ref, cnt_ref, t, nt - 1, slot, True)


def _dispatch(u4, routet, s0, hoff, cnt, zoff, n_used, *, n_rows):
    assert MOE_WIN == MOE_ROWS
    n, d = u4.shape
    nt = n // MOE_TILE
    per = MOE_TILE // routet.shape[2]
    grid_spec = pltpu.PrefetchScalarGridSpec(
        num_scalar_prefetch=5,
        grid=(nt,),
        in_specs=[pl.BlockSpec((MOE_TILE, d), lambda t, *_: (t, 0)),
                  pl.BlockSpec((per, 8, routet.shape[2]), lambda t, *_: (t, 0, 0))],
        out_specs=pl.BlockSpec(memory_space=pl.ANY),
        scratch_shapes=[pltpu.VMEM((2, N_EXPERTS, MOE_WIN, d), BF16), pltpu.VMEM((MOE_WIN, d), BF16),
                        pltpu.SemaphoreType.DMA((2, N_EXPERTS)), pltpu.SemaphoreType.DMA((N_EXPERTS,))],
    )
    return pl.pallas_call(
        functools.partial(_dispatch_kernel, nt=nt, n_blk=n_rows // MOE_ROWS),
        grid_spec=grid_spec,
        out_shape=jax.ShapeDtypeStruct((n_rows, d), BF16),
        compiler_params=_cp("arbitrary"),
        name="moe_dispatch",
    )(s0, hoff, cnt, zoff, n_used, u4, routet)


def _moe_kernel(be_ref, used_ref, x_ref, w1_ref, w3_ref, w2_ref, y_ref):
    used = used_ref[pl.program_id(0)] > 0

    @pl.when(used)
    def _():
        for r0 in range(0, y_ref.shape[0], FFN_ROW_GROUP):
            rows = slice(r0, r0 + FFN_ROW_GROUP)
            y_ref[rows, :] = _swiglu(x_ref[rows, :], w1_ref, w3_ref, w2_ref).astype(BF16)

    @pl.when(jnp.logical_not(used))
    def _():
        y_ref[...] = jnp.zeros(y_ref.shape, BF16)


def _moe_ffn(xs, block_e, block_used, w1, w3, w2, *, rows):
    n_rows, d = xs.shape
    f = w1.shape[2]
    wspec = lambda s: pl.BlockSpec((None,) + s, lambda i, be, bu: (be[i], 0, 0), pipeline_mode=pl.Buffered(1))
    grid_spec = pltpu.PrefetchScalarGridSpec(
        num_scalar_prefetch=2,
        grid=(n_rows // rows,),
        in_specs=[pl.BlockSpec((rows, d), lambda i, be, bu: (i, 0)),
                  wspec((d, f)), wspec((d, f)), wspec((f, d))],
        out_specs=pl.BlockSpec((rows, d), lambda i, be, bu: (i, 0)),
    )
    return pl.pallas_call(
        _moe_kernel,
        grid_spec=grid_spec,
        out_shape=jax.ShapeDtypeStruct((n_rows, d), BF16),
        compiler_params=_cp("arbitrary"),
        name="moe_ffn",
    )(block_e, block_used, xs, w1, w3, w2)


def _combine_kernel(s0_ref, hoff_ref, cnt_ref, ys_ref, route_ref, x_ref, mod_ref, pg_ref, out_ref,
                    buf_ref, acc_ref, sem, *, nti, nt):
    t = pl.program_id(0) * nti + pl.program_id(1)
    slot = lax.rem(t, 2)
    half = MOE_TILE // 2

    @pl.when(t == 0)
    def _():
        _window_dma("start", ys_ref, buf_ref, sem, s0_ref, cnt_ref, 0, nt - 1, 0, False)

    @pl.when(t + 1 < nt)
    def _():
        _window_dma("start", ys_ref, buf_ref, sem, s0_ref, cnt_ref, t + 1, nt - 1, 1 - slot, False)

    _window_dma("wait", ys_ref, buf_ref, sem, s0_ref, cnt_ref, t, nt - 1, slot, False)

    route = route_ref[...]
    sr = lax.broadcasted_iota(jnp.int32, (LANES, N_EXPERTS * LANES), 0)
    sc = lax.broadcasted_iota(jnp.int32, (LANES, N_EXPERTS * LANES), 1)
    spread = (sc // LANES == sr).astype(BF16)
    rank_b = _dot(route[:, 0:LANES], spread)
    gate_b = _dot(route[:, LANES:2 * LANES], spread) + _dot(route[:, 2 * LANES:3 * LANES], spread)
    second = lax.broadcasted_iota(jnp.int32, (MOE_TILE, LANES), 0) >= half
    scol = lax.broadcasted_iota(jnp.int32, (MOE_TILE, LANES), 1).astype(F32)

    def onehot(lr, first):
        return jnp.concatenate([(scol + float(first + c) == lr) for c in range(0, half, LANES)],
                               axis=1).astype(BF16)

    for e in range(N_EXPERTS):
        rk = rank_b[:, e * LANES:(e + 1) * LANES]
        off = jnp.where(second, hoff_ref[t * N_EXPERTS + e].astype(F32), 0.0)
        lr = jnp.where(rk < 0.0, -1.0, rk + off)
        ge = jnp.concatenate([gate_b[:, e * LANES:(e + 1) * LANES]] * (D_MODEL // LANES), axis=1)
        z = ge * _dot(onehot(lr, 0), buf_ref[slot, e, 0:half, :])
        if e == 0:
            acc_ref[...] = z
        else:
            acc_ref[...] += z

        @pl.when(cnt_ref[t * N_EXPERTS + e] > half)
        def _():
            acc_ref[...] += ge * _dot(onehot(lr, half), buf_ref[slot, e, half:MOE_WIN, :])

    out_ref[...] = x_ref[...] + mod_ref[5:6, :] * _rms(acc_ref[...], pg_ref[...])


def _combine(s0, hoff, cnt, ys, route, x3, mod1, post_g):
    b, t, d = x3.shape
    nti = t // MOE_TILE
    row = lambda w: pl.BlockSpec((None, MOE_TILE, w), lambda bi, i, *_: (bi, i, 0))
    grid_spec = pltpu.PrefetchScalarGridSpec(
        num_scalar_prefetch=3,
        grid=(b, nti),
        in_specs=[pl.BlockSpec(memory_space=pl.ANY), row(3 * LANES), row(d),
                  pl.BlockSpec((None, 6, d), lambda bi, i, *_: (bi, 0, 0)),
                  pl.BlockSpec((1, d), lambda bi, i, *_: (0, 0))],
        out_specs=row(d),
        scratch_shapes=[pltpu.VMEM((2, N_EXPERTS, MOE_WIN, d), BF16), pltpu.VMEM((MOE_TILE, d), F32),
                        pltpu.SemaphoreType.DMA((2, N_EXPERTS))],
    )
    return pl.pallas_call(
        functools.partial(_combine_kernel, nti=nti, nt=b * nti),
        grid_spec=grid_spec,
        out_shape=jax.ShapeDtypeStruct((b, t, d), F32),
        compiler_params=_cp("arbitrary", "arbitrary"),
        name="moe_combine",
    )(s0, hoff, cnt, ys, route, x3, mod1, post_g.reshape(1, d))


def _rope_tables(t_len):
    f32 = np.float32
    rows = t_len // GRID_W
    row = np.repeat(np.arange(rows, dtype=f32), GRID_W)
    col = np.tile(np.arange(GRID_W, dtype=f32), rows)
    axis_dim = DIFF_DQK // 2
    inv = (f32(ROPE_BASE) ** (-np.arange(0, axis_dim, 2, dtype=f32) / f32(axis_dim))).astype(f32)
    ang_r = (row[:, None] * inv).astype(f32)
    ang_c = (col[:, None] * inv).astype(f32)
    cr, sr, cc, sc = np.cos(ang_r), np.sin(ang_r), np.cos(ang_c), np.sin(ang_c)
    cos = np.concatenate([cr, cr, cc, cc] * 2, axis=1).astype(f32)
    sin = np.concatenate([-sr, sr, -sc, sc] * 2, axis=1).astype(f32)
    return jnp.asarray(cos), jnp.asarray(sin)


def kernel(x, c, ctx, c_ctx, l0_mod_w, l0_mod_b, l0_mix_pre_g, l0_mix_post_g, l0_w_in, l0_mlstm_gate_b, l0_mlstm_conv_w, l0_mlstm_norm_g, l0_lambda_q1, l0_lambda_k1, l0_lambda_q2, l0_lambda_k2, l0_diff_norm_g, l0_w_out, l0_ffn_pre_g, l0_ffn_post_g, l0_ffn_w1, l0_ffn_w3, l0_ffn_w2, l1_mod_w, l1_mod_b, l1_mix_pre_g, l1_mix_post_g, l1_conv_pw1_w, l1_conv_pw1_b, l1_conv_dw_w, l1_conv_dw_b, l1_conv_ln_g, l1_conv_ln_b, l1_conv_pw2_w, l1_conv_pw2_b, l1_ffn_pre_g, l1_ffn_post_g, l1_router_w, l1_moe_w1, l1_moe_w3, l1_moe_w2):
    b, t, d = x.shape
    n_ctx = ctx.shape[1]
    L = MLSTM_CHUNK
    assert d == D_MODEL and n_ctx == L and t % (2 * L) == 0 and b <= 8

    cpad = jnp.zeros((16, d), F32).at[:b].set(c).at[8].set(c_ctx)
    mod0 = _adaln(cpad, l0_mod_w, l0_mod_b).reshape(16, 6, d)
    mod1 = _adaln(cpad, l1_mod_w, l1_mod_b).reshape(16, 6, d)

    w_perm = jnp.concatenate(
        [l0_w_in[:, :2048], l0_w_in[:, 2080:], l0_w_in[:, 2048:2080], jnp.zeros((d, 96), F32)], axis=1).astype(BF16)
    gate_b = jnp.concatenate([l0_mlstm_gate_b, jnp.zeros((96,), F32)]).reshape(1, LANES)
    cos, sin = _rope_tables(t)
    g0 = l0_mix_pre_g.reshape(1, d)
    mq, mk, mv, mo, dq, dk, dv, gates = _inproj(x, mod0, None, g0, w_perm, l0_mlstm_conv_w, gate_b, cos, sin,
                                                 tm=ROW_TILE, rope=True)
    _, mkc, mvc, _, _, dkc, dvc, gates_c = _inproj(ctx, mod0, 8, g0, w_perm, l0_mlstm_conv_w, gate_b,
                                                   cos[:n_ctx], sin[:n_ctx], tm=n_ctx, rope=False)

    arow, cols = _gateprep(gates_c, gates, L=L)
    hm = _mlstm(mq, mk, mv, mo, mkc, mvc, arow, cols, l0_mlstm_norm_g, L=L)

    lam_init = 0.8 - 0.6 * math.exp(-0.3 * 0)
    lam_in = jnp.zeros((8, LANES), F32).at[0, :DIFF_DQK].set(l0_lambda_q1).at[1, :DIFF_DQK].set(l0_lambda_k1)
    lam_in = lam_in.at[2, :DIFF_DQK].set(l0_lambda_q2).at[3, :DIFF_DQK].set(l0_lambda_k2)
    hd = _attn(dq, dk, dv, dkc, dvc, lam_in, l0_diff_norm_g, tq=min(ATTN_Q_TILE, t), sub=MXU_DIM,
               max_sub_per_step=t // MXU_DIM, lam_init=lam_init)

    w_out = l0_w_out.astype(BF16)
    x2, h1 = _ffn(hm, hd, w_out[:512], w_out[512:], l0_mix_post_g, l0_ffn_pre_g,
                  l0_ffn_w1.astype(BF16), l0_ffn_w3.astype(BF16), l0_ffn_w2.astype(BF16), x, mod0, mod1,
                  l0_ffn_post_g, l1_mix_pre_g, l1_conv_pw1_w.astype(BF16), l1_conv_pw1_b.reshape(1, -1),
                  tm=ROW_TILE)

    dw = jnp.concatenate([l1_conv_dw_w, jnp.zeros((1, d), F32)], axis=0)
    rw = jnp.concatenate([l1_router_w, jnp.zeros((d, LANES - N_EXPERTS), F32)], axis=1).astype(BF16)
    v1 = lambda a: a.reshape(1, -1)
    conv_tm = MOE_TILE // 2
    x3, u4, route, routet, cnt_tile = _convmod(
        h1, dw, v1(l1_conv_dw_b), v1(l1_conv_ln_g),
        v1(l1_conv_ln_b), l1_conv_pw2_w.astype(BF16), v1(l1_conv_pw2_b), x2, mod1, v1(l1_mix_post_g),
        v1(l1_ffn_pre_g), rw, tm=conv_tm)

    n = b * t
    rows = MOE_ROWS
    nt = n // MOE_TILE
    cnt_half = cnt_tile[:, 0, :N_EXPERTS].astype(jnp.int32).reshape(nt, 2, N_EXPERTS)
    cnt = cnt_half[:, 0] + cnt_half[:, 1]
    aligned = ((cnt + MOE_ALIGN - 1) // MOE_ALIGN) * MOE_ALIGN
    base = jnp.cumsum(aligned, axis=0) - aligned
    cap = ((base[-1] + MOE_WIN + rows - 1) // rows) * rows
    pend = jnp.cumsum(cap)
    s0 = (pend - cap)[None, :] + base
    n_blk = (2 * n + N_EXPERTS * (MOE_ALIGN * nt + MOE_WIN + rows)) // rows + 1
    blk_row = jnp.arange(n_blk, dtype=jnp.int32) * rows
    block_e = jnp.minimum(jnp.sum((blk_row[:, None] >= pend[None, :]).astype(jnp.int32), axis=1), N_EXPERTS - 1)
    n_used = (pend[-1:] // rows).astype(jnp.int32)
    slot_end = (pend - cap) + base[-1] + aligned[-1]
    block_used = ((blk_row < slot_end[block_e]) & (blk_row < pend[-1])).astype(jnp.int32)
    flat = lambda a: a.reshape(-1).astype(jnp.int32)

    xs = _dispatch(u4.reshape(n, d), routet, flat(s0), flat(cnt_half[:, 0]), flat(cnt), flat(pend - MOE_WIN),
                   n_used, n_rows=n_blk * rows)
    ys = _moe_ffn(xs, block_e, block_used, l1_moe_w1.astype(BF16), l1_moe_w3.astype(BF16),
                  l1_moe_w2.astype(BF16), rows=rows)
    return _combine(flat(s0), flat(cnt_half[:, 0]), flat(cnt), ys, route, x3, mod1, l1_ffn_post_g)
```

```python
import functools
import math

import jax
import jax.numpy as jnp
import numpy as np
from jax import lax
from jax.experimental import pallas as pl
from jax.experimental.pallas import tpu as pltpu

F32 = jnp.float32
BF16 = jnp.bfloat16
EPS = 1e-6
NEG = -1e30
LOG2E = 1.4426950408889634

D_MODEL = 1024
GRID_W = 64
MLSTM_HEADS = 8
MLSTM_D = 64
MLSTM_CHUNK = 256
DIFF_HEADS = 4
DIFF_DQK = 64
ROPE_BASE = 10000.0
CONV_WIDTH = 31
N_EXPERTS = 8
MOE_ROWS = 512
MOE_TILE = 512
MOE_WIN = 512
MOE_ALIGN = 16
MOE_CHUNKS = (0, 160, 256, 512)
LANES = 128
MXU_DIM = 256
V7X_VMEM_BYTES = 64 * 1024 * 1024
VMEM_LIMIT = V7X_VMEM_BYTES - 12 * 1024 * 1024

ROW_TILE = 512
ATTN_Q_TILE = 1024
FFN_ROW_GROUP = 512


def _cp(*sem):
    return pltpu.CompilerParams(dimension_semantics=sem, vmem_limit_bytes=VMEM_LIMIT)


def _rms(x, g):
    return x * lax.rsqrt(jnp.mean(x * x, axis=-1, keepdims=True) + EPS) * g


def _silu(x):
    return x * jax.nn.sigmoid(x)


def _dot(a, b):
    return jnp.dot(a, b, preferred_element_type=F32)


def _dot_nt(a, b):
    return lax.dot_general(a, b, (((1,), (1,)), ((), ())), preferred_element_type=F32)


def _adaln_kernel(c_ref, w_ref, b_ref, o_ref):
    s = _silu(c_ref[...])
    o_ref[...] = _dot(s.astype(BF16), w_ref[...].astype(BF16)) + b_ref[...]


def _adaln(cpad, w, b):
    rows, d = cpad.shape
    n = w.shape[1]
    tn = 1536
    return pl.pallas_call(
        _adaln_kernel,
        grid=(n // tn,),
        in_specs=[pl.BlockSpec((rows, d), lambda j: (0, 0)),
                  pl.BlockSpec((d, tn), lambda j: (0, j)),
                  pl.BlockSpec((1, tn), lambda j: (0, j))],
        out_specs=pl.BlockSpec((rows, tn), lambda j: (0, j)),
        out_shape=jax.ShapeDtypeStruct((rows, n), F32),
        compiler_params=_cp("arbitrary"),
        name="adaln",
    )(cpad, w, b.reshape(1, n))


def _inproj_kernel(x_ref, xp_ref, xn_ref, mod_ref, g_ref, w_ref, cw_ref, gb_ref, cos_ref, sin_ref,
                   mq_ref, mk_ref, mv_ref, mo_ref, dq_ref, dk_ref, dv_ref, gt_ref, pext_ref,
                   *, tm, nt, rope):
    i = pl.program_id(1)
    g = g_ref[...]
    sh = mod_ref[0:1, :]
    sc = mod_ref[1:2, :]

    def mod(xv):
        return _rms(xv, g) * (1.0 + sc) + sh

    u = mod(x_ref[...])
    up = jnp.where(i > 0, mod(xp_ref[...]), 0.0)
    un = jnp.where(i < nt - 1, mod(xn_ref[...]), 0.0)
    ub = u.astype(BF16)
    uext = jnp.concatenate([up.astype(BF16), ub, un.astype(BF16)], axis=0)

    pext_ref[...] = _dot(uext, w_ref[:, 0:1024])
    cw = cw_ref[...]
    conv = (cw[0:1, :] * pext_ref[pl.ds(7, tm), :] + cw[1:2, :] * pext_ref[pl.ds(8, tm), :]
            + cw[2:3, :] * pext_ref[pl.ds(9, tm), :])
    act = _silu(conv)
    mq_ref[...] = (act[:, 0:512] * (MLSTM_D ** -0.5)).astype(BF16)
    mk_ref[...] = act[:, 512:1024].astype(BF16)

    p = _dot(ub, w_ref[:, 1024:2048])
    mv_ref[...] = p[:, 0:512].astype(BF16)
    mo_ref[...] = p[:, 512:1024].astype(BF16)

    p = _dot(ub, w_ref[:, 2048:3072])
    if rope:
        lane = lax.broadcasted_iota(jnp.int32, p.shape, 1)
        first_half = ((lane // 16) % 2) == 0
        nl = p.shape[1]
        partner = jnp.where(first_half, pltpu.roll(p, nl - 16, 1), pltpu.roll(p, 16, 1))
        cos = jnp.concatenate([cos_ref[...]] * 8, axis=1)
        sin = jnp.concatenate([sin_ref[...]] * 8, axis=1)
        p = p * cos + partner * sin
    dq_ref[...] = (p[:, 0:512] * (LOG2E * DIFF_DQK ** -0.5)).astype(BF16)
    dk_ref[...] = p[:, 512:1024].astype(BF16)

    p = _dot(ub, w_ref[:, 3072:3712])
    dv_ref[...] = p[:, 0:512].astype(BF16)
    gt_ref[...] = p[:, 512:640] + gb_ref[...]


def _inproj(x, mod, mod_row, g, w_perm, conv_w, gate_b, cos, sin, *, tm, rope):
    b, t, d = x.shape
    nt = t // tm
    hb = tm // 8
    nb8 = t // 8
    bf = lambda: jax.ShapeDtypeStruct((b, t, 512), BF16)
    if mod_row is None:
        mod_map = lambda bi, i: (bi, 0, 0)
    else:
        mod_map = lambda bi, i: (mod_row, 0, 0)
    kern = functools.partial(_inproj_kernel, tm=tm, nt=nt, rope=rope)
    o512 = pl.BlockSpec((None, tm, 512), lambda bi, i: (bi, i, 0))
    return pl.pallas_call(
        kern,
        grid=(b, nt),
        in_specs=[
            pl.BlockSpec((None, tm, d), lambda bi, i: (bi, i, 0)),
            pl.BlockSpec((None, 8, d), lambda bi, i: (bi, jnp.maximum(i * hb - 1, 0), 0)),
            pl.BlockSpec((None, 8, d), lambda bi, i: (bi, jnp.minimum((i + 1) * hb, nb8 - 1), 0)),
            pl.BlockSpec((None, 6, d), mod_map),
            pl.BlockSpec((1, d), lambda bi, i: (0, 0)),
            pl.BlockSpec((d, 3712), lambda bi, i: (0, 0)),
            pl.BlockSpec((3, d), lambda bi, i: (0, 0)),
            pl.BlockSpec((1, LANES), lambda bi, i: (0, 0)),
            pl.BlockSpec((tm, LANES), lambda bi, i: (i, 0)),
            pl.BlockSpec((tm, LANES), lambda bi, i: (i, 0)),
        ],
        out_specs=[o512] * 7 + [pl.BlockSpec((None, tm, LANES), lambda bi, i: (bi, i, 0))],
        out_shape=[bf() for _ in range(7)] + [jax.ShapeDtypeStruct((b, t, LANES), F32)],
        scratch_shapes=[pltpu.VMEM((tm + 16, 1024), F32)],
        compiler_params=_cp("arbitrary", "arbitrary"),
        name="inproj_rope" if rope else "inproj_ctx",
    )(x, x, x, mod, g, w_perm, conv_w, gate_b, cos, sin)


def _gateprep_kernel(gc_ref, g_ref, arow_ref, col_ref, *, L, nch):
    tt = L * nch
    gt = jnp.concatenate([gc_ref[...].T, g_ref[...].T], axis=1)
    i_f, f_f, i_b, f_b = gt[0:8], gt[8:16], gt[16:24], gt[24:32]

    def logsig(v):
        return jnp.minimum(v, 0.0) - jnp.log1p(jnp.exp(-jnp.abs(v)))

    pos = lax.broadcasted_iota(jnp.int32, (8, tt), 1) % L

    def scan(v, op, ident, reverse):
        s = 1
        while s < L:
            if reverse:
                shifted = pltpu.roll(v, tt - s, 1)
                valid = pos < L - s
            else:
                shifted = pltpu.roll(v, s, 1)
                valid = pos >= s
            v = op(v, jnp.where(valid, shifted, ident))
            s *= 2
        return v

    outs = []
    for d, (ig, fg) in enumerate(((i_f, f_f), (i_b, f_b))):
        rev = d == 1
        bcum = scan(logsig(fg), jnp.add, 0.0, rev)
        a = ig - bcum
        cm = scan(a, jnp.maximum, NEG, rev)
        order = list(range(nch)) if not rev else [0] + list(range(nch - 1, 0, -1))
        mp = jnp.zeros((8, 1), F32)
        mp_c = [None] * nch
        for j in order:
            e = j * L if rev else j * L + L - 1
            mp_c[j] = jnp.broadcast_to(mp, (8, L))
            mp = bcum[:, e:e + 1] + jnp.maximum(mp, cm[:, e:e + 1])
        mprev = jnp.concatenate(mp_c, axis=1)
        m = jnp.maximum(mprev, cm)
        outs.append((a, m, jnp.exp(mprev - m), jnp.exp(-(bcum + m))))

    arow_ref[...] = jnp.concatenate([outs[0][0], outs[1][0]], axis=0)
    rows = []
    for pair in range(4):
        for q in range(1, 5):
            for d in range(2):
                rows.append(outs[d][q][2 * pair:2 * pair + 2] if q < 4 else jnp.zeros((2, tt), F32))
    rows.append(jnp.zeros((64, tt), F32))
    col_ref[...] = jnp.concatenate(rows, axis=0).T


def _gateprep(gates_c, gates, *, L):
    b, t, _ = gates.shape
    n_ctx = gates_c.shape[1]
    tt = n_ctx + t
    nch = tt // L
    return pl.pallas_call(
        functools.partial(_gateprep_kernel, L=L, nch=nch),
        grid=(b,),
        in_specs=[pl.BlockSpec((None, n_ctx, LANES), lambda bi: (bi, 0, 0)),
                  pl.BlockSpec((None, t, LANES), lambda bi: (bi, 0, 0))],
        out_specs=[pl.BlockSpec((None, 16, tt), lambda bi: (bi, 0, 0)),
                   pl.BlockSpec((None, tt, LANES), lambda bi: (bi, 0, 0))],
        out_shape=[jax.ShapeDtypeStruct((b, 16, tt), F32), jax.ShapeDtypeStruct((b, tt, LANES), F32)],
        compiler_params=_cp("arbitrary"),
        name="mlstm_gateprep",
    )(gates_c, gates)


def _mlstm_kernel(q_ref, k_ref, v_ref, o_ref, kc_ref, vc_ref, arow_ref, col_ref, ng_ref, out_ref,
                  c_ref, hf_ref, hb_ref, *, L, nc):
    p = pl.program_id(1)
    half = nc // 2
    lane = lax.broadcasted_iota(jnp.int32, (L, LANES), 1)
    lo = lane < MLSTM_D
    head_mask = (lo, jnp.logical_not(lo))
    ri = lax.broadcasted_iota(jnp.int32, (L, L), 0)
    ci = lax.broadcasted_iota(jnp.int32, (L, L), 1)
    causal = (ci <= ri, ci >= ri)
    ones_t = jnp.ones((L, LANES), BF16)
    shift = lax.rem(LANES - 16 * p, LANES)

    def cols(off):
        return pltpu.roll(col_ref[pl.ds(off, L), :], shift, 1)

    def col(blk, q, d, hh):
        j = q * 4 + d * 2 + hh
        return blk[:, j:j + 1]

    def vext_of(vb, hh):
        return jnp.concatenate([jnp.where(head_mask[hh], vb, jnp.zeros_like(vb)), ones_t], axis=1)

    def state_update(d, hh, kb, vext, blk, arow, dec):
        last = L - 1 if d == 0 else 0
        khm = jnp.where(head_mask[hh], kb, jnp.zeros_like(kb))
        ws = jnp.exp(arow - col(blk, 0, d, hh)[last:last + 1, :])
        ksc = (khm.T.astype(F32) * ws).astype(BF16)
        upd = _dot(ksc, vext)
        if dec is None:
            c_ref[d, hh] = upd
        else:
            c_ref[d, hh] = dec * c_ref[d, hh] + upd

    def arow_of(d, hh, off):
        return arow_ref[pl.ds(d * 8 + 2 * p + hh, 1), pl.ds(off, L)]

    blk0 = cols(0)
    kcb = kc_ref[...]
    vcb = vc_ref[...]
    for d in range(2):
        for hh in range(2):
            state_update(d, hh, kcb, vext_of(vcb, hh), blk0, arow_of(d, hh, 0), None)

    def compute(d, c):
        t0 = pl.multiple_of(c * L, L)
        off = pl.multiple_of(c * L + L, L)
        qb = q_ref[pl.ds(t0, L), :]
        kb = k_ref[pl.ds(t0, L), :]
        vb = v_ref[pl.ds(t0, L), :]
        blk = cols(off)
        last = L - 1 if d == 0 else 0
        hs = []
        for hh in range(2):
            arow = arow_of(d, hh, off)
            khm = jnp.where(head_mask[hh], kb, jnp.zeros_like(kb))
            s = _dot_nt(qb, khm)
            arg = jnp.where(causal[d], arow - col(blk, 0, d, hh), NEG)
            pm = (s * jnp.exp(arg)).astype(BF16)
            vext = vext_of(vb, hh)
            ch = c_ref[d, hh]
            tot = _dot(pm, vext) + col(blk, 1, d, hh) * _dot(qb, ch.astype(BF16))
            den = jnp.maximum(jnp.abs(tot[:, LANES:]), col(blk, 2, d, hh))
            hs.append(tot[:, :LANES] / den)
            dec = col(blk, 1, d, hh)[last:last + 1, :]
            state_update(d, hh, kb, vext, blk, arow, dec)
        return jnp.where(lo, hs[0], hs[1])

    def finalize(c, hsum):
        t0 = pl.multiple_of(c * L, L)
        sq = hsum * hsum
        s0 = jnp.sum(jnp.where(lo, sq, 0.0), axis=-1, keepdims=True)
        s1 = jnp.sum(jnp.where(lo, 0.0, sq), axis=-1, keepdims=True)
        ms = jnp.where(lo, s0, s1) * (1.0 / MLSTM_D)
        y = hsum * lax.rsqrt(ms + EPS) * ng_ref[...]
        gate = jax.nn.sigmoid(o_ref[pl.ds(t0, L), :].astype(F32))
        out_ref[pl.ds(t0, L), :] = (y * gate).astype(BF16)

    def phase_a(i, carry):
        hf_ref[pl.ds(pl.multiple_of(i * L, L), L), :] = compute(0, i)
        cb = nc - 1 - i
        hb_ref[pl.ds(pl.multiple_of((cb - half) * L, L), L), :] = compute(1, cb)
        return carry

    def phase_b(i, carry):
        hf = compute(0, i)
        finalize(i, hf + hb_ref[pl.ds(pl.multiple_of((i - half) * L, L), L), :])
        cb = nc - 1 - i
        hb = compute(1, cb)
        finalize(cb, hb + hf_ref[pl.ds(pl.multiple_of(cb * L, L), L), :])
        return carry

    lax.fori_loop(0, half, phase_a, 0)
    lax.fori_loop(half, nc, phase_b, 0)


def _mlstm(mq, mk, mv, mo, mkc, mvc, arow, cols, norm_g, *, L):
    b, t, _ = mq.shape
    ctx = mkc.shape[1]
    tt = arow.shape[2]
    nc = t // L
    tok = pl.BlockSpec((None, t, LANES), lambda bi, p: (bi, 0, p))
    ctxs = pl.BlockSpec((None, ctx, LANES), lambda bi, p: (bi, 0, p))
    return pl.pallas_call(
        functools.partial(_mlstm_kernel, L=L, nc=nc),
        grid=(b, 4),
        in_specs=[tok, tok, tok, tok, ctxs, ctxs,
                  pl.BlockSpec((None, 16, tt), lambda bi, p: (bi, 0, 0)),
                  pl.BlockSpec((None, tt, LANES), lambda bi, p: (bi, 0, 0)),
                  pl.BlockSpec((1, LANES), lambda bi, p: (0, p))],
        out_specs=tok,
        out_shape=jax.ShapeDtypeStruct((b, t, 512), BF16),
        scratch_shapes=[pltpu.VMEM((2, 2, LANES, 2 * LANES), F32),
                        pltpu.VMEM((t // 2, LANES), F32),
                        pltpu.VMEM((t // 2, LANES), F32)],
        compiler_params=_cp("arbitrary", "arbitrary"),
        name="mlstm_scan",
    )(mq, mk, mv, mo, mkc, mvc, arow, cols, norm_g.reshape(1, 512))


def _attn_kernel(q_ref, k_ref, v_ref, kc_ref, vc_ref, lam_ref, ng_ref, out_ref, kmax_ref, m_ref, acc_ref,
                 *, tq, kb, nkb, sub, lam_init):
    i = pl.program_id(2)
    lane = lax.broadcasted_iota(jnp.int32, (tq, LANES), 1)
    lo = lane < DIFF_DQK
    rr = lax.broadcasted_iota(jnp.int32, (LANES, LANES), 0)
    cc = lax.broadcasted_iota(jnp.int32, (LANES, LANES), 1)
    same_comp = ((rr < DIFF_DQK) == (cc < DIFF_DQK)).astype(BF16)

    def comp_sqnorm(a):
        af = a.astype(F32)
        return _dot((af * af).astype(BF16), same_comp)

    @pl.when(i == 0)
    def _():
        mx = jnp.max(comp_sqnorm(kc_ref[...]), axis=0, keepdims=True)

        def kbody(j, mx):
            k0 = pl.multiple_of(j * kb, kb)
            return jnp.maximum(mx, jnp.max(comp_sqnorm(k_ref[pl.ds(k0, kb), :]), axis=0, keepdims=True))

        kmax_ref[...] = lax.fori_loop(0, nkb, kbody, mx)

    q = q_ref[...]
    zq = jnp.zeros_like(q)
    bnd = jnp.sqrt(comp_sqnorm(q) * kmax_ref[...]) * 1.02
    b1 = jnp.where(lane == 0, -bnd, 0.0)
    b2 = jnp.where(lane == 0, -pltpu.roll(bnd, DIFF_DQK, 1), 0.0)
    qs = jnp.concatenate(
        [jnp.concatenate([jnp.where(lo, q, zq), jnp.where(lo, zq, q)], axis=0),
         jnp.concatenate([b1, b2], axis=0).astype(BF16)], axis=1)
    n_ctx = kc_ref.shape[0]

    def ext(blk):
        return jnp.concatenate([blk, jnp.ones_like(blk)], axis=1)

    def sum_blocks(kr, vr, k0, n, tot):
        for c in range(0, n, sub):
            pm = jnp.exp2(_dot_nt(qs, ext(kr[pl.ds(k0 + c, sub), :]))).astype(BF16)
            part = _dot(pm, ext(vr[pl.ds(k0 + c, sub), :]))
            tot = part if tot is None else tot + part
        return tot

    def slow_step(kblk, vblk):
        s = _dot_nt(qs, ext(kblk))
        m_old = m_ref[...]
        m_new = jnp.maximum(m_old, jnp.max(s, axis=-1, keepdims=True))
        pm = jnp.exp2(s - m_new).astype(BF16)
        acc_ref[...] = jnp.exp2(m_old - m_new) * acc_ref[...] + _dot(pm, ext(vblk))
        m_ref[...] = m_new

    fast = jnp.max(bnd) <= 56.0

    @pl.when(fast)
    def _():
        tot = sum_blocks(kc_ref, vc_ref, 0, n_ctx, None)
        if nkb == 1:
            acc_ref[...] = sum_blocks(k_ref, v_ref, 0, kb, tot)
        else:
            acc_ref[...] = tot

            def body(j, carry):
                acc_ref[...] += sum_blocks(k_ref, v_ref, pl.multiple_of(j * kb, kb), kb, None)
                return carry

            lax.fori_loop(0, nkb, body, 0)

    @pl.when(jnp.logical_not(fast))
    def _():
        m_ref[...] = jnp.full(m_ref.shape, NEG, F32)
        acc_ref[...] = jnp.zeros(acc_ref.shape, F32)
        slow_step(kc_ref[...], vc_ref[...])

        def body(j, carry):
            k0 = pl.multiple_of(j * sub, sub)
            slow_step(k_ref[pl.ds(k0, sub), :], v_ref[pl.ds(k0, sub), :])
            return carry

        lax.fori_loop(0, (nkb * kb) // sub, body, 0)

    lq = lam_ref[...]
    lam = (jnp.exp(jnp.sum(lq[0:1, :] * lq[1:2, :], axis=-1, keepdims=True))
           - jnp.exp(jnp.sum(lq[2:3, :] * lq[3:4, :], axis=-1, keepdims=True)) + lam_init)
    a1 = acc_ref[0:tq, :]
    a2 = acc_ref[tq:2 * tq, :]
    o = a1[:, :LANES] / a1[:, LANES:] - lam * (a2[:, :LANES] / a2[:, LANES:])
    out_ref[...] = (_rms(o, ng_ref[...]) * (1.0 - lam_init)).astype(BF16)


def _attn(dq, dk, dv, dkc, dvc, lam_in, norm_g, *, tq, sub, max_sub_per_step, lam_init):
    b, t, _ = dq.shape
    n_ctx = dkc.shape[1]
    assert n_ctx % sub == 0 and t % sub == 0
    nsub = t // sub
    per = max(g for g in range(1, max_sub_per_step + 1) if nsub % g == 0)
    kb = per * sub
    full = pl.BlockSpec((None, t, LANES), lambda bi, h, i: (bi, 0, h))
    ctxs = pl.BlockSpec((None, n_ctx, LANES), lambda bi, h, i: (bi, 0, h))
    qs = pl.BlockSpec((None, tq, LANES), lambda bi, h, i: (bi, i, h))
    return pl.pallas_call(
        functools.partial(_attn_kernel, tq=tq, kb=kb, nkb=t // kb, sub=sub, lam_init=lam_init),
        grid=(b, DIFF_HEADS, t // tq),
        in_specs=[qs, full, full, ctxs, ctxs,
                  pl.BlockSpec((8, LANES), lambda bi, h, i: (0, 0)),
                  pl.BlockSpec((1, LANES), lambda bi, h, i: (0, h))],
        out_specs=qs,
        out_shape=jax.ShapeDtypeStruct((b, t, 512), BF16),
        scratch_shapes=[pltpu.VMEM((1, LANES), F32), pltpu.VMEM((2 * tq, 1), F32),
                        pltpu.VMEM((2 * tq, 2 * LANES), F32)],
        compiler_params=_cp("arbitrary", "arbitrary", "arbitrary"),
        name="diff_attn",
    )(dq, dk, dv, dkc, dvc, lam_in, norm_g.reshape(1, 512))


def _swiglu(u, w1_ref, w3_ref, w2_ref):
    return _dot((_silu(_dot(u, w1_ref[...])) * _dot(u, w3_ref[...])).astype(BF16), w2_ref[...])


def _ffn_kernel(hm_ref, hd_ref, wt_ref, wb_ref, mg_ref, fg_ref, w1_ref, w3_ref, w2_ref, x_ref, mod0_ref, mod1_ref,
                pg_ref, ng_ref, pw_ref, pb_ref, x2_ref, h_ref):
    for r0 in range(0, x_ref.shape[0], FFN_ROW_GROUP):
        rows = slice(r0, r0 + FFN_ROW_GROUP)
        mix = _dot(hm_ref[rows, :], wt_ref[...]) + _dot(hd_ref[rows, :], wb_ref[...])
        x1 = x_ref[rows, :] + mod0_ref[2:3, :] * _rms(mix, mg_ref[...])
        u = (_rms(x1, fg_ref[...]) * (1.0 + mod0_ref[4:5, :]) + mod0_ref[3:4, :]).astype(BF16)
        y = _swiglu(u, w1_ref, w3_ref, w2_ref)
        x2 = x1 + mod0_ref[5:6, :] * _rms(y, pg_ref[...])
        x2_ref[rows, :] = x2
        u3 = (_rms(x2, ng_ref[...]) * (1.0 + mod1_ref[1:2, :]) + mod1_ref[0:1, :]).astype(BF16)
        ag = _dot(u3, pw_ref[...]) + pb_ref[...]
        h_ref[rows, :] = ag[:, :D_MODEL] * jax.nn.sigmoid(ag[:, D_MODEL:])


def _ffn(hm, hd, wt, wb, mix_post_g, ffn_pre_g, w1, w3, w2, x, mod0, mod1, post_g, next_pre_g, pw1, pb1, *, tm):
    b, t, d = x.shape
    f = w1.shape[1]
    half = hm.shape[2]
    row = lambda w: pl.BlockSpec((None, tm, w), lambda bi, i: (bi, i, 0))
    modb = pl.BlockSpec((None, 6, d), lambda bi, i: (bi, 0, 0))
    vec = lambda w: pl.BlockSpec((1, w), lambda bi, i: (0, 0))
    wspec = lambda s: pl.BlockSpec(s, lambda bi, i: (0, 0), pipeline_mode=pl.Buffered(1))
    return pl.pallas_call(
        _ffn_kernel,
        grid=(b, t // tm),
        in_specs=[row(half), row(half), wspec((half, d)), wspec((half, d)), vec(d), vec(d),
                  wspec((d, f)), wspec((d, f)), wspec((f, d)), row(d), modb, modb, vec(d), vec(d),
                  wspec((d, 2 * d)), vec(2 * d)],
        out_specs=[row(d), row(d)],
        out_shape=[jax.ShapeDtypeStruct((b, t, d), F32), jax.ShapeDtypeStruct((b, t, d), F32)],
        compiler_params=_cp("arbitrary", "arbitrary"),
        name="outproj_ffn_glu",
    )(hm, hd, wt, wb, mix_post_g.reshape(1, d), ffn_pre_g.reshape(1, d), w1, w3, w2, x, mod0, mod1,
      post_g.reshape(1, d), next_pre_g.reshape(1, d), pw1, pb1)


HALO = 16
CONV_ROW_BLOCK = 128
CONV_LANES = 256
CONV_ROWS_EXTRA = 24


def _conv_kernel(h_ref, hp_ref, hn_ref, dw_ref, dwb_ref, lng_ref, lnb_ref, w2_ref, b2_ref,
                 x_ref, mod_ref, pg_ref, fg_ref, rw_ref, x3_ref, u4_ref, route_ref, routet_ref, cnt_ref,
                 hs_ref, sh_ref, cv_ref, wb_ref, *, tm, nt):
    i = pl.program_id(1)

    hs_ref[0:HALO, :] = jnp.where(i > 0, hp_ref[...], 0.0)
    hs_ref[HALO:HALO + tm, :] = h_ref[...]
    hs_ref[HALO + tm:, :] = jnp.where(i < nt - 1, hn_ref[...], 0.0)

    for r in range(1, 8):
        sh_ref[r - 1] = hs_ref[pl.ds(r, tm + CONV_ROWS_EXTRA), :]

    @pl.when((pl.program_id(0) == 0) & (i == 0))
    def _():
        for j in range(CONV_WIDTH):
            wb_ref[j] = jnp.broadcast_to(dw_ref[j:j + 1, :], (8, D_MODEL))

    nsub = CONV_ROW_BLOCK // 8

    for l0 in range(0, D_MODEL, CONV_LANES):
        def conv_rows(rb, carry, l0=l0):
            r0 = pl.multiple_of(rb * CONV_ROW_BLOCK, CONV_ROW_BLOCK)
            bias = jnp.broadcast_to(dwb_ref[:, l0:l0 + CONV_LANES], (8, CONV_LANES))
            acc = [bias] * nsub
            for j in range(CONV_WIDTH):
                r, a = (j + 1) % 8, (j + 1) // 8
                w = wb_ref[j, :, l0:l0 + CONV_LANES]
                for s in range(nsub):
                    rows = pl.ds(r0 + 8 * (a + s), 8)
                    if r == 0:
                        win = hs_ref[rows, l0:l0 + CONV_LANES]
                    else:
                        win = sh_ref[r - 1, rows, l0:l0 + CONV_LANES]
                    acc[s] = acc[s] + w * win
            for s in range(nsub):
                cv_ref[pl.ds(r0 + 8 * s, 8), l0:l0 + CONV_LANES] = acc[s]
            return carry

        lax.fori_loop(0, tm // CONV_ROW_BLOCK, conv_rows, 0)
    acc = cv_ref[...]
    mu = jnp.mean(acc, axis=-1, keepdims=True)
    cen = acc - mu
    var = jnp.mean(cen * cen, axis=-1, keepdims=True)
    hn = _silu(cen * lax.rsqrt(var + EPS) * lng_ref[...] + lnb_ref[...])
    y = _dot(hn.astype(BF16), w2_ref[...]) + b2_ref[...]
    x3 = x_ref[...] + mod_ref[2:3, :] * _rms(y, pg_ref[...])
    x3_ref[...] = x3
    u4 = _rms(x3, fg_ref[...]) * (1.0 + mod_ref[4:5, :]) + mod_ref[3:4, :]
    u4b = u4.astype(BF16)
    u4_ref[...] = u4b

    lane = lax.broadcasted_iota(jnp.int32, (tm, LANES), 1).astype(F32)
    logits = jnp.where(lane < N_EXPERTS, _dot(u4b, rw_ref[...]), NEG)
    m1 = jnp.max(logits, axis=-1, keepdims=True)
    i1 = jnp.min(jnp.where(logits == m1, lane, float(LANES)), axis=-1, keepdims=True)
    l2 = jnp.where(lane == i1, NEG, logits)
    m2 = jnp.max(l2, axis=-1, keepdims=True)
    i2 = jnp.min(jnp.where(l2 == m2, lane, float(LANES)), axis=-1, keepdims=True)
    e21 = jnp.exp(m2 - m1)
    g1 = 1.0 / (1.0 + e21)
    g2 = e21 * g1
    sel = ((lane == i1) | (lane == i2)).astype(F32)
    ri = lax.broadcasted_iota(jnp.int32, (tm, tm), 0)
    ci = lax.broadcasted_iota(jnp.int32, (tm, tm), 1)
    tri = (ci <= ri).astype(BF16)
    csum = _dot(tri, sel.astype(BF16))
    rank = csum - sel
    r1 = jnp.sum(jnp.where(lane == i1, rank, 0.0), axis=-1, keepdims=True)
    r2 = jnp.sum(jnp.where(lane == i2, rank, 0.0), axis=-1, keepdims=True)
    cnt_ref[...] = jnp.broadcast_to(csum[tm - 1:tm, :], (8, LANES))
    route = jnp.zeros((tm, LANES), F32)
    for n, v in enumerate((i1, i2, g1, g2, r1, r2)):
        route = jnp.where(lane == float(n), v, route)
    routet_ref[...] = route.T[0:8, :]
    gsel = jnp.where(lane == i1, g1, jnp.where(lane == i2, g2, 0.0))
    ghi = gsel.astype(BF16)
    glo = (gsel - ghi.astype(F32)).astype(BF16)
    route_ref[...] = jnp.concatenate([jnp.where(sel > 0.0, rank, -1.0).astype(BF16), ghi, glo], axis=1)


def _convmod(h, dw, dwb, lng, lnb, w2, b2, x2, mod1, post_g, ffn_pre_g, rw, *, tm):
    b, t, d = x2.shape
    nt = t // tm
    hb = tm // HALO
    nbh = t // HALO
    row = lambda w: pl.BlockSpec((None, tm, w), lambda bi, i: (bi, i, 0))
    cst = lambda s: pl.BlockSpec(s, lambda bi, i: (0,) * len(s))
    return pl.pallas_call(
        functools.partial(_conv_kernel, tm=tm, nt=nt),
        grid=(b, nt),
        in_specs=[row(d),
                  pl.BlockSpec((None, HALO, d), lambda bi, i: (bi, jnp.maximum(i * hb - 1, 0), 0)),
                  pl.BlockSpec((None, HALO, d), lambda bi, i: (bi, jnp.minimum((i + 1) * hb, nbh - 1), 0)),
                  cst((32, d)), cst((1, d)), cst((1, d)), cst((1, d)),
                  cst((d, d)), cst((1, d)), row(d),
                  pl.BlockSpec((None, 6, d), lambda bi, i: (bi, 0, 0)), cst((1, d)), cst((1, d)),
                  cst((d, LANES))],
        out_specs=[row(d), row(d), row(3 * LANES),
                   pl.BlockSpec((None, 8, tm), lambda bi, i: (bi * nt + i, 0, 0)),
                   pl.BlockSpec((None, 8, LANES), lambda bi, i: (bi * nt + i, 0, 0))],
        out_shape=[jax.ShapeDtypeStruct((b, t, d), F32), jax.ShapeDtypeStruct((b, t, d), BF16),
                   jax.ShapeDtypeStruct((b, t, 3 * LANES), BF16), jax.ShapeDtypeStruct((b * nt, 8, tm), F32),
                   jax.ShapeDtypeStruct((b * nt, 8, LANES), F32)],
        scratch_shapes=[pltpu.VMEM((tm + 2 * HALO, d), F32), pltpu.VMEM((7, tm + CONV_ROWS_EXTRA, d), F32),
                        pltpu.VMEM((tm, d), F32), pltpu.VMEM((CONV_WIDTH, 8, d), F32)],
        compiler_params=_cp("arbitrary", "arbitrary"),
        name="conv_module_router",
    )(h, h, h, dw, dwb, lng, lnb, w2, b2, x2, mod1, post_g, ffn_pre_g, rw)


def _window_copies(ref_hbm, buf_ref, sem, s0_ref, t, slot, to_hbm):
    out = []
    for e in range(N_EXPERTS):
        hbm = ref_hbm.at[pl.ds(pl.multiple_of(s0_ref[t * N_EXPERTS + e], MOE_ALIGN), MOE_WIN), :]
        vmem = buf_ref.at[slot, e]
        src, dst = (vmem, hbm) if to_hbm else (hbm, vmem)
        out.append(pltpu.make_async_copy(src, dst, sem.at[slot, e]))
    return out


def _dispatch_kernel(s0_ref, hoff_ref, cnt_ref, zoff_ref, nu_ref, u_ref, rt_ref, xs_ref, win_ref, zero_ref,
                     sem, zsem, *, nt, n_blk):
    t = pl.program_id(0)
    slot = lax.rem(t, 2)
    half = MOE_TILE // 2

    @pl.when(t == 0)
    def _():
        zero_ref[...] = jnp.zeros(zero_ref.shape, BF16)
        zc = [pltpu.make_async_copy(
            zero_ref, xs_ref.at[pl.ds(pl.multiple_of(zoff_ref[e], MOE_ALIGN), MOE_WIN), :], zsem.at[e])
            for e in range(N_EXPERTS)]
        for c in zc:
            c.start()
        for c in zc:
            c.wait()

        def clear_block(bk, carry):
            c = pltpu.make_async_copy(
                zero_ref, xs_ref.at[pl.ds(pl.multiple_of(bk * MOE_ROWS, MOE_ROWS), MOE_ROWS), :], zsem.at[0])
            c.start()
            c.wait()
            return carry

        lax.fori_loop(nu_ref[0], n_blk, clear_block, 0)

    rt = jnp.concatenate([rt_ref[0], rt_ref[1]], axis=1)
    e1, e2, r1, r2 = rt[0:1], rt[1:2], rt[4:5], rt[5:6]
    second = lax.broadcasted_iota(jnp.int32, (1, MOE_TILE), 1) >= half
    u = u_ref[...]
    lrs = []
    for e in range(N_EXPERTS):
        off = jnp.where(second, hoff_ref[t * N_EXPERTS + e].astype(F32), 0.0)
        lrs.append(jnp.where(e1 == float(e), r1 + off, jnp.where(e2 == float(e), r2 + off, -1.0)))

    c0 = MOE_CHUNKS[1]
    srow0 = lax.broadcasted_iota(jnp.int32, (c0, MOE_TILE), 0).astype(F32)
    first = _dot(jnp.concatenate([(srow0 == lr).astype(BF16) for lr in lrs], axis=0), u).astype(BF16)
    for e in range(N_EXPERTS):
        lr = lrs[e]
        cnt = cnt_ref[t * N_EXPERTS + e]
        win_ref[slot, e, 0:c0, :] = first[e * c0:(e + 1) * c0, :]
        for lo_row, hi_row in zip(MOE_CHUNKS[1:-1], MOE_CHUNKS[2:]):
            n_rows = hi_row - lo_row

            @pl.when(cnt > lo_row)
            def _(lo_row=lo_row, n_rows=n_rows, lr=lr, e=e):
                srow = lax.broadcasted_iota(jnp.int32, (n_rows, MOE_TILE), 0).astype(F32) + float(lo_row)
                win_ref[slot, e, lo_row:lo_row + n_rows, :] = _dot((srow == lr).astype(BF16), u).astype(BF16)

            @pl.when(cnt <= lo_row)
            def _(lo_row=lo_row, n_rows=n_rows, e=e):
                win_ref[slot, e, lo_row:lo_row + n_rows, :] = jnp.zeros((n_rows, D_MODEL), BF16)

    @pl.when(t > 0)
    def _():
        for c in _window_copies(xs_ref, win_ref, sem, s0_ref, t - 1, 1 - slot, True):
            c.wait()

    for c in _window_copies(xs_ref, win_ref, sem, s0_ref, t, slot, True):
        c.start()

    @pl.when(t == nt - 1)
    def _():
        for c in _window_copies(xs_ref, win_ref, sem, s0_ref, t, slot, True):
            c.wait()


def _dispatch(u4, routet, s0, hoff, cnt, zoff, n_used, *, n_rows):
    assert MOE_WIN == MOE_ROWS
    n, d = u4.shape
    nt = n // MOE_TILE
    per = MOE_TILE // routet.shape[2]
    grid_spec = pltpu.PrefetchScalarGridSpec(
        num_scalar_prefetch=5,
        grid=(nt,),
        in_specs=[pl.BlockSpec((MOE_TILE, d), lambda t, *_: (t, 0)),
                  pl.BlockSpec((per, 8, routet.shape[2]), lambda t, *_: (t, 0, 0))],
        out_specs=pl.BlockSpec(memory_space=pl.ANY),
        scratch_shapes=[pltpu.VMEM((2, N_EXPERTS, MOE_WIN, d), BF16), pltpu.VMEM((MOE_WIN, d), BF16),
                        pltpu.SemaphoreType.DMA((2, N_EXPERTS)), pltpu.SemaphoreType.DMA((N_EXPERTS,))],
    )
    return pl.pallas_call(
        functools.partial(_dispatch_kernel, nt=nt, n_blk=n_rows // MOE_ROWS),
        grid_spec=grid_spec,
        out_shape=jax.ShapeDtypeStruct((n_rows, d), BF16),
        compiler_params=_cp("arbitrary"),
        name="moe_dispatch",
    )(s0, hoff, cnt, zoff, n_used, u4, routet)


def _moe_kernel(be_ref, used_ref, x_ref, w1_ref, w3_ref, w2_ref, y_ref):
    used = used_ref[pl.program_id(0)] > 0

    @pl.when(used)
    def _():
        for r0 in range(0, y_ref.shape[0], FFN_ROW_GROUP):
            rows = slice(r0, r0 + FFN_ROW_GROUP)
            y_ref[rows, :] = _swiglu(x_ref[rows, :], w1_ref, w3_ref, w2_ref).astype(BF16)

    @pl.when(jnp.logical_not(used))
    def _():
        y_ref[...] = jnp.zeros(y_ref.shape, BF16)


def _moe_ffn(xs, block_e, block_used, w1, w3, w2, *, rows):
    n_rows, d = xs.shape
    f = w1.shape[2]
    wspec = lambda s: pl.BlockSpec((None,) + s, lambda i, be, bu: (be[i], 0, 0), pipeline_mode=pl.Buffered(1))
    grid_spec = pltpu.PrefetchScalarGridSpec(
        num_scalar_prefetch=2,
        grid=(n_rows // rows,),
        in_specs=[pl.BlockSpec((rows, d), lambda i, be, bu: (i, 0)),
                  wspec((d, f)), wspec((d, f)), wspec((f, d))],
        out_specs=pl.BlockSpec((rows, d), lambda i, be, bu: (i, 0)),
    )
    return pl.pallas_call(
        _moe_kernel,
        grid_spec=grid_spec,
        out_shape=jax.ShapeDtypeStruct((n_rows, d), BF16),
        compiler_params=_cp("arbitrary"),
        name="moe_ffn",
    )(block_e, block_used, xs, w1, w3, w2)


def _combine_kernel(s0_ref, hoff_ref, cnt_ref, ys_ref, route_ref, x_ref, mod_ref, pg_ref, out_ref,
                    buf_ref, acc_ref, sem, *, nti, nt):
    t = pl.program_id(0) * nti + pl.program_id(1)
    slot = lax.rem(t, 2)
    half = MOE_TILE // 2

    @pl.when(t == 0)
    def _():
        for c in _window_copies(ys_ref, buf_ref, sem, s0_ref, 0, 0, False):
            c.start()

    @pl.when(t + 1 < nt)
    def _():
        for c in _window_copies(ys_ref, buf_ref, sem, s0_ref, t + 1, 1 - slot, False):
            c.start()

    for c in _window_copies(ys_ref, buf_ref, sem, s0_ref, t, slot, False):
        c.wait()

    route = route_ref[...]
    sr = lax.broadcasted_iota(jnp.int32, (LANES, N_EXPERTS * LANES), 0)
    sc = lax.broadcasted_iota(jnp.int32, (LANES, N_EXPERTS * LANES), 1)
    spread = (sc // LANES == sr).astype(BF16)
    rank_b = _dot(route[:, 0:LANES], spread)
    gate_b = _dot(route[:, LANES:2 * LANES], spread) + _dot(route[:, 2 * LANES:3 * LANES], spread)
    second = lax.broadcasted_iota(jnp.int32, (MOE_TILE, LANES), 0) >= half
    scol = lax.broadcasted_iota(jnp.int32, (MOE_TILE, LANES), 1).astype(F32)

    def onehot(lr, first):
        return jnp.concatenate([(scol + float(first + c) == lr) for c in range(0, half, LANES)],
                               axis=1).astype(BF16)

    for e in range(N_EXPERTS):
        rk = rank_b[:, e * LANES:(e + 1) * LANES]
        off = jnp.where(second, hoff_ref[t * N_EXPERTS + e].astype(F32), 0.0)
        lr = jnp.where(rk < 0.0, -1.0, rk + off)
        ge = jnp.concatenate([gate_b[:, e * LANES:(e + 1) * LANES]] * (D_MODEL // LANES), axis=1)
        z = ge * _dot(onehot(lr, 0), buf_ref[slot, e, 0:half, :])
        if e == 0:
            acc_ref[...] = z
        else:
            acc_ref[...] += z

        @pl.when(cnt_ref[t * N_EXPERTS + e] > half)
        def _():
            acc_ref[...] += ge * _dot(onehot(lr, half), buf_ref[slot, e, half:MOE_WIN, :])

    out_ref[...] = x_ref[...] + mod_ref[5:6, :] * _rms(acc_ref[...], pg_ref[...])


def _combine(s0, hoff, cnt, ys, route, x3, mod1, post_g):
    b, t, d = x3.shape
    nti = t // MOE_TILE
    row = lambda w: pl.BlockSpec((None, MOE_TILE, w), lambda bi, i, *_: (bi, i, 0))
    grid_spec = pltpu.PrefetchScalarGridSpec(
        num_scalar_prefetch=3,
        grid=(b, nti),
        in_specs=[pl.BlockSpec(memory_space=pl.ANY), row(3 * LANES), row(d),
                  pl.BlockSpec((None, 6, d), lambda bi, i, *_: (bi, 0, 0)),
                  pl.BlockSpec((1, d), lambda bi, i, *_: (0, 0))],
        out_specs=row(d),
        scratch_shapes=[pltpu.VMEM((2, N_EXPERTS, MOE_WIN, d), BF16), pltpu.VMEM((MOE_TILE, d), F32),
                        pltpu.SemaphoreType.DMA((2, N_EXPERTS))],
    )
    return pl.pallas_call(
        functools.partial(_combine_kernel, nti=nti, nt=b * nti),
        grid_spec=grid_spec,
        out_shape=jax.ShapeDtypeStruct((b, t, d), F32),
        compiler_params=_cp("arbitrary", "arbitrary"),
        name="moe_combine",
    )(s0, hoff, cnt, ys, route, x3, mod1, post_g.reshape(1, d))


def _rope_tables(t_len):
    f32 = np.float32
    rows = t_len // GRID_W
    row = np.repeat(np.arange(rows, dtype=f32), GRID_W)
    col = np.tile(np.arange(GRID_W, dtype=f32), rows)
    axis_dim = DIFF_DQK // 2
    inv = (f32(ROPE_BASE) ** (-np.arange(0, axis_dim, 2, dtype=f32) / f32(axis_dim))).astype(f32)
    ang_r = (row[:, None] * inv).astype(f32)
    ang_c = (col[:, None] * inv).astype(f32)
    cr, sr, cc, sc = np.cos(ang_r), np.sin(ang_r), np.cos(ang_c), np.sin(ang_c)
    cos = np.concatenate([cr, cr, cc, cc] * 2, axis=1).astype(f32)
    sin = np.concatenate([-sr, sr, -sc, sc] * 2, axis=1).astype(f32)
    return jnp.asarray(cos), jnp.asarray(sin)


def kernel(x, c, ctx, c_ctx, l0_mod_w, l0_mod_b, l0_mix_pre_g, l0_mix_post_g, l0_w_in, l0_mlstm_gate_b, l0_mlstm_conv_w, l0_mlstm_norm_g, l0_lambda_q1, l0_lambda_k1, l0_lambda_q2, l0_lambda_k2, l0_diff_norm_g, l0_w_out, l0_ffn_pre_g, l0_ffn_post_g, l0_ffn_w1, l0_ffn_w3, l0_ffn_w2, l1_mod_w, l1_mod_b, l1_mix_pre_g, l1_mix_post_g, l1_conv_pw1_w, l1_conv_pw1_b, l1_conv_dw_w, l1_conv_dw_b, l1_conv_ln_g, l1_conv_ln_b, l1_conv_pw2_w, l1_conv_pw2_b, l1_ffn_pre_g, l1_ffn_post_g, l1_router_w, l1_moe_w1, l1_moe_w3, l1_moe_w2):
    b, t, d = x.shape
    n_ctx = ctx.shape[1]
    L = MLSTM_CHUNK
    assert d == D_MODEL and n_ctx == L and t % (2 * L) == 0 and b <= 8

    cpad = jnp.zeros((16, d), F32).at[:b].set(c).at[8].set(c_ctx)
    mod0 = _adaln(cpad, l0_mod_w, l0_mod_b).reshape(16, 6, d)
    mod1 = _adaln(cpad, l1_mod_w, l1_mod_b).reshape(16, 6, d)

    w_perm = jnp.concatenate(
        [l0_w_in[:, :2048], l0_w_in[:, 2080:], l0_w_in[:, 2048:2080], jnp.zeros((d, 96), F32)], axis=1).astype(BF16)
    gate_b = jnp.concatenate([l0_mlstm_gate_b, jnp.zeros((96,), F32)]).reshape(1, LANES)
    cos, sin = _rope_tables(t)
    g0 = l0_mix_pre_g.reshape(1, d)
    mq, mk, mv, mo, dq, dk, dv, gates = _inproj(x, mod0, None, g0, w_perm, l0_mlstm_conv_w, gate_b, cos, sin,
                                                 tm=ROW_TILE, rope=True)
    _, mkc, mvc, _, _, dkc, dvc, gates_c = _inproj(ctx, mod0, 8, g0, w_perm, l0_mlstm_conv_w, gate_b,
                                                   cos[:n_ctx], sin[:n_ctx], tm=n_ctx, rope=False)

    arow, cols = _gateprep(gates_c, gates, L=L)
    hm = _mlstm(mq, mk, mv, mo, mkc, mvc, arow, cols, l0_mlstm_norm_g, L=L)

    lam_init = 0.8 - 0.6 * math.exp(-0.3 * 0)
    lam_in = jnp.zeros((8, LANES), F32).at[0, :DIFF_DQK].set(l0_lambda_q1).at[1, :DIFF_DQK].set(l0_lambda_k1)
    lam_in = lam_in.at[2, :DIFF_DQK].set(l0_lambda_q2).at[3, :DIFF_DQK].set(l0_lambda_k2)
    hd = _attn(dq, dk, dv, dkc, dvc, lam_in, l0_diff_norm_g, tq=min(ATTN_Q_TILE, t), sub=MXU_DIM,
               max_sub_per_step=t // MXU_DIM, lam_init=lam_init)

    w_out = l0_w_out.astype(BF16)
    x2, h1 = _ffn(hm, hd, w_out[:512], w_out[512:], l0_mix_post_g, l0_ffn_pre_g,
                  l0_ffn_w1.astype(BF16), l0_ffn_w3.astype(BF16), l0_ffn_w2.astype(BF16), x, mod0, mod1,
                  l0_ffn_post_g, l1_mix_pre_g, l1_conv_pw1_w.astype(BF16), l1_conv_pw1_b.reshape(1, -1),
                  tm=ROW_TILE)

    dw = jnp.concatenate([l1_conv_dw_w, jnp.zeros((1, d), F32)], axis=0)
    rw = jnp.concatenate([l1_router_w, jnp.zeros((d, LANES - N_EXPERTS), F32)], axis=1).astype(BF16)
    v1 = lambda a: a.reshape(1, -1)
    conv_tm = MOE_TILE // 2
    x3, u4, route, routet, cnt_tile = _convmod(
        h1, dw, v1(l1_conv_dw_b), v1(l1_conv_ln_g),
        v1(l1_conv_ln_b), l1_conv_pw2_w.astype(BF16), v1(l1_conv_pw2_b), x2, mod1, v1(l1_mix_post_g),
        v1(l1_ffn_pre_g), rw, tm=conv_tm)

    n = b * t
    rows = MOE_ROWS
    nt = n // MOE_TILE
    cnt_half = cnt_tile[:, 0, :N_EXPERTS].astype(jnp.int32).reshape(nt, 2, N_EXPERTS)
    cnt = cnt_half[:, 0] + cnt_half[:, 1]
    aligned = ((cnt + MOE_ALIGN - 1) // MOE_ALIGN) * MOE_ALIGN
    base = jnp.cumsum(aligned, axis=0) - aligned
    cap = ((base[-1] + MOE_WIN + rows - 1) // rows) * rows
    pend = jnp.cumsum(cap)
    s0 = (pend - cap)[None, :] + base
    n_blk = (2 * n + N_EXPERTS * (MOE_ALIGN * nt + MOE_WIN + rows)) // rows + 1
    blk_row = jnp.arange(n_blk, dtype=jnp.int32) * rows
    block_e = jnp.minimum(jnp.sum((blk_row[:, None] >= pend[None, :]).astype(jnp.int32), axis=1), N_EXPERTS - 1)
    n_used = (pend[-1:] // rows).astype(jnp.int32)
    slot_end = (pend - cap) + base[-1] + aligned[-1]
    block_used = ((blk_row < slot_end[block_e]) & (blk_row < pend[-1])).astype(jnp.int32)
    flat = lambda a: a.reshape(-1).astype(jnp.int32)

    xs = _dispatch(u4.reshape(n, d), routet, flat(s0), flat(cnt_half[:, 0]), flat(cnt), flat(pend - MOE_WIN),
                   n_used, n_rows=n_blk * rows)
    ys = _moe_ffn(xs, block_e, block_used, l1_moe_w1.astype(BF16), l1_moe_w3.astype(BF16),
                  l1_moe_w2.astype(BF16), rows=rows)
    return _combine(flat(s0), flat(cnt_half[:, 0]), flat(cnt), ys, route, x3, mod1, l1_ffn_post_g)
```
